```python
import math
import jax, jax.numpy as jnp
from jax import lax
import numpy as np

D_MODEL = 1024
BATCH = 16
SEQ = 4096
DEPTH = 4

CHUNK = 64
HEAD_DIM = 64
EPS = 1e-6

H_A = 6
D_A = H_A * HEAD_DIM
N_IDX_HEADS = 8
D_IDX = 64
TOPK_MAX = 256
Q_BLOCK = CHUNK
N_BUCKETS = 32
MAX_DISTANCE = 1024

D_B = 256
CONV_WIDTH = 31

H_C = 6
D_C = H_C * HEAD_DIM
D_DECAY_LORA = 32
D_AAA_LORA = 32
LNX_EPS = 64e-5

D_MIX = D_A + D_B + D_C

A_SIZES = (D_A, D_A, D_A, D_A, N_IDX_HEADS * D_IDX, D_IDX, N_IDX_HEADS)
B_SIZES = (D_B, D_B, D_B)
C_SHIFT_SIZES = (D_C, D_C, D_C, D_DECAY_LORA, D_AAA_LORA)
A_COLS = sum(A_SIZES)
B_COLS = sum(B_SIZES)
C_SHIFT_COLS = sum(C_SHIFT_SIZES)
C_COLS = C_SHIFT_COLS + D_C
N_IN = A_COLS + B_COLS + C_COLS

kernel_name = "hybrid_dsa_conformer_rwkv7_chunk_causal"


def _split(t, sizes):
    return jnp.split(t, [int(i) for i in np.cumsum(sizes)[:-1]], axis=-1)


def rms_norm(x, g, eps=EPS):
    xf = x.astype(jnp.float32)
    y = xf * lax.rsqrt(jnp.mean(xf * xf, axis=-1, keepdims=True) + eps)
    return (y * g.astype(jnp.float32)).astype(x.dtype)


def layer_norm(x, g, b, eps):
    xf = x.astype(jnp.float32)
    mu = jnp.mean(xf, axis=-1, keepdims=True)
    var = jnp.mean(jnp.square(xf - mu), axis=-1, keepdims=True)
    y = (xf - mu) * lax.rsqrt(var + eps)
    return (y * g.astype(jnp.float32) + b.astype(jnp.float32)).astype(x.dtype)


def t5_bucket(rel):
    nb = N_BUCKETS // 2
    max_exact = nb // 2
    ret = jnp.where(rel > 0, nb, 0)
    n = jnp.abs(rel)
    nf = jnp.maximum(n, 1).astype(jnp.float32)
    large = max_exact + (jnp.log(nf / max_exact) / math.log(MAX_DISTANCE / max_exact)
                         * (nb - max_exact)).astype(jnp.int32)
    large = jnp.minimum(large, nb - 1)
    return ret + jnp.where(n < max_exact, n, large)


def dsa_mixer(pa, q_g, k_g, rel_bias):
    B, S, _ = pa.shape
    q, k, v, gate, qi, ki, wi = _split(pa, A_SIZES)
    q = rms_norm(q.reshape(B, S, H_A, HEAD_DIM), q_g)
    k = rms_norm(k.reshape(B, S, H_A, HEAD_DIM), k_g)
    v = v.reshape(B, S, H_A, HEAD_DIM)
    ki = ki.astype(jnp.float32)
    top_k = min(TOPK_MAX, S // 4)
    nb = S // Q_BLOCK
    key_pos = jnp.arange(S, dtype=jnp.int32)
    scale = HEAD_DIM ** -0.5

    def block(args):
        blk, qb, qib, wb = args
        q_pos = blk * Q_BLOCK + jnp.arange(Q_BLOCK, dtype=jnp.int32)
        q_end = (q_pos // CHUNK + 1) * CHUNK
        dots = jnp.einsum('bqhd,bsd->bqhs', qib.astype(jnp.float32), ki)
        score = jnp.einsum('bqhs,bqh->bqs', jax.nn.relu(dots), wb.astype(jnp.float32))
        score = jnp.where(key_pos[None, None, :] < q_end[None, :, None], score, -jnp.inf)
        _, idx = lax.top_k(score, top_k)
        valid = idx < q_end[None, :, None]
        k_sel = jax.vmap(lambda kb, ib: kb[ib])(k, idx)
        v_sel = jax.vmap(lambda vb, ib: vb[ib])(v, idx)
        logits = jnp.einsum('bqhd,bqkhd->bhqk', qb, k_sel).astype(jnp.float32) * scale
        bias = rel_bias[t5_bucket(idx - q_pos[None, :, None])]
        logits = logits + jnp.transpose(bias.astype(jnp.float32), (0, 3, 1, 2))
        logits = jnp.where(valid[:, None], logits, -jnp.inf)
        p = jax.nn.softmax(logits, axis=-1).astype(v.dtype)
        return jnp.einsum('bhqk,bqkhd->bqhd', p, v_sel)

    qb = q.reshape(B, nb, Q_BLOCK, H_A, HEAD_DIM).transpose(1, 0, 2, 3, 4)
    qib = qi.reshape(B, nb, Q_BLOCK, N_IDX_HEADS, D_IDX).transpose(1, 0, 2, 3, 4)
    wb = wi.reshape(B, nb, Q_BLOCK, N_IDX_HEADS).transpose(1, 0, 2, 3)
    out = lax.map(block, (jnp.arange(nb, dtype=jnp.int32), qb, qib, wb))
    y = out.transpose(1, 0, 2, 3, 4).reshape(B, S, D_A)
    return y * jax.nn.silu(gate)


def conformer_conv_mixer(pb, conv_w, conv_b, ln_g, ln_b):
    val, glu, gate = _split(pb, B_SIZES)
    u = val * jax.nn.sigmoid(glu)
    u = jnp.pad(u, ((0, 0), (CONV_WIDTH - 1, 0), (0, 0)))
    u = lax.conv_general_dilated(u, conv_w[:, None, :].astype(u.dtype), window_strides=(1,),
                                 padding='VALID', dimension_numbers=('NWC', 'WIO', 'NWC'),
                                 feature_group_count=D_B) + conv_b
    u = layer_norm(u, ln_g, ln_b, EPS)
    return jax.nn.silu(u) * jax.nn.silu(gate)


def rwkv7_mixer(pc, mu, w0, w_up, a0, a_up, k_k, k_a, r_k, lnx_g, lnx_b):
    B, S, _ = pc.shape
    f32 = jnp.float32
    ps, gate = pc[..., :C_SHIFT_COLS], pc[..., C_SHIFT_COLS:]
    prev = jnp.pad(ps, ((0, 0), (1, 0), (0, 0)))[:, :-1]
    ps = ps + mu * (prev - ps)
    r, k, v, wd, ad = _split(ps, C_SHIFT_SIZES)
    w_log = -jax.nn.softplus(-(w0 + jnp.tanh(wd) @ w_up)) - 0.5
    decay = jnp.exp(-jnp.exp(w_log.astype(f32)))
    a = jax.nn.sigmoid(a0 + ad @ a_up)
    kk = (k * k_k).astype(f32).reshape(B, S, H_C, HEAD_DIM)
    kk = kk * lax.rsqrt(jnp.maximum(jnp.sum(kk * kk, -1, keepdims=True), 1e-12))
    k = k * (1 + (a - 1) * k_a)
    hs = lambda t: t.astype(f32).reshape(B, S, H_C, HEAD_DIM)
    r, k, v, a, decay = hs(r), hs(k), hs(v), hs(a), hs(decay)
    b_vec = kk * a

    def step(state, inp):
        r_t, w_t, k_t, v_t, kk_t, b_t = inp
        sa = -jnp.einsum('bhvk,bhk->bhv', state, kk_t)
        state = (state * w_t[:, :, None, :] + sa[..., None] * b_t[:, :, None, :]
                 + v_t[..., None] * k_t[:, :, None, :])
        return state, jnp.einsum('bhvk,bhk->bhv', state, r_t)

    xs = tuple(t.transpose(1, 0, 2, 3) for t in (r, decay, k, v, kk, b_vec))
    state0 = jnp.zeros((B, H_C, HEAD_DIM, HEAD_DIM), f32)
    _, y = lax.scan(step, state0, xs)
    y = y.transpose(1, 0, 2, 3)
    mu_y = jnp.mean(y, -1, keepdims=True)
    var_y = jnp.mean(jnp.square(y - mu_y), -1, keepdims=True)
    y = ((y - mu_y) * lax.rsqrt(var_y + LNX_EPS) * lnx_g.astype(f32).reshape(H_C, HEAD_DIM)
         + lnx_b.astype(f32).reshape(H_C, HEAD_DIM))
    bonus = jnp.sum(r * k * r_k.astype(f32), -1, keepdims=True) * v
    y = (y + bonus).reshape(B, S, D_C).astype(pc.dtype)
    return y * jax.nn.silu(gate)


def setup_inputs(seed: int = 0) -> dict:
    key = jax.random.key(seed)
    ks = jax.random.split(key, 26)
    f = jnp.float32
    L = DEPTH

    def nrm(k, shape, s):
        return jax.random.normal(k, shape, f) * s

    return {
        "x": nrm(ks[0], (BATCH, SEQ, D_MODEL), 1.0),
        "c": nrm(ks[1], (BATCH, D_MODEL), 1.0),
        "norm_g": 1.0 + nrm(ks[2], (L, D_MODEL), 0.05),
        "w_ada": nrm(ks[3], (L, D_MODEL, 3 * D_MODEL), 0.5 * D_MODEL ** -0.5),
        "b_ada": nrm(ks[4], (L, 3 * D_MODEL), 0.02),
        "w_in": nrm(ks[5], (L, D_MODEL, N_IN), D_MODEL ** -0.5),
        "w_out": nrm(ks[6], (L, D_MIX, D_MODEL), D_MIX ** -0.5),
        "q_norm_g": 1.0 + nrm(ks[7], (L, HEAD_DIM), 0.05),
        "k_norm_g": 1.0 + nrm(ks[8], (L, HEAD_DIM), 0.05),
        "rel_bias": nrm(ks[9], (N_BUCKETS, H_A), 0.5),
        "conv_w": nrm(ks[10], (L, CONV_WIDTH, D_B), CONV_WIDTH ** -0.5),
        "conv_b": nrm(ks[11], (L, D_B), 0.02),
        "conv_ln_g": 1.0 + nrm(ks[12], (L, D_B), 0.05),
        "conv_ln_b": nrm(ks[13], (L, D_B), 0.02),
        "shift_mu": jax.random.uniform(ks[14], (L, C_SHIFT_COLS), f),
        "decay_w0": nrm(ks[15], (L, D_C), 1.0),
        "decay_up": nrm(ks[16], (L, D_DECAY_LORA, D_C), 0.1),
        "iclr_a0": nrm(ks[17], (L, D_C), 0.5),
        "iclr_up": nrm(ks[18], (L, D_AAA_LORA, D_C), 0.1),
        "key_k": 0.85 + nrm(ks[19], (L, D_C), 0.05),
        "key_a": 1.0 + nrm(ks[20], (L, D_C), 0.05),
        "bonus_r_k": nrm(ks[21], (L, H_C, HEAD_DIM), 0.1),
        "lnx_g": 1.0 + nrm(ks[22], (L, D_C), 0.05),
        "lnx_b": nrm(ks[23], (L, D_C), 0.02),
    }


def reference(x, c, norm_g, w_ada, b_ada, w_in, w_out, q_norm_g, k_norm_g, rel_bias,
              conv_w, conv_b, conv_ln_g, conv_ln_b, shift_mu, decay_w0, decay_up,
              iclr_a0, iclr_up, key_k, key_a, bonus_r_k, lnx_g, lnx_b):
    c_act = jax.nn.silu(c)
    for l in range(DEPTH):
        mod = c_act @ w_ada[l] + b_ada[l]
        shift, scale, gate = jnp.split(mod, 3, axis=-1)
        h = rms_norm(x, norm_g[l]) * (1 + scale[:, None, :]) + shift[:, None, :]
        p = h @ w_in[l]
        pa, pb, pc = _split(p, (A_COLS, B_COLS, C_COLS))
        ya = dsa_mixer(pa, q_norm_g[l], k_norm_g[l], rel_bias)
        yb = conformer_conv_mixer(pb, conv_w[l], conv_b[l], conv_ln_g[l], conv_ln_b[l])
        yc = rwkv7_mixer(pc, shift_mu[l], decay_w0[l], decay_up[l], iclr_a0[l], iclr_up[l],
                         key_k[l], key_a[l], bonus_r_k[l], lnx_g[l], lnx_b[l])
        y = jnp.concatenate([ya, yb, yc], axis=-1) @ w_out[l]
        x = x + gate[:, None, :] * y
    return x
```

```python
import functools
import math

import jax
import jax.numpy as jnp
import numpy as np
from jax import lax
from jax.experimental import pallas as pl
from jax.experimental.pallas import tpu as pltpu

F32 = jnp.float32
BF16 = jnp.bfloat16

HEAD_DIM = 64
CHUNK = 64
EPS = 1e-6
H_A = 6
D_A = H_A * HEAD_DIM
N_IDX_HEADS = 8
D_IDX = 64
TOPK_MAX = 256
N_BUCKETS = 32
MAX_DISTANCE = 1024
D_B = 256
CONV_WIDTH = 31
H_C = 6
D_C = H_C * HEAD_DIM
D_LORA = 32
LNX_EPS = 64e-5
D_MIX = D_A + D_B + D_C

LANES = 128
TILE = 128
N_FAR = 9
VMEM_LIMIT = 48 * 1024 * 1024

C_Q, C_K, C_V = 0, 384, 768
C_QI = 1152
C_KI = 1664
C_WI = 1792
C_GATES = 1920
C_CONV = 2944
C_RW = 3456
N_COLS = 4736
RW_COLS = 1280

INT_MIN = -(2 ** 31)
INT_MAX = 2 ** 31 - 1
NEG_KEY = int(np.array(-np.inf, np.float32).view(np.int32)) ^ 0x7FFFFFFF
NEG_BIG = -1e30

NT_DIMS = (((1,), (1,)), ((), ()))
TN_DIMS = (((0,), (0,)), ((), ()))


def _bdot(a, b):
    return jnp.dot(a.astype(BF16), b.astype(BF16), preferred_element_type=F32)


def _bdot_nt(a, b):
    return lax.dot_general(a.astype(BF16), b.astype(BF16), NT_DIMS, preferred_element_type=F32)


def _sigmoid(x):
    return 1.0 / (1.0 + jnp.exp(-x))


def _silu(x):
    return x * _sigmoid(x)


def _mod_kernel(c_ref, w_ref, b_ref, o_ref):
    ca = _silu(c_ref[...])
    o_ref[0] = jnp.dot(ca, w_ref[0], precision=lax.Precision.HIGHEST,
                       preferred_element_type=F32) + b_ref[0]


def _modulation(c, w_ada, b_ada):
    depth, d, d3 = w_ada.shape
    b = c.shape[0]
    return pl.pallas_call(
        _mod_kernel,
        grid=(depth, d3 // d),
        in_specs=[pl.BlockSpec((b, d), lambda l, j: (0, 0)),
                  pl.BlockSpec((1, d, d), lambda l, j: (l, 0, j)),
                  pl.BlockSpec((1, 1, d), lambda l, j: (l, 0, j))],
        out_specs=pl.BlockSpec((1, b, d), lambda l, j: (l, 0, j)),
        out_shape=jax.ShapeDtypeStruct((depth, b, d3), F32),
        compiler_params=pltpu.CompilerParams(
            dimension_semantics=("arbitrary", "arbitrary"), vmem_limit_bytes=VMEM_LIMIT),
    )(c, w_ada, b_ada.reshape(depth, 1, d3))


def _inproj_kernel(x_ref, sc_ref, sh_ref, g_ref, w_ref, avg_ref, qg_ref, kg_ref,
                   q_ref, k_ref, v_ref, qi_ref, ki_ref, wi_ref, gates_ref, conv_ref, rw_ref):
    x = x_ref[...]
    ms = jnp.mean(x * x, axis=-1, keepdims=True)
    h = x * lax.rsqrt(ms + EPS) * g_ref[...]
    hb = (h * sc_ref[0] + sh_ref[0]).astype(BF16)

    def mm(c0, c1):
        return jnp.dot(hb, w_ref[:, c0:c1], preferred_element_type=F32)

    def head_rms(t, g):
        m2 = jnp.dot((t * t).astype(BF16), avg_ref[...], preferred_element_type=F32)
        return t * lax.rsqrt(m2 + EPS) * g

    q_ref[...] = head_rms(mm(C_Q, C_Q + D_A), qg_ref[...]).astype(BF16)
    k_ref[...] = head_rms(mm(C_K, C_K + D_A), kg_ref[...]).astype(BF16)
    v_ref[...] = mm(C_V, C_V + D_A).astype(BF16)
    qi_ref[...] = mm(C_QI, C_QI + 512).astype(BF16)
    ki_ref[...] = mm(C_KI, C_KI + LANES).astype(BF16)
    wi_ref[...] = mm(C_WI, C_WI + LANES)[:, :N_IDX_HEADS]
    gates_ref[:, 0:512] = mm(C_GATES, C_GATES + 512)
    gates_ref[:, 512:1024] = mm(C_GATES + 512, C_GATES + 1024)
    conv_ref[...] = mm(C_CONV, C_CONV + 512)
    rw_ref[:, 0:512] = mm(C_RW, C_RW + 512)
    rw_ref[:, 512:1024] = mm(C_RW + 512, C_RW + 1024)
    rw_ref[:, 1024:RW_COLS] = mm(C_RW + 1024, C_RW + RW_COLS)


def _inproj(x2, sc1p, shift, norm_g, w, avg, qg, kg, seq, tm):
    n, d = x2.shape
    per_b = seq // tm
    row = lambda i: (i, 0)
    full = lambda i: (0, 0)
    bat = lambda i: (i // per_b, 0, 0)
    widths = (D_A, D_A, D_A, 512, LANES, N_IDX_HEADS, D_MIX, 512, RW_COLS)
    dtypes = (BF16, BF16, BF16, BF16, BF16, F32, F32, F32, F32)
    return pl.pallas_call(
        _inproj_kernel,
        grid=(n // tm,),
        in_specs=[pl.BlockSpec((tm, d), row),
                  pl.BlockSpec((1, 1, d), bat),
                  pl.BlockSpec((1, 1, d), bat),
                  pl.BlockSpec((1, d), full),
                  pl.BlockSpec((d, N_COLS), full),
                  pl.BlockSpec((D_A, D_A), full),
                  pl.BlockSpec((1, D_A), full),
                  pl.BlockSpec((1, D_A), full)],
        out_specs=[pl.BlockSpec((tm, wd), row) for wd in widths],
        out_shape=[jax.ShapeDtypeStruct((n, wd), dt) for wd, dt in zip(widths, dtypes)],
        compiler_params=pltpu.CompilerParams(
            dimension_semantics=("arbitrary",), vmem_limit_bytes=VMEM_LIMIT),
    )(x2, sc1p, shift, norm_g, w, avg, qg, kg)


def _sortable(s):
    b = lax.bitcast_convert_type(s, jnp.int32)
    return b ^ ((b >> 31) & jnp.int32(0x7FFFFFFF))


def _attn_kernel(q_ref, k_ref, v_ref, qi_ref, ki_ref, wi_ref, bias_ref, o_ref,
                 key_scr, wb_scr, qih_scr, qh_scr, m_scr, l_scr, acc_scr, *, topk, pos_bits):
    tq = TILE
    qb = pl.program_id(1)
    nkt = qb + 1
    kf = float(topk)
    lane = lax.broadcasted_iota(jnp.int32, (tq, LANES), 1)
    lo_half = lane < HEAD_DIM

    wi = wi_ref[0]
    qi = qi_ref[0]
    for h in range(N_IDX_HEADS):
        wb_scr[h] = jnp.broadcast_to(wi[:, h:h + 1], (tq, LANES))
        pair = qi[:, (h // 2) * LANES:(h // 2 + 1) * LANES]
        keep = lo_half if h % 2 == 0 else jnp.logical_not(lo_half)
        qih_scr[h] = jnp.where(keep, pair, jnp.zeros_like(pair))
    q = q_ref[0]
    for h in range(H_A):
        pair = q[:, (h // 2) * LANES:(h // 2 + 1) * LANES]
        keep = lo_half if h % 2 == 0 else jnp.logical_not(lo_half)
        qh_scr[h] = jnp.where(keep, pair, jnp.zeros_like(pair))

    def score_keys(kt):
        kit = ki_ref[0, pl.ds(pl.multiple_of(kt * TILE, TILE), TILE), :]
        s = None
        for h in range(N_IDX_HEADS):
            d = lax.dot_general(qih_scr[h], kit, NT_DIMS, preferred_element_type=F32)
            t = jnp.maximum(d, 0.0) * wb_scr[h]
            s = t if s is None else s + t
        return _sortable(s)

    def p1(kt, c):
        key_scr[kt] = score_keys(kt)
        return c

    lax.fori_loop(0, qb, p1, 0)
    row_chunk = lax.broadcasted_iota(jnp.int32, (tq, LANES), 0) // CHUNK
    adm = (lane // CHUNK) <= row_chunk
    key_scr[qb] = jnp.where(adm, score_keys(qb), NEG_KEY)

    def count_ge(thr_b):
        def body(kt, c):
            return c + jnp.where(key_scr[kt] >= thr_b, 1.0, 0.0)
        c = lax.fori_loop(0, nkt, body, jnp.zeros((tq, LANES), F32))
        return jnp.sum(c, axis=1, keepdims=True)

    def bisect(_, lohi):
        lo, hi = lohi
        mid = (lo >> 1) + (hi >> 1) + (lo & hi & 1)
        ge = count_ge(jnp.broadcast_to(mid, (tq, LANES))) >= kf
        return jnp.where(ge, mid, lo), jnp.where(ge, hi, mid)

    lo0 = jnp.full((tq, 1), INT_MIN, jnp.int32)
    hi0 = jnp.full((tq, 1), INT_MAX, jnp.int32)
    thr, _ = lax.fori_loop(0, 32, bisect, (lo0, hi0))
    thr_b = jnp.broadcast_to(thr, (tq, LANES))
    cnt_ge = count_ge(thr_b)
    tie = cnt_ge > kf

    @pl.when(jnp.max(jnp.where(tie, 1.0, 0.0)) > 0.0)
    def _():
        need = kf - count_ge(thr_b + 1)

        def count_eq_le(pm_b):
            def body(kt, c):
                hit = jnp.where(key_scr[kt] == thr_b, lane + kt * TILE, INT_MAX) <= pm_b
                return c + jnp.where(hit, 1.0, 0.0)
            c = lax.fori_loop(0, nkt, body, jnp.zeros((tq, LANES), F32))
            return jnp.sum(c, axis=1, keepdims=True)

        def pbisect(_, lohi):
            plo, phi = lohi
            pmid = (plo + phi) >> 1
            ok = count_eq_le(jnp.broadcast_to(pmid, (tq, LANES))) >= need
            return jnp.where(ok, plo, pmid), jnp.where(ok, pmid, phi)

        plo0 = jnp.full((tq, 1), -1, jnp.int32)
        phi0 = jnp.full((tq, 1), (1 << pos_bits) - 1, jnp.int32)
        _, pthr = lax.fori_loop(0, pos_bits + 1, pbisect, (plo0, phi0))
        pthr_b = jnp.broadcast_to(jnp.where(tie, pthr, INT_MAX), (tq, LANES))

        def drop(kt, c):
            kk = key_scr[kt]
            pos = jnp.where(kk == thr_b, lane + kt * TILE, -1)
            key_scr[kt] = jnp.where(pos > pthr_b, NEG_KEY, kk)
            return c

        lax.fori_loop(0, nkt, drop, 0)

    thr_eff_b = jnp.maximum(thr_b, NEG_KEY + 1)

    for h in range(H_A):
        m_scr[h] = jnp.full((tq, LANES), NEG_BIG, F32)
        l_scr[h] = jnp.zeros((tq, LANES), F32)
        acc_scr[h] = jnp.zeros((tq, LANES), F32)

    def p3(kt, c):
        off = pl.multiple_of(kt * TILE, TILE)
        mb = jnp.where(key_scr[kt] >= thr_eff_b, 0.0, NEG_BIG)
        k_t = k_ref[0, pl.ds(off, TILE), :]
        v_t = v_ref[0, pl.ds(off, TILE), :]
        dd = jnp.maximum(kt - qb, -N_FAR) + N_FAR
        for h in range(H_A):
            g = h // 2
            s = lax.dot_general(qh_scr[h], k_t[:, g * LANES:(g + 1) * LANES], NT_DIMS,
                                preferred_element_type=F32)
            s = s + bias_ref[h * (N_FAR + 1) + dd] + mb
            m_prev = m_scr[h]
            m_new = jnp.maximum(m_prev, jnp.max(s, axis=1, keepdims=True))
            alpha = jnp.exp(m_prev - m_new)
            p = jnp.exp(s - m_new)
            l_scr[h] = alpha * l_scr[h] + jnp.sum(p, axis=1, keepdims=True)
            acc_scr[h] = alpha * acc_scr[h] + jnp.dot(
                p.astype(BF16), v_t[:, g * LANES:(g + 1) * LANES], preferred_element_type=F32)
            m_scr[h] = m_new
        return c

    lax.fori_loop(0, nkt, p3, 0)

    for g in range(H_A // 2):
        even = acc_scr[2 * g] / l_scr[2 * g]
        odd = acc_scr[2 * g + 1] / l_scr[2 * g + 1]
        o_ref[0, :, g * LANES:(g + 1) * LANES] = jnp.where(lo_half, even, odd)


def _attention(q, k, v, qi, ki, wi, bias_tiles, topk):
    b, s, _ = q.shape
    nt = s // TILE
    kern = functools.partial(_attn_kernel, topk=topk, pos_bits=max(1, int(math.ceil(math.log2(s)))))
    qtile = lambda bb, i: (bb, i, 0)
    whole = lambda bb, i: (bb, 0, 0)
    return pl.pallas_call(
        kern,
        grid=(b, nt),
        in_specs=[pl.BlockSpec((1, TILE, D_A), qtile),
                  pl.BlockSpec((1, s, D_A), whole),
                  pl.BlockSpec((1, s, D_A), whole),
                  pl.BlockSpec((1, TILE, 512), qtile),
                  pl.BlockSpec((1, s, LANES), whole),
                  pl.BlockSpec((1, TILE, N_IDX_HEADS), qtile),
                  pl.BlockSpec(bias_tiles.shape, lambda bb, i: (0, 0, 0))],
        out_specs=pl.BlockSpec((1, TILE, D_A), qtile),
        out_shape=jax.ShapeDtypeStruct((b, s, D_A), F32),
        scratch_shapes=[pltpu.VMEM((nt, TILE, LANES), jnp.int32),
                        pltpu.VMEM((N_IDX_HEADS, TILE, LANES), F32),
                        pltpu.VMEM((N_IDX_HEADS, TILE, LANES), BF16),
                        pltpu.VMEM((H_A, TILE, LANES), BF16),
                        pltpu.VMEM((H_A, TILE, LANES), F32),
                        pltpu.VMEM((H_A, TILE, LANES), F32),
                        pltpu.VMEM((H_A, TILE, LANES), F32)],
        compiler_params=pltpu.CompilerParams(
            dimension_semantics=("arbitrary", "arbitrary"), vmem_limit_bytes=VMEM_LIMIT),
    )(q, k, v, qi, ki, wi, bias_tiles)


def _t5_bucket(rel):
    nb = N_BUCKETS // 2
    max_exact = nb // 2
    ret = jnp.where(rel > 0, nb, 0)
    n = jnp.abs(rel)
    nf = jnp.maximum(n, 1).astype(F32)
    large = max_exact + (jnp.log(nf / max_exact) / math.log(MAX_DISTANCE / max_exact)
                         * (nb - max_exact)).astype(jnp.int32)
    large = jnp.minimum(large, nb - 1)
    return ret + jnp.where(n < max_exact, n, large)


def _bias_tiles(rel_bias):
    t = jnp.arange(TILE, dtype=jnp.int32)[:, None]
    j = jnp.arange(TILE, dtype=jnp.int32)[None, :]
    tiles = [rel_bias[_t5_bucket((dd - N_FAR) * TILE + j - t)] for dd in range(N_FAR + 1)]
    arr = jnp.stack(tiles).astype(F32)
    return arr.transpose(3, 0, 1, 2).reshape(H_A * (N_FAR + 1), TILE, TILE)


CONV_HALO = 32


def _conv_kernel(cur_ref, halo_ref, w_ref, b_ref, g_ref, beta_ref, o_ref, u_scr, *, tt):
    i = pl.program_id(1)

    def glu(t):
        return t[:, :D_B] * _sigmoid(t[:, D_B:])

    u_scr[0:CONV_HALO, :] = jnp.where(i > 0, glu(halo_ref[0]), 0.0)
    u_scr[CONV_HALO:CONV_HALO + tt, :] = glu(cur_ref[0])
    rows = 64
    first = CONV_HALO - (CONV_WIDTH - 1)
    for r0 in range(0, tt, rows):
        acc = jnp.broadcast_to(b_ref[...], (rows, D_B))
        for j in range(CONV_WIDTH):
            acc = acc + w_ref[j:j + 1, :] * u_scr[r0 + first + j:r0 + first + j + rows, :]
        mu = jnp.mean(acc, axis=-1, keepdims=True)
        cen = acc - mu
        var = jnp.mean(cen * cen, axis=-1, keepdims=True)
        y = cen * lax.rsqrt(var + EPS) * g_ref[...] + beta_ref[...]
        o_ref[0, r0:r0 + rows, :] = _silu(y)


def _conv(conv_in, w, b, g, beta, tt):
    bsz, s, _ = conv_in.shape
    per = tt // CONV_HALO
    vec = lambda bb, i: (0, 0)
    return pl.pallas_call(
        functools.partial(_conv_kernel, tt=tt),
        grid=(bsz, s // tt),
        in_specs=[pl.BlockSpec((1, tt, 2 * D_B), lambda bb, i: (bb, i, 0)),
                  pl.BlockSpec((1, CONV_HALO, 2 * D_B),
                               lambda bb, i: (bb, jnp.maximum(i * per - 1, 0), 0)),
                  pl.BlockSpec((CONV_WIDTH, D_B), vec),
                  pl.BlockSpec((1, D_B), vec),
                  pl.BlockSpec((1, D_B), vec),
                  pl.BlockSpec((1, D_B), vec)],
        out_specs=pl.BlockSpec((1, tt, D_B), lambda bb, i: (bb, i, 0)),
        out_shape=jax.ShapeDtypeStruct((bsz, s, D_B), F32),
        scratch_shapes=[pltpu.VMEM((CONV_HALO + tt, D_B), F32)],
        compiler_params=pltpu.CompilerParams(
            dimension_semantics=("arbitrary", "arbitrary"), vmem_limit_bytes=VMEM_LIMIT),
    )(conv_in, conv_in, w, b, g, beta)


def _rwkv_kernel(rw_ref, mu_ref, lora_ref, w0_ref, a0_ref, kk_ref, ka_ref, rk_ref, lg_ref, lb_ref,
                 o_ref, s_scr, prev_scr):
    c = pl.program_id(1)
    n = CHUNK

    @pl.when(c == 0)
    def _():
        s_scr[...] = jnp.zeros_like(s_scr)
        prev_scr[...] = jnp.zeros_like(prev_scr)

    ps = rw_ref[0]
    row = lax.broadcasted_iota(jnp.int32, (n, 1), 0)
    prev = jnp.where(row == 0, prev_scr[0:1, :], pltpu.roll(ps, 1, axis=0))
    prev_scr[0:1, :] = ps[n - 1:n, :]
    xs = ps + mu_ref[...] * (prev - ps)
    r = xs[:, 0:D_C]
    k = xs[:, D_C:2 * D_C]
    v = xs[:, 2 * D_C:3 * D_C]
    dn = xs[:, 3 * D_C:3 * D_C + 2 * D_LORA]
    lane64 = lax.broadcasted_iota(jnp.int32, (n, 2 * D_LORA), 1)
    dn = jnp.where(lane64 < D_LORA, jnp.tanh(dn), dn)
    pre = jnp.dot(dn, lora_ref[...], precision=lax.Precision.HIGHEST, preferred_element_type=F32)
    z = -(w0_ref[...] + pre[:, :D_C])
    softplus = jnp.maximum(z, 0.0) + jnp.log(1.0 + jnp.exp(-jnp.abs(z)))
    logdec = -jnp.exp(-softplus - 0.5)
    a = _sigmoid(a0_ref[...] + pre[:, D_C:])
    kkraw = k * kk_ref[...]
    k2 = k * (1.0 + (a - 1.0) * ka_ref[...])
    bonus_pre = r * k2 * rk_ref[...]

    ri = lax.broadcasted_iota(jnp.int32, (n, n), 0)
    ci = lax.broadcasted_iota(jnp.int32, (n, n), 1)
    incl = ci <= ri
    strict = ci < ri
    tri = jnp.where(incl, 1.0, 0.0)

    for h in range(H_C):
        sl = slice(h * HEAD_DIM, (h + 1) * HEAD_DIM)
        ld, r_h, v_h, k2_h, a_h, kr_h = logdec[:, sl], r[:, sl], v[:, sl], k2[:, sl], a[:, sl], kkraw[:, sl]
        kkn = kr_h * lax.rsqrt(jnp.maximum(jnp.sum(kr_h * kr_h, axis=-1, keepdims=True), 1e-12))
        cum = jnp.dot(tri, ld, precision=lax.Precision.HIGHEST, preferred_element_type=F32)
        p_in = jnp.exp(cum)
        p_inv = jnp.exp(-cum)
        rt = r_h * p_in
        at = kkn * jnp.exp(cum - ld)
        kt = k2_h * p_inv
        bt = kkn * a_h * p_inv
        lhs = jnp.concatenate([at, rt], axis=0)
        ak = _bdot_nt(lhs, kt)
        ab = _bdot_nt(lhs, bt)
        a_ak = jnp.where(strict, ak[:n], 0.0)
        a_rk = jnp.where(incl, ak[n:], 0.0)
        a_rb = jnp.where(incl, ab[n:], 0.0)
        npow = jnp.where(strict, -ab[:n], 0.0)
        tm = npow
        for _ in range(5):
            npow = _bdot(npow, npow)
            tm = tm + npow + _bdot(tm, npow)
        rhs = _bdot(a_ak, v_h)
        wt = at + _bdot(tm, at)
        uloc = rhs + _bdot(tm, rhs)
        s0 = s_scr[h]
        u = _bdot_nt(wt, s0) + uloc
        o = _bdot_nt(rt, s0) + _bdot(a_rk, v_h) - _bdot(a_rb, u)
        mcat = jnp.concatenate([kt, bt], axis=0).astype(BF16)
        ncat = jnp.concatenate([v_h, -u], axis=0).astype(BF16)
        upd = lax.dot_general(ncat, mcat, TN_DIMS, preferred_element_type=F32)
        s_scr[h] = (s0 + upd) * p_in[n - 1:n, :]
        mu_y = jnp.mean(o, axis=-1, keepdims=True)
        cen = o - mu_y
        var = jnp.mean(cen * cen, axis=-1, keepdims=True)
        y = cen * lax.rsqrt(var + LNX_EPS) * lg_ref[:, sl] + lb_ref[:, sl]
        y = y + jnp.sum(bonus_pre[:, sl], axis=-1, keepdims=True) * v_h
        o_ref[0, :, sl] = y


def _rwkv(rw, mu, lora, w0, a0, kk, ka, rk, lg, lb):
    bsz, s, _ = rw.shape
    vec = lambda bb, i: (0, 0)
    return pl.pallas_call(
        _rwkv_kernel,
        grid=(bsz, s // CHUNK),
        in_specs=[pl.BlockSpec((1, CHUNK, RW_COLS), lambda bb, i: (bb, i, 0)),
                  pl.BlockSpec((1, RW_COLS), vec),
                  pl.BlockSpec((2 * D_LORA, 2 * D_C), vec)] +
                 [pl.BlockSpec((1, D_C), vec)] * 7,
        out_specs=pl.BlockSpec((1, CHUNK, D_C), lambda bb, i: (bb, i, 0)),
        out_shape=jax.ShapeDtypeStruct((bsz, s, D_C), F32),
        scratch_shapes=[pltpu.VMEM((H_C, HEAD_DIM, HEAD_DIM), F32),
                        pltpu.VMEM((8, RW_COLS), F32)],
        compiler_params=pltpu.CompilerParams(
            dimension_semantics=("arbitrary", "arbitrary"), vmem_limit_bytes=VMEM_LIMIT),
    )(rw, mu, lora, w0, a0, kk, ka, rk, lg, lb)


def _outproj_kernel(x_ref, attn_ref, conv_ref, rw_ref, gates_ref, gm_ref, w_ref, o_ref):
    sg = _silu(gates_ref[...])
    ya = (attn_ref[...] * sg[:, 0:D_A]).astype(BF16)
    yb = (conv_ref[...] * sg[:, D_A:D_A + D_B]).astype(BF16)
    yc = (rw_ref[...] * sg[:, D_A + D_B:D_MIX]).astype(BF16)
    y = (jnp.dot(ya, w_ref[0:D_A, :], preferred_element_type=F32)
         + jnp.dot(yb, w_ref[D_A:D_A + D_B, :], preferred_element_type=F32)
         + jnp.dot(yc, w_ref[D_A + D_B:D_MIX, :], preferred_element_type=F32))
    o_ref[...] = x_ref[...] + gm_ref[0] * y


def _outproj(x2, attn, conv, rw, gates, gate_mod, w, seq, tm):
    n, d = x2.shape
    per_b = seq // tm
    row = lambda i: (i, 0)
    return pl.pallas_call(
        _outproj_kernel,
        grid=(n // tm,),
        in_specs=[pl.BlockSpec((tm, d), row),
                  pl.BlockSpec((tm, D_A), row),
                  pl.BlockSpec((tm, D_B), row),
                  pl.BlockSpec((tm, D_C), row),
                  pl.BlockSpec((tm, D_MIX), row),
                  pl.BlockSpec((1, 1, d), lambda i: (i // per_b, 0, 0)),
                  pl.BlockSpec((D_MIX, d), lambda i: (0, 0))],
        out_specs=pl.BlockSpec((tm, d), row),
        out_shape=jax.ShapeDtypeStruct((n, d), F32),
        compiler_params=pltpu.CompilerParams(
            dimension_semantics=("arbitrary",), vmem_limit_bytes=VMEM_LIMIT),
    )(x2, attn, conv, rw, gates, gate_mod, w)


def _reorder_w_in(w_in):
    depth, d, _ = w_in.shape
    a_cols = 4 * D_A + N_IDX_HEADS * D_IDX + D_IDX + N_IDX_HEADS
    b0 = a_cols
    c0 = a_cols + 3 * D_B
    c_shift = 3 * D_C + 2 * D_LORA
    zeros = lambda m: jnp.zeros((depth, d, m), w_in.dtype)
    qi0 = 4 * D_A
    ki0 = qi0 + N_IDX_HEADS * D_IDX
    wi0 = ki0 + D_IDX
    parts = [w_in[..., 0:3 * D_A],
             w_in[..., qi0:ki0],
             w_in[..., ki0:wi0], w_in[..., ki0:wi0],
             w_in[..., wi0:a_cols], zeros(LANES - N_IDX_HEADS),
             w_in[..., 3 * D_A:4 * D_A], w_in[..., b0 + 2 * D_B:b0 + 3 * D_B],
             w_in[..., c0 + c_shift:c0 + c_shift + D_C],
             w_in[..., b0:b0 + 2 * D_B],
             w_in[..., c0:c0 + c_shift], zeros(RW_COLS - c_shift)]
    w = jnp.concatenate(parts, axis=-1)
    assert w.shape[-1] == N_COLS
    return w.astype(BF16)


def kernel(x, c, norm_g, w_ada, b_ada, w_in, w_out, q_norm_g, k_norm_g, rel_bias, conv_w, conv_b,
           conv_ln_g, conv_ln_b, shift_mu, decay_w0, decay_up, iclr_a0, iclr_up, key_k, key_a,
           bonus_r_k, lnx_g, lnx_b):
    bsz, seq, d = x.shape
    depth = w_in.shape[0]
    assert seq % (2 * TILE) == 0 and d == w_out.shape[-1]
    topk = min(TOPK_MAX, seq // 4)
    tm = 512 if seq % 512 == 0 else 256
    conv_tt = 256

    mod = _modulation(c, w_ada, b_ada)
    shift = mod[:, :, None, 0:d]
    sc1p = 1.0 + mod[:, :, None, d:2 * d]
    gate_mod = mod[:, :, None, 2 * d:3 * d]

    w_all = _reorder_w_in(w_in)
    head = jnp.arange(D_A) // HEAD_DIM
    avg = (head[:, None] == head[None, :]).astype(BF16) * (1.0 / HEAD_DIM)
    qg = jnp.tile(q_norm_g, (1, H_A))[:, None, :] * (HEAD_DIM ** -0.5)
    kg = jnp.tile(k_norm_g, (1, H_A))[:, None, :]
    bias_tiles = _bias_tiles(rel_bias)
    c_shift = 3 * D_C + 2 * D_LORA
    mu_pad = jnp.pad(shift_mu, ((0, 0), (0, RW_COLS - c_shift)))[:, None, :]
    zl = jnp.zeros((depth, D_LORA, D_C), F32)
    lora = jnp.concatenate([jnp.concatenate([decay_up, zl], axis=2),
                            jnp.concatenate([zl, iclr_up], axis=2)], axis=1)
    row = lambda t: t[:, None, :]
    layers = dict(
        shift=shift, sc1p=sc1p, gate_mod=gate_mod, norm_g=row(norm_g), w=w_all, qg=qg, kg=kg,
        w_out=w_out.astype(BF16), conv_w=conv_w, conv_b=row(conv_b), conv_g=row(conv_ln_g),
        conv_beta=row(conv_ln_b), mu=mu_pad, lora=lora, w0=row(decay_w0), a0=row(iclr_a0),
        kk=row(key_k), ka=row(key_a), rk=row(bonus_r_k.reshape(depth, D_C)), lg=row(lnx_g),
        lb=row(lnx_b))

    def layer(x2, p):
        q, k, v, qi, ki, wi, gates, conv_in, rw_in = _inproj(
            x2, p["sc1p"], p["shift"], p["norm_g"], p["w"], avg, p["qg"], p["kg"], seq, tm)
        r3 = lambda t: t.reshape(bsz, seq, t.shape[-1])
        attn = _attention(r3(q), r3(k), r3(v), r3(qi), r3(ki), r3(wi), bias_tiles, topk)
        conv = _conv(r3(conv_in), p["conv_w"], p["conv_b"], p["conv_g"], p["conv_beta"], conv_tt)
        rwo = _rwkv(r3(rw_in), p["mu"], p["lora"], p["w0"], p["a0"], p["kk"], p["ka"], p["rk"],
                    p["lg"], p["lb"])
        x2 = _outproj(x2, attn.reshape(-1, D_A), conv.reshape(-1, D_B), rwo.reshape(-1, D_C),
                      gates, p["gate_mod"], p["w_out"], seq, tm)
        return x2, None

    x2, _ = lax.scan(layer, x.reshape(bsz * seq, d), layers)
    return x2.reshape(bsz, seq, d)
```

```python
import functools
import math

import jax
import jax.numpy as jnp
import numpy as np
from jax import lax
from jax.experimental import pallas as pl
from jax.experimental.pallas import tpu as pltpu

F32 = jnp.float32
BF16 = jnp.bfloat16

HEAD_DIM = 64
CHUNK = 64
EPS = 1e-6
H_A = 6
D_A = H_A * HEAD_DIM
N_IDX_HEADS = 8
D_IDX = 64
TOPK_MAX = 256
N_BUCKETS = 32
MAX_DISTANCE = 1024
D_B = 256
CONV_WIDTH = 31
H_C = 6
D_C = H_C * HEAD_DIM
D_LORA = 32
LNX_EPS = 64e-5
D_MIX = D_A + D_B + D_C

LANES = 128
TILE = 128
GROUP = 4
N_FAR = 9
VMEM_LIMIT = 48 * 1024 * 1024

C_Q, C_K, C_V = 0, 384, 768
C_QI = 1152
C_KI = 1664
C_WI = 1792
C_GATES = 1920
C_CONV = 2944
C_RW = 3456
N_COLS = 4736
RW_COLS = 1280

INT_MIN = -(2 ** 31)
INT_MAX = 2 ** 31 - 1
NEG_KEY = int(np.array(-np.inf, np.float32).view(np.int32)) ^ 0x7FFFFFFF
NEG_BIG = -1e30

NT_DIMS = (((1,), (1,)), ((), ()))
TN_DIMS = (((0,), (0,)), ((), ()))


def _bdot(a, b):
    return jnp.dot(a.astype(BF16), b.astype(BF16), preferred_element_type=F32)


def _bdot_nt(a, b):
    return lax.dot_general(a.astype(BF16), b.astype(BF16), NT_DIMS, preferred_element_type=F32)


def _sigmoid(x):
    return 1.0 / (1.0 + jnp.exp(-x))


def _silu(x):
    return x * _sigmoid(x)


def _mod_kernel(c_ref, w_ref, b_ref, o_ref):
    ca = _silu(c_ref[...])
    o_ref[0] = jnp.dot(ca, w_ref[0], precision=lax.Precision.HIGHEST,
                       preferred_element_type=F32) + b_ref[0]


def _modulation(c, w_ada, b_ada):
    depth, d, d3 = w_ada.shape
    b = c.shape[0]
    return pl.pallas_call(
        _mod_kernel,
        grid=(depth, d3 // d),
        in_specs=[pl.BlockSpec((b, d), lambda l, j: (0, 0)),
                  pl.BlockSpec((1, d, d), lambda l, j: (l, 0, j)),
                  pl.BlockSpec((1, 1, d), lambda l, j: (l, 0, j))],
        out_specs=pl.BlockSpec((1, b, d), lambda l, j: (l, 0, j)),
        out_shape=jax.ShapeDtypeStruct((depth, b, d3), F32),
        compiler_params=pltpu.CompilerParams(
            dimension_semantics=("arbitrary", "arbitrary"), vmem_limit_bytes=VMEM_LIMIT),
    )(c, w_ada, b_ada.reshape(depth, 1, d3))


def _inproj_kernel(x_ref, sc_ref, sh_ref, g_ref, w_ref, avg_ref, qg_ref, kg_ref,
                   q_ref, k_ref, vt_ref, qi_ref, ki_ref, wit_ref, gates_ref, conv_ref, rw_ref):
    x = x_ref[...]
    ms = jnp.mean(x * x, axis=-1, keepdims=True)
    h = x * lax.rsqrt(ms + EPS) * g_ref[...]
    hb = (h * sc_ref[0] + sh_ref[0]).astype(BF16)

    def mm(c0, c1):
        return jnp.dot(hb, w_ref[:, c0:c1], preferred_element_type=F32)

    def head_rms(t, g):
        m2 = jnp.dot((t * t).astype(BF16), avg_ref[...], preferred_element_type=F32)
        return t * lax.rsqrt(m2 + EPS) * g

    q_ref[...] = head_rms(mm(C_Q, C_Q + D_A), qg_ref[...]).astype(BF16)
    k_ref[...] = head_rms(mm(C_K, C_K + D_A), kg_ref[...]).astype(BF16)
    vt_ref[0] = mm(C_V, C_V + D_A).T.astype(BF16)
    qi_ref[...] = mm(C_QI, C_QI + 512).astype(BF16)
    ki_ref[...] = mm(C_KI, C_KI + LANES).astype(BF16)
    wit_ref[...] = mm(C_WI, C_WI + LANES).T[:N_IDX_HEADS, :]
    gates_ref[:, 0:512] = mm(C_GATES, C_GATES + 512)
    gates_ref[:, 512:1024] = mm(C_GATES + 512, C_GATES + 1024)
    conv_ref[...] = mm(C_CONV, C_CONV + 512)
    rw_ref[:, 0:512] = mm(C_RW, C_RW + 512)
    rw_ref[:, 512:1024] = mm(C_RW + 512, C_RW + 1024)
    rw_ref[:, 1024:RW_COLS] = mm(C_RW + 1024, C_RW + RW_COLS)


def _inproj(x2, sc1p, shift, norm_g, w, avg, qg, kg, seq, tm):
    n, d = x2.shape
    per_b = seq // tm
    row = lambda i: (i, 0)
    full = lambda i: (0, 0)
    bat = lambda i: (i // per_b, 0, 0)
    rows = lambda wd, dt: (pl.BlockSpec((tm, wd), row), jax.ShapeDtypeStruct((n, wd), dt))
    outs = [rows(D_A, BF16), rows(D_A, BF16),
            (pl.BlockSpec((1, D_A, tm), lambda i: (i, 0, 0)),
             jax.ShapeDtypeStruct((n // tm, D_A, tm), BF16)),
            rows(512, BF16), rows(LANES, BF16),
            (pl.BlockSpec((N_IDX_HEADS, tm), lambda i: (0, i)),
             jax.ShapeDtypeStruct((N_IDX_HEADS, n), F32)),
            rows(D_MIX, F32), rows(512, F32), rows(RW_COLS, F32)]
    return pl.pallas_call(
        _inproj_kernel,
        grid=(n // tm,),
        in_specs=[pl.BlockSpec((tm, d), row),
                  pl.BlockSpec((1, 1, d), bat),
                  pl.BlockSpec((1, 1, d), bat),
                  pl.BlockSpec((1, d), full),
                  pl.BlockSpec((d, N_COLS), full),
                  pl.BlockSpec((D_A, D_A), full),
                  pl.BlockSpec((1, D_A), full),
                  pl.BlockSpec((1, D_A), full)],
        out_specs=[o[0] for o in outs],
        out_shape=[o[1] for o in outs],
        compiler_params=pltpu.CompilerParams(
            dimension_semantics=("arbitrary",), vmem_limit_bytes=VMEM_LIMIT),
    )(x2, sc1p, shift, norm_g, w, avg, qg, kg)


def _sortable(s):
    b = lax.bitcast_convert_type(s, jnp.int32)
    return b ^ ((b >> 31) & jnp.int32(0x7FFFFFFF))


def _fold(t, op):
    out = t[0:8, :]
    for i in range(1, TILE // 8):
        out = op(out, t[8 * i:8 * i + 8, :])
    return out


def _attn_kernel(q_ref, k_ref, vt_ref, qi_ref, ki_ref, wit_ref, bias_ref, o_ref,
                 key_scr, qih_scr, qh_scr, mb_scr, s_scr, m_scr, l_scr, acc_scr,
                 *, topk, pos_bits):
    qb = pl.program_id(1)
    nst = qb // GROUP + 1
    kf = float(topk)
    lane = lax.broadcasted_iota(jnp.int32, (TILE, LANES), 1)
    krow = lax.broadcasted_iota(jnp.int32, (TILE, LANES), 0)
    lo_half = lane < HEAD_DIM

    wit = wit_ref[...]
    qi = qi_ref[0]
    for h in range(N_IDX_HEADS):
        pair = qi[:, (h // 2) * LANES:(h // 2 + 1) * LANES]
        keep = lo_half if h % 2 == 0 else jnp.logical_not(lo_half)
        qih_scr[h] = jnp.where(keep, pair, jnp.zeros_like(pair))
    q = q_ref[0]
    for h in range(H_A):
        pair = q[:, (h // 2) * LANES:(h // 2 + 1) * LANES]
        keep = lo_half if h % 2 == 0 else jnp.logical_not(lo_half)
        qh_scr[h] = jnp.where(keep, pair, jnp.zeros_like(pair))

    adm = (krow // CHUNK) <= (lane // CHUNK)

    def p1(st, c):
        base = st * GROUP
        kis = ki_ref[0, pl.ds(pl.multiple_of(base * TILE, GROUP * TILE), GROUP * TILE), :]
        acc = [None] * GROUP
        for h in range(N_IDX_HEADS):
            d = lax.dot_general(kis, qih_scr[h], NT_DIMS, preferred_element_type=F32)
            w_row = wit[h:h + 1, :]
            for u in range(GROUP):
                t = jnp.maximum(d[u * TILE:(u + 1) * TILE, :], 0.0) * w_row
                acc[u] = t if acc[u] is None else acc[u] + t
        for u in range(GROUP):
            kt = base + u
            diag = jnp.where(adm, _sortable(acc[u]), NEG_KEY)
            key_scr[kt] = jnp.where(kt < qb, _sortable(acc[u]), jnp.where(kt == qb, diag, NEG_KEY))
        return c

    lax.fori_loop(0, nst, p1, 0)

    def count_ge(thr_row):
        def body(st, c):
            for u in range(GROUP):
                c = c + _fold(jnp.where(key_scr[st * GROUP + u] >= thr_row, 1.0, 0.0), jnp.add)
            return c
        c = lax.fori_loop(0, nst, body, jnp.zeros((8, LANES), F32))
        return jnp.sum(c, axis=0, keepdims=True)

    def bisect(_, lohi):
        lo, hi = lohi
        mid = (lo >> 1) + (hi >> 1) + (lo & hi & 1)
        ge = count_ge(mid) >= kf
        return jnp.where(ge, mid, lo), jnp.where(ge, hi, mid)

    lo0 = jnp.full((1, LANES), INT_MIN, jnp.int32)
    hi0 = jnp.full((1, LANES), INT_MAX, jnp.int32)
    thr, _ = lax.fori_loop(0, 32, bisect, (lo0, hi0))
    cnt_ge = count_ge(thr)
    tie = jnp.logical_and(cnt_ge > kf, thr > NEG_KEY)

    @pl.when(jnp.max(jnp.where(tie, 1.0, 0.0)) > 0.0)
    def _():
        need = kf - count_ge(thr + 1)

        def count_eq_le(pmax):
            def body(kt, c):
                hit = jnp.where(key_scr[kt] == thr, krow + kt * TILE, INT_MAX) <= pmax
                return c + _fold(jnp.where(hit, 1.0, 0.0), jnp.add)
            c = lax.fori_loop(0, nst * GROUP, body, jnp.zeros((8, LANES), F32))
            return jnp.sum(c, axis=0, keepdims=True)

        def pbisect(_, lohi):
            plo, phi = lohi
            pmid = (plo + phi) >> 1
            ok = count_eq_le(pmid) >= need
            return jnp.where(ok, plo, pmid), jnp.where(ok, pmid, phi)

        plo0 = jnp.full((1, LANES), -1, jnp.int32)
        phi0 = jnp.full((1, LANES), (1 << pos_bits) - 1, jnp.int32)
        _, pthr = lax.fori_loop(0, pos_bits + 1, pbisect, (plo0, phi0))
        pthr = jnp.where(tie, pthr, INT_MAX)

        def drop(kt, c):
            kk = key_scr[kt]
            pos = jnp.where(kk == thr, krow + kt * TILE, -1)
            key_scr[kt] = jnp.where(pos > pthr, NEG_KEY, kk)
            return c

        lax.fori_loop(0, nst * GROUP, drop, 0)

    thr_eff = jnp.maximum(thr, NEG_KEY + 1)

    for h in range(H_A):
        m_scr[h] = jnp.full((8, LANES), NEG_BIG, F32)
        l_scr[h] = jnp.zeros((8, LANES), F32)
        acc_scr[h] = jnp.zeros((HEAD_DIM, LANES), F32)

    def p3a(st, c):
        base = st * GROUP
        k_s = k_ref[0, pl.ds(pl.multiple_of(base * TILE, GROUP * TILE), GROUP * TILE), :]
        for u in range(GROUP):
            mb_scr[u] = jnp.where(key_scr[base + u] >= thr_eff, 0.0, NEG_BIG)
        for h in range(H_A):
            g = h // 2
            s = lax.dot_general(k_s[:, g * LANES:(g + 1) * LANES], qh_scr[h], NT_DIMS,
                                preferred_element_type=F32)
            mx = m_scr[h]
            for u in range(GROUP):
                dd = jnp.clip(base + u - qb, -N_FAR, 0) + N_FAR
                su = s[u * TILE:(u + 1) * TILE, :] + bias_ref[h * (N_FAR + 1) + dd] + mb_scr[u]
                s_scr[h, base + u] = su
                mx = jnp.maximum(mx, _fold(su, jnp.maximum))
            m_scr[h] = mx
        return c

    lax.fori_loop(0, nst, p3a, 0)
    m_row = [jnp.max(m_scr[h], axis=0, keepdims=True) for h in range(H_A)]

    def p3b(st, c):
        base = st * GROUP
        for h in range(H_A):
            lsum = l_scr[h]
            ps = []
            for u in range(GROUP):
                p = jnp.exp(s_scr[h, base + u] - m_row[h])
                lsum = lsum + _fold(p, jnp.add)
                ps.append(p.astype(BF16))
            l_scr[h] = lsum
            acc_scr[h] += jnp.dot(vt_ref[st, h * HEAD_DIM:(h + 1) * HEAD_DIM, :],
                                  jnp.concatenate(ps, axis=0), preferred_element_type=F32)
        return c

    lax.fori_loop(0, nst, p3b, 0)

    out_t = jnp.concatenate(
        [acc_scr[h] / jnp.sum(l_scr[h], axis=0, keepdims=True) for h in range(H_A)], axis=0)
    o_ref[0] = out_t.T


def _attention(q, k, vt, qi, ki, wit, bias_tiles, topk):
    b, s, _ = q.shape
    nt = s // TILE
    ng = s // (GROUP * TILE)
    kern = functools.partial(_attn_kernel, topk=topk, pos_bits=max(1, int(math.ceil(math.log2(s)))))
    qtile = lambda bb, i: (bb, i, 0)
    whole = lambda bb, i: (bb, 0, 0)
    once = pl.Buffered(1)
    return pl.pallas_call(
        kern,
        grid=(b, nt),
        in_specs=[pl.BlockSpec((1, TILE, D_A), qtile),
                  pl.BlockSpec((1, s, D_A), whole, pipeline_mode=once),
                  pl.BlockSpec((ng, D_A, GROUP * TILE), whole, pipeline_mode=once),
                  pl.BlockSpec((1, TILE, 512), qtile),
                  pl.BlockSpec((1, s, LANES), whole, pipeline_mode=once),
                  pl.BlockSpec((N_IDX_HEADS, TILE), lambda bb, i: (0, bb * nt + i)),
                  pl.BlockSpec(bias_tiles.shape, lambda bb, i: (0, 0, 0), pipeline_mode=once)],
        out_specs=pl.BlockSpec((1, TILE, D_A), qtile),
        out_shape=jax.ShapeDtypeStruct((b, s, D_A), F32),
        scratch_shapes=[pltpu.VMEM((nt, TILE, LANES), jnp.int32),
                        pltpu.VMEM((N_IDX_HEADS, TILE, LANES), BF16),
                        pltpu.VMEM((H_A, TILE, LANES), BF16),
                        pltpu.VMEM((GROUP, TILE, LANES), F32),
                        pltpu.VMEM((H_A, nt, TILE, LANES), F32),
                        pltpu.VMEM((H_A, 8, LANES), F32),
                        pltpu.VMEM((H_A, 8, LANES), F32),
                        pltpu.VMEM((H_A, HEAD_DIM, LANES), F32)],
        compiler_params=pltpu.CompilerParams(
            dimension_semantics=("arbitrary", "arbitrary"), vmem_limit_bytes=VMEM_LIMIT),
    )(q, k, vt, qi, ki, wit, bias_tiles)


def _t5_bucket(rel):
    nb = N_BUCKETS // 2
    max_exact = nb // 2
    ret = jnp.where(rel > 0, nb, 0)
    n = jnp.abs(rel)
    nf = jnp.maximum(n, 1).astype(F32)
    large = max_exact + (jnp.log(nf / max_exact) / math.log(MAX_DISTANCE / max_exact)
                         * (nb - max_exact)).astype(jnp.int32)
    large = jnp.minimum(large, nb - 1)
    return ret + jnp.where(n < max_exact, n, large)


def _bias_tiles(rel_bias):
    j = jnp.arange(TILE, dtype=jnp.int32)[:, None]
    t = jnp.arange(TILE, dtype=jnp.int32)[None, :]
    tiles = [rel_bias[_t5_bucket((dd - N_FAR) * TILE + j - t)] for dd in range(N_FAR + 1)]
    arr = jnp.stack(tiles).astype(F32)
    return arr.transpose(3, 0, 1, 2).reshape(H_A * (N_FAR + 1), TILE, TILE)


CONV_HALO = 32


def _conv_kernel(cur_ref, halo_ref, w_ref, b_ref, g_ref, beta_ref, o_ref, u_scr, *, tt):
    i = pl.program_id(1)

    def glu(t):
        return t[:, :D_B] * _sigmoid(t[:, D_B:])

    u_scr[0:CONV_HALO, :] = jnp.where(i > 0, glu(halo_ref[0]), 0.0)
    u_scr[CONV_HALO:CONV_HALO + tt, :] = glu(cur_ref[0])
    rows = 64
    first = CONV_HALO - (CONV_WIDTH - 1)
    for r0 in range(0, tt, rows):
        acc = jnp.broadcast_to(b_ref[...], (rows, D_B))
        for j in range(CONV_WIDTH):
            acc = acc + w_ref[j:j + 1, :] * u_scr[r0 + first + j:r0 + first + j + rows, :]
        mu = jnp.mean(acc, axis=-1, keepdims=True)
        cen = acc - mu
        var = jnp.mean(cen * cen, axis=-1, keepdims=True)
        y = cen * lax.rsqrt(var + EPS) * g_ref[...] + beta_ref[...]
        o_ref[0, r0:r0 + rows, :] = _silu(y)


def _conv(conv_in, w, b, g, beta, tt):
    bsz, s, _ = conv_in.shape
    per = tt // CONV_HALO
    vec = lambda bb, i: (0, 0)
    return pl.pallas_call(
        functools.partial(_conv_kernel, tt=tt),
        grid=(bsz, s // tt),
        in_specs=[pl.BlockSpec((1, tt, 2 * D_B), lambda bb, i: (bb, i, 0)),
                  pl.BlockSpec((1, CONV_HALO, 2 * D_B),
                               lambda bb, i: (bb, jnp.maximum(i * per - 1, 0), 0)),
                  pl.BlockSpec((CONV_WIDTH, D_B), vec),
                  pl.BlockSpec((1, D_B), vec),
                  pl.BlockSpec((1, D_B), vec),
                  pl.BlockSpec((1, D_B), vec)],
        out_specs=pl.BlockSpec((1, tt, D_B), lambda bb, i: (bb, i, 0)),
        out_shape=jax.ShapeDtypeStruct((bsz, s, D_B), F32),
        scratch_shapes=[pltpu.VMEM((CONV_HALO + tt, D_B), F32)],
        compiler_params=pltpu.CompilerParams(
            dimension_semantics=("arbitrary", "arbitrary"), vmem_limit_bytes=VMEM_LIMIT),
    )(conv_in, conv_in, w, b, g, beta)


def _rwkv_kernel(rw_ref, mu_ref, lora_ref, w0_ref, a0_ref, kk_ref, ka_ref, rk_ref, lg_ref, lb_ref,
                 o_ref, s_scr, prev_scr):
    c = pl.program_id(1)
    n = CHUNK

    @pl.when(c == 0)
    def _():
        s_scr[...] = jnp.zeros_like(s_scr)
        prev_scr[...] = jnp.zeros_like(prev_scr)

    ps = rw_ref[0]
    row = lax.broadcasted_iota(jnp.int32, (n, 1), 0)
    prev = jnp.where(row == 0, prev_scr[0:1, :], pltpu.roll(ps, 1, axis=0))
    prev_scr[0:1, :] = ps[n - 1:n, :]
    xs = ps + mu_ref[...] * (prev - ps)
    r = xs[:, 0:D_C]
    k = xs[:, D_C:2 * D_C]
    v = xs[:, 2 * D_C:3 * D_C]
    dn = xs[:, 3 * D_C:3 * D_C + 2 * D_LORA]
    lane64 = lax.broadcasted_iota(jnp.int32, (n, 2 * D_LORA), 1)
    dn = jnp.where(lane64 < D_LORA, jnp.tanh(dn), dn)
    pre = jnp.dot(dn, lora_ref[...], precision=lax.Precision.HIGHEST, preferred_element_type=F32)
    z = -(w0_ref[...] + pre[:, :D_C])
    softplus = jnp.maximum(z, 0.0) + jnp.log(1.0 + jnp.exp(-jnp.abs(z)))
    logdec = -jnp.exp(-softplus - 0.5)
    a = _sigmoid(a0_ref[...] + pre[:, D_C:])
    kkraw = k * kk_ref[...]
    k2 = k * (1.0 + (a - 1.0) * ka_ref[...])
    bonus_pre = r * k2 * rk_ref[...]

    ri = lax.broadcasted_iota(jnp.int32, (n, n), 0)
    ci = lax.broadcasted_iota(jnp.int32, (n, n), 1)
    incl = ci <= ri
    strict = ci < ri
    tri = jnp.where(incl, 1.0, 0.0)

    cum_all = jnp.dot(tri, logdec, precision=lax.Precision.HIGHEST, preferred_element_type=F32)
    p_in_all = jnp.exp(cum_all)
    p_inv_all = jnp.exp(-cum_all)
    p_ex_all = jnp.exp(cum_all - logdec)

    heads = range(H_C)
    hs = lambda t, h: t[:, h * HEAD_DIM:(h + 1) * HEAD_DIM]
    v_h = [hs(v, h) for h in heads]
    p_in = [hs(p_in_all, h) for h in heads]
    kkn = []
    for h in heads:
        kr = hs(kkraw, h)
        kkn.append(kr * lax.rsqrt(jnp.maximum(jnp.sum(kr * kr, axis=-1, keepdims=True), 1e-12)))
    rt = [hs(r, h) * p_in[h] for h in heads]
    at = [kkn[h] * hs(p_ex_all, h) for h in heads]
    kt = [hs(k2, h) * hs(p_inv_all, h) for h in heads]
    bt = [kkn[h] * hs(a, h) * hs(p_inv_all, h) for h in heads]
    lhs = [jnp.concatenate([at[h], rt[h]], axis=0) for h in heads]
    ak = [_bdot_nt(lhs[h], kt[h]) for h in heads]
    ab = [_bdot_nt(lhs[h], bt[h]) for h in heads]
    a_ak = [jnp.where(strict, ak[h][:n], 0.0) for h in heads]
    a_rk = [jnp.where(incl, ak[h][n:], 0.0) for h in heads]
    a_rb = [jnp.where(incl, ab[h][n:], 0.0) for h in heads]
    npow = [jnp.where(strict, -ab[h][:n], 0.0) for h in heads]
    tm = list(npow)
    rhs = [_bdot(a_ak[h], v_h[h]) for h in heads]
    for _ in range(5):
        npow = [_bdot(npow[h], npow[h]) for h in heads]
        tm = [tm[h] + npow[h] + _bdot(tm[h], npow[h]) for h in heads]
    wt = [at[h] + _bdot(tm[h], at[h]) for h in heads]
    uloc = [rhs[h] + _bdot(tm[h], rhs[h]) for h in heads]
    o_loc = [_bdot(a_rk[h], v_h[h]) for h in heads]
    s0 = [s_scr[h] for h in heads]
    u = [_bdot_nt(wt[h], s0[h]) + uloc[h] for h in heads]
    o = [_bdot_nt(rt[h], s0[h]) + o_loc[h] - _bdot(a_rb[h], u[h]) for h in heads]
    for h in heads:
        mcat = jnp.concatenate([kt[h], bt[h]], axis=0).astype(BF16)
        ncat = jnp.concatenate([v_h[h], -u[h]], axis=0).astype(BF16)
        upd = lax.dot_general(ncat, mcat, TN_DIMS, preferred_element_type=F32)
        s_scr[h] = (s0[h] + upd) * p_in[h][n - 1:n, :]
    for h in heads:
        sl = slice(h * HEAD_DIM, (h + 1) * HEAD_DIM)
        mu_y = jnp.mean(o[h], axis=-1, keepdims=True)
        cen = o[h] - mu_y
        var = jnp.mean(cen * cen, axis=-1, keepdims=True)
        y = cen * lax.rsqrt(var + LNX_EPS) * lg_ref[:, sl] + lb_ref[:, sl]
        y = y + jnp.sum(bonus_pre[:, sl], axis=-1, keepdims=True) * v_h[h]
        o_ref[0, :, sl] = y


def _rwkv(rw, mu, lora, w0, a0, kk, ka, rk, lg, lb):
    bsz, s, _ = rw.shape
    vec = lambda bb, i: (0, 0)
    return pl.pallas_call(
        _rwkv_kernel,
        grid=(bsz, s // CHUNK),
        in_specs=[pl.BlockSpec((1, CHUNK, RW_COLS), lambda bb, i: (bb, i, 0)),
                  pl.BlockSpec((1, RW_COLS), vec),
                  pl.BlockSpec((2 * D_LORA, 2 * D_C), vec)] +
                 [pl.BlockSpec((1, D_C), vec)] * 7,
        out_specs=pl.BlockSpec((1, CHUNK, D_C), lambda bb, i: (bb, i, 0)),
        out_shape=jax.ShapeDtypeStruct((bsz, s, D_C), F32),
        scratch_shapes=[pltpu.VMEM((H_C, HEAD_DIM, HEAD_DIM), F32),
                        pltpu.VMEM((8, RW_COLS), F32)],
        compiler_params=pltpu.CompilerParams(
            dimension_semantics=("arbitrary", "arbitrary"), vmem_limit_bytes=VMEM_LIMIT),
    )(rw, mu, lora, w0, a0, kk, ka, rk, lg, lb)


def _outproj_kernel(x_ref, attn_ref, conv_ref, rw_ref, gates_ref, gm_ref, w_ref, o_ref):
    sg = _silu(gates_ref[...])
    ya = (attn_ref[...] * sg[:, 0:D_A]).astype(BF16)
    yb = (conv_ref[...] * sg[:, D_A:D_A + D_B]).astype(BF16)
    yc = (rw_ref[...] * sg[:, D_A + D_B:D_MIX]).astype(BF16)
    y = (jnp.dot(ya, w_ref[0:D_A, :], preferred_element_type=F32)
         + jnp.dot(yb, w_ref[D_A:D_A + D_B, :], preferred_element_type=F32)
         + jnp.dot(yc, w_ref[D_A + D_B:D_MIX, :], preferred_element_type=F32))
    o_ref[...] = x_ref[...] + gm_ref[0] * y


def _outproj(x2, attn, conv, rw, gates, gate_mod, w, seq, tm):
    n, d = x2.shape
    per_b = seq // tm
    row = lambda i: (i, 0)
    return pl.pallas_call(
        _outproj_kernel,
        grid=(n // tm,),
        in_specs=[pl.BlockSpec((tm, d), row),
                  pl.BlockSpec((tm, D_A), row),
                  pl.BlockSpec((tm, D_B), row),
                  pl.BlockSpec((tm, D_C), row),
                  pl.BlockSpec((tm, D_MIX), row),
                  pl.BlockSpec((1, 1, d), lambda i: (i // per_b, 0, 0)),
                  pl.BlockSpec((D_MIX, d), lambda i: (0, 0))],
        out_specs=pl.BlockSpec((tm, d), row),
        out_shape=jax.ShapeDtypeStruct((n, d), F32),
        compiler_params=pltpu.CompilerParams(
            dimension_semantics=("arbitrary",), vmem_limit_bytes=VMEM_LIMIT),
    )(x2, attn, conv, rw, gates, gate_mod, w)


def _reorder_w_in(w_in):
    depth, d, _ = w_in.shape
    a_cols = 4 * D_A + N_IDX_HEADS * D_IDX + D_IDX + N_IDX_HEADS
    b0 = a_cols
    c0 = a_cols + 3 * D_B
    c_shift = 3 * D_C + 2 * D_LORA
    zeros = lambda m: jnp.zeros((depth, d, m), w_in.dtype)
    qi0 = 4 * D_A
    ki0 = qi0 + N_IDX_HEADS * D_IDX
    wi0 = ki0 + D_IDX
    parts = [w_in[..., 0:3 * D_A],
             w_in[..., qi0:ki0],
             w_in[..., ki0:wi0], w_in[..., ki0:wi0],
             w_in[..., wi0:a_cols], zeros(LANES - N_IDX_HEADS),
             w_in[..., 3 * D_A:4 * D_A], w_in[..., b0 + 2 * D_B:b0 + 3 * D_B],
             w_in[..., c0 + c_shift:c0 + c_shift + D_C],
             w_in[..., b0:b0 + 2 * D_B],
             w_in[..., c0:c0 + c_shift], zeros(RW_COLS - c_shift)]
    w = jnp.concatenate(parts, axis=-1)
    assert w.shape[-1] == N_COLS
    return w.astype(BF16)


def kernel(x, c, norm_g, w_ada, b_ada, w_in, w_out, q_norm_g, k_norm_g, rel_bias, conv_w, conv_b,
           conv_ln_g, conv_ln_b, shift_mu, decay_w0, decay_up, iclr_a0, iclr_up, key_k, key_a,
           bonus_r_k, lnx_g, lnx_b):
    bsz, seq, d = x.shape
    depth = w_in.shape[0]
    assert seq % (GROUP * TILE) == 0 and d == w_out.shape[-1]
    topk = min(TOPK_MAX, seq // 4)
    tm = GROUP * TILE
    conv_tt = 256

    mod = _modulation(c, w_ada, b_ada)
    shift = mod[:, :, None, 0:d]
    sc1p = 1.0 + mod[:, :, None, d:2 * d]
    gate_mod = mod[:, :, None, 2 * d:3 * d]

    w_all = _reorder_w_in(w_in)
    head = jnp.arange(D_A) // HEAD_DIM
    avg = (head[:, None] == head[None, :]).astype(BF16) * (1.0 / HEAD_DIM)
    qg = jnp.tile(q_norm_g, (1, H_A))[:, None, :] * (HEAD_DIM ** -0.5)
    kg = jnp.tile(k_norm_g, (1, H_A))[:, None, :]
    bias_tiles = _bias_tiles(rel_bias)
    c_shift = 3 * D_C + 2 * D_LORA
    mu_pad = jnp.pad(shift_mu, ((0, 0), (0, RW_COLS - c_shift)))[:, None, :]
    zl = jnp.zeros((depth, D_LORA, D_C), F32)
    lora = jnp.concatenate([jnp.concatenate([decay_up, zl], axis=2),
                            jnp.concatenate([zl, iclr_up], axis=2)], axis=1)
    row = lambda t: t[:, None, :]
    layers = dict(
        shift=shift, sc1p=sc1p, gate_mod=gate_mod, norm_g=row(norm_g), w=w_all, qg=qg, kg=kg,
        w_out=w_out.astype(BF16), conv_w=conv_w, conv_b=row(conv_b), conv_g=row(conv_ln_g),
        conv_beta=row(conv_ln_b), mu=mu_pad, lora=lora, w0=row(decay_w0), a0=row(iclr_a0),
        kk=row(key_k), ka=row(key_a), rk=row(bonus_r_k.reshape(depth, D_C)), lg=row(lnx_g),
        lb=row(lnx_b))

    def layer(x2, p):
        q, k, vt, qi, ki, wit, gates, conv_in, rw_in = _inproj(
            x2, p["sc1p"], p["shift"], p["norm_g"], p["w"], avg, p["qg"], p["kg"], seq, tm)
        r3 = lambda t: t.reshape(bsz, seq, t.shape[-1])
        attn = _attention(r3(q), r3(k), vt, r3(qi), r3(ki), wit, bias_tiles, topk)
        conv = _conv(r3(conv_in), p["conv_w"], p["conv_b"], p["conv_g"], p["conv_beta"], conv_tt)
        rwo = _rwkv(r3(rw_in), p["mu"], p["lora"], p["w0"], p["a0"], p["kk"], p["ka"], p["rk"],
                    p["lg"], p["lb"])
        x2 = _outproj(x2, attn.reshape(-1, D_A), conv.reshape(-1, D_B), rwo.reshape(-1, D_C),
                      gates, p["gate_mod"], p["w_out"], seq, tm)
        return x2, None

    x2, _ = lax.scan(layer, x.reshape(bsz * seq, d), layers)
    return x2.reshape(bsz, seq, d)
```

```python
import functools
import math

import jax
import jax.numpy as jnp
import numpy as np
from jax import lax
from jax.experimental import pallas as pl
from jax.experimental.pallas import tpu as pltpu

F32 = jnp.float32
BF16 = jnp.bfloat16

HEAD_DIM = 64
CHUNK = 64
EPS = 1e-6
H_A = 6
D_A = H_A * HEAD_DIM
N_IDX_HEADS = 8
D_IDX = 64
TOPK_MAX = 256
N_BUCKETS = 32
MAX_DISTANCE = 1024
D_B = 256
CONV_WIDTH = 31
H_C = 6
D_C = H_C * HEAD_DIM
D_LORA = 32
LNX_EPS = 64e-5
D_MIX = D_A + D_B + D_C

LANES = 128
TILE = 128
GROUP = 4
ONES_ROWS = 16
RW_CHUNKS = 4
N_FAR = 9
VMEM_LIMIT = 48 * 1024 * 1024

C_Q, C_K, C_V = 0, 384, 768
C_QI = 1152
C_KI = 1664
C_WI = 1792
C_GATES = 1920
C_CONV = 2944
C_RW = 3456
N_COLS = 4736
RW_COLS = 1280

INT_MIN = -(2 ** 31)
INT_MAX = 2 ** 31 - 1
NEG_KEY = int(np.array(-np.inf, np.float32).view(np.int32)) ^ 0x7FFFFFFF
NEG_BIG = -1e30

NT_DIMS = (((1,), (1,)), ((), ()))
TN_DIMS = (((0,), (0,)), ((), ()))


def _bdot(a, b):
    return jnp.dot(a.astype(BF16), b.astype(BF16), preferred_element_type=F32)


def _bdot_nt(a, b):
    return lax.dot_general(a.astype(BF16), b.astype(BF16), NT_DIMS, preferred_element_type=F32)


def _sigmoid(x):
    return 1.0 / (1.0 + jnp.exp(-x))


def _silu(x):
    return x * _sigmoid(x)


def _mod_kernel(c_ref, w_ref, b_ref, o_ref):
    ca = _silu(c_ref[...])
    o_ref[0] = jnp.dot(ca, w_ref[0], precision=lax.Precision.HIGHEST,
                       preferred_element_type=F32) + b_ref[0]


def _modulation(c, w_ada, b_ada):
    depth, d, d3 = w_ada.shape
    b = c.shape[0]
    return pl.pallas_call(
        _mod_kernel,
        grid=(depth, d3 // d),
        in_specs=[pl.BlockSpec((b, d), lambda l, j: (0, 0)),
                  pl.BlockSpec((1, d, d), lambda l, j: (l, 0, j)),
                  pl.BlockSpec((1, 1, d), lambda l, j: (l, 0, j))],
        out_specs=pl.BlockSpec((1, b, d), lambda l, j: (l, 0, j)),
        out_shape=jax.ShapeDtypeStruct((depth, b, d3), F32),
        compiler_params=pltpu.CompilerParams(
            dimension_semantics=("arbitrary", "arbitrary"), vmem_limit_bytes=VMEM_LIMIT),
    )(c, w_ada, b_ada.reshape(depth, 1, d3))


def _inproj_kernel(x_ref, sc_ref, sh_ref, g_ref, w_ref, avg_ref, qg_ref, kg_ref,
                   q_ref, k_ref, vt_ref, qi_ref, ki_ref, wit_ref, gates_ref, conv_ref, rw_ref):
    x = x_ref[...]
    ms = jnp.mean(x * x, axis=-1, keepdims=True)
    h = x * lax.rsqrt(ms + EPS) * g_ref[...]
    hb = (h * sc_ref[0] + sh_ref[0]).astype(BF16)

    def mm(c0, c1):
        return jnp.dot(hb, w_ref[:, c0:c1], preferred_element_type=F32)

    def head_rms(t, g):
        m2 = jnp.dot((t * t).astype(BF16), avg_ref[...], preferred_element_type=F32)
        return t * lax.rsqrt(m2 + EPS) * g

    q_ref[...] = head_rms(mm(C_Q, C_Q + D_A), qg_ref[...]).astype(BF16)
    k_ref[...] = head_rms(mm(C_K, C_K + D_A), kg_ref[...]).astype(BF16)
    vt_ref[0] = mm(C_V, C_V + D_A).T.astype(BF16)
    qi_ref[...] = mm(C_QI, C_QI + 512).astype(BF16)
    ki_ref[...] = mm(C_KI, C_KI + LANES).astype(BF16)
    wit_ref[...] = mm(C_WI, C_WI + LANES).T[:N_IDX_HEADS, :]
    gates_ref[:, 0:512] = mm(C_GATES, C_GATES + 512)
    gates_ref[:, 512:1024] = mm(C_GATES + 512, C_GATES + 1024)
    conv_ref[...] = mm(C_CONV, C_CONV + 512)
    rw_ref[:, 0:512] = mm(C_RW, C_RW + 512)
    rw_ref[:, 512:1024] = mm(C_RW + 512, C_RW + 1024)
    rw_ref[:, 1024:RW_COLS] = mm(C_RW + 1024, C_RW + RW_COLS)


def _inproj(x2, sc1p, shift, norm_g, w, avg, qg, kg, seq, tm):
    n, d = x2.shape
    per_b = seq // tm
    row = lambda i: (i, 0)
    full = lambda i: (0, 0)
    bat = lambda i: (i // per_b, 0, 0)
    rows = lambda wd, dt: (pl.BlockSpec((tm, wd), row), jax.ShapeDtypeStruct((n, wd), dt))
    outs = [rows(D_A, BF16), rows(D_A, BF16),
            (pl.BlockSpec((1, D_A, tm), lambda i: (i, 0, 0)),
             jax.ShapeDtypeStruct((n // tm, D_A, tm), BF16)),
            rows(512, BF16), rows(LANES, BF16),
            (pl.BlockSpec((N_IDX_HEADS, tm), lambda i: (0, i)),
             jax.ShapeDtypeStruct((N_IDX_HEADS, n), F32)),
            rows(D_MIX, F32), rows(512, F32), rows(RW_COLS, F32)]
    return pl.pallas_call(
        _inproj_kernel,
        grid=(n // tm,),
        in_specs=[pl.BlockSpec((tm, d), row),
                  pl.BlockSpec((1, 1, d), bat),
                  pl.BlockSpec((1, 1, d), bat),
                  pl.BlockSpec((1, d), full),
                  pl.BlockSpec((d, N_COLS), full),
                  pl.BlockSpec((D_A, D_A), full),
                  pl.BlockSpec((1, D_A), full),
                  pl.BlockSpec((1, D_A), full)],
        out_specs=[o[0] for o in outs],
        out_shape=[o[1] for o in outs],
        compiler_params=pltpu.CompilerParams(
            dimension_semantics=("arbitrary",), vmem_limit_bytes=VMEM_LIMIT),
    )(x2, sc1p, shift, norm_g, w, avg, qg, kg)


def _sortable(s):
    b = lax.bitcast_convert_type(s, jnp.int32)
    return b ^ ((b >> 31) & jnp.int32(0x7FFFFFFF))


def _fold(t, op):
    out = t[0:8, :]
    for i in range(1, TILE // 8):
        out = op(out, t[8 * i:8 * i + 8, :])
    return out


def _attn_kernel(q_ref, k_ref, vt_ref, qi_ref, ki_ref, wit_ref, bias_ref, o_ref,
                 key_scr, qih_scr, qh_scr, mb_scr, s_scr, m_scr, acc_scr,
                 *, topk, pos_bits):
    qb = pl.program_id(1)
    nst = qb // GROUP + 1
    kf = float(topk)
    lane = lax.broadcasted_iota(jnp.int32, (TILE, LANES), 1)
    krow = lax.broadcasted_iota(jnp.int32, (TILE, LANES), 0)
    lo_half = lane < HEAD_DIM

    wit = wit_ref[...]
    qi = qi_ref[0]
    q = q_ref[0]
    for src, dst, n_pairs in ((qi, qih_scr, N_IDX_HEADS // 2), (q, qh_scr, H_A // 2)):
        for g in range(n_pairs):
            pair = src[:, g * LANES:(g + 1) * LANES]
            dst[g, 0:TILE, :] = jnp.where(lo_half, pair, jnp.zeros_like(pair))
            dst[g, TILE:2 * TILE, :] = jnp.where(lo_half, jnp.zeros_like(pair), pair)

    adm = (krow // CHUNK) <= (lane // CHUNK)

    def p1(st, c):
        base = st * GROUP
        kis = ki_ref[0, pl.ds(pl.multiple_of(base * TILE, GROUP * TILE), GROUP * TILE), :]
        acc = [None] * GROUP
        for g in range(N_IDX_HEADS // 2):
            d = lax.dot_general(kis, qih_scr[g], NT_DIMS, preferred_element_type=F32)
            for par in range(2):
                w_row = wit[2 * g + par:2 * g + par + 1, :]
                for u in range(GROUP):
                    t = jnp.maximum(d[u * TILE:(u + 1) * TILE, par * TILE:(par + 1) * TILE], 0.0) * w_row
                    acc[u] = t if acc[u] is None else acc[u] + t
        for u in range(GROUP):
            kt = base + u
            diag = jnp.where(adm, _sortable(acc[u]), NEG_KEY)
            key_scr[kt] = jnp.where(kt < qb, _sortable(acc[u]), jnp.where(kt == qb, diag, NEG_KEY))
        return c

    lax.fori_loop(0, nst, p1, 0)

    def count_ge(thr_row):
        def body(st, c):
            for u in range(GROUP):
                c = c + _fold(jnp.where(key_scr[st * GROUP + u] >= thr_row, 1.0, 0.0), jnp.add)
            return c
        c = lax.fori_loop(0, nst, body, jnp.zeros((8, LANES), F32))
        return jnp.sum(c, axis=0, keepdims=True)

    def bisect(_, lohi):
        lo, hi = lohi
        mid = (lo >> 1) + (hi >> 1) + (lo & hi & 1)
        ge = count_ge(mid) >= kf
        return jnp.where(ge, mid, lo), jnp.where(ge, hi, mid)

    lo0 = jnp.full((1, LANES), INT_MIN, jnp.int32)
    hi0 = jnp.full((1, LANES), INT_MAX, jnp.int32)
    thr, _ = lax.fori_loop(0, 32, bisect, (lo0, hi0))
    cnt_ge = count_ge(thr)
    tie = jnp.logical_and(cnt_ge > kf, thr > NEG_KEY)

    @pl.when(jnp.max(jnp.where(tie, 1.0, 0.0)) > 0.0)
    def _():
        need = kf - count_ge(thr + 1)

        def count_eq_le(pmax):
            def body(kt, c):
                hit = jnp.where(key_scr[kt] == thr, krow + kt * TILE, INT_MAX) <= pmax
                return c + _fold(jnp.where(hit, 1.0, 0.0), jnp.add)
            c = lax.fori_loop(0, nst * GROUP, body, jnp.zeros((8, LANES), F32))
            return jnp.sum(c, axis=0, keepdims=True)

        def pbisect(_, lohi):
            plo, phi = lohi
            pmid = (plo + phi) >> 1
            ok = count_eq_le(pmid) >= need
            return jnp.where(ok, plo, pmid), jnp.where(ok, pmid, phi)

        plo0 = jnp.full((1, LANES), -1, jnp.int32)
        phi0 = jnp.full((1, LANES), (1 << pos_bits) - 1, jnp.int32)
        _, pthr = lax.fori_loop(0, pos_bits + 1, pbisect, (plo0, phi0))
        pthr = jnp.where(tie, pthr, INT_MAX)

        def drop(kt, c):
            kk = key_scr[kt]
            pos = jnp.where(kk == thr, krow + kt * TILE, -1)
            key_scr[kt] = jnp.where(pos > pthr, NEG_KEY, kk)
            return c

        lax.fori_loop(0, nst * GROUP, drop, 0)

    thr_eff = jnp.maximum(thr, NEG_KEY + 1)

    for h in range(H_A):
        m_scr[h] = jnp.full((8, LANES), NEG_BIG, F32)
        acc_scr[h] = jnp.zeros((HEAD_DIM + ONES_ROWS, LANES), F32)

    def p3a(st, c):
        base = st * GROUP
        k_s = k_ref[0, pl.ds(pl.multiple_of(base * TILE, GROUP * TILE), GROUP * TILE), :]
        for u in range(GROUP):
            mb_scr[u] = jnp.where(key_scr[base + u] >= thr_eff, 0.0, NEG_BIG)
        for g in range(H_A // 2):
            s = lax.dot_general(k_s[:, g * LANES:(g + 1) * LANES], qh_scr[g], NT_DIMS,
                                preferred_element_type=F32)
            for par in range(2):
                h = 2 * g + par
                mx = m_scr[h]
                for u in range(GROUP):
                    dd = jnp.clip(base + u - qb, -N_FAR, 0) + N_FAR
                    su = (s[u * TILE:(u + 1) * TILE, par * TILE:(par + 1) * TILE]
                          + bias_ref[h * (N_FAR + 1) + dd] + mb_scr[u])
                    s_scr[h, base + u] = su
                    mx = jnp.maximum(mx, _fold(su, jnp.maximum))
                m_scr[h] = mx
        return c

    lax.fori_loop(0, nst, p3a, 0)
    m_row = [jnp.max(m_scr[h], axis=0, keepdims=True) for h in range(H_A)]

    ones_rows = jnp.ones((ONES_ROWS, GROUP * TILE), BF16)

    def p3b(st, c):
        base = st * GROUP
        for h in range(H_A):
            ps = [jnp.exp((s_scr[h, base + u] - m_row[h]).astype(BF16)) for u in range(GROUP)]
            lhs = jnp.concatenate([vt_ref[st, h * HEAD_DIM:(h + 1) * HEAD_DIM, :], ones_rows], axis=0)
            acc_scr[h] += jnp.dot(lhs, jnp.concatenate(ps, axis=0), preferred_element_type=F32)
        return c

    lax.fori_loop(0, nst, p3b, 0)

    outs = []
    for h in range(H_A):
        acc = acc_scr[h]
        outs.append(acc[0:HEAD_DIM, :] / acc[HEAD_DIM:HEAD_DIM + 1, :])
    o_ref[0] = jnp.concatenate(outs, axis=0).T


def _attention(q, k, vt, qi, ki, wit, bias_tiles, topk):
    b, s, _ = q.shape
    nt = s // TILE
    ng = s // (GROUP * TILE)
    kern = functools.partial(_attn_kernel, topk=topk, pos_bits=max(1, int(math.ceil(math.log2(s)))))
    qtile = lambda bb, i: (bb, i, 0)
    whole = lambda bb, i: (bb, 0, 0)
    once = pl.Buffered(1)
    return pl.pallas_call(
        kern,
        grid=(b, nt),
        in_specs=[pl.BlockSpec((1, TILE, D_A), qtile),
                  pl.BlockSpec((1, s, D_A), whole, pipeline_mode=once),
                  pl.BlockSpec((ng, D_A, GROUP * TILE), whole, pipeline_mode=once),
                  pl.BlockSpec((1, TILE, 512), qtile),
                  pl.BlockSpec((1, s, LANES), whole, pipeline_mode=once),
                  pl.BlockSpec((N_IDX_HEADS, TILE), lambda bb, i: (0, bb * nt + i)),
                  pl.BlockSpec(bias_tiles.shape, lambda bb, i: (0, 0, 0), pipeline_mode=once)],
        out_specs=pl.BlockSpec((1, TILE, D_A), qtile),
        out_shape=jax.ShapeDtypeStruct((b, s, D_A), F32),
        scratch_shapes=[pltpu.VMEM((nt, TILE, LANES), jnp.int32),
                        pltpu.VMEM((N_IDX_HEADS // 2, 2 * TILE, LANES), BF16),
                        pltpu.VMEM((H_A // 2, 2 * TILE, LANES), BF16),
                        pltpu.VMEM((GROUP, TILE, LANES), F32),
                        pltpu.VMEM((H_A, nt, TILE, LANES), F32),
                        pltpu.VMEM((H_A, 8, LANES), F32),
                        pltpu.VMEM((H_A, HEAD_DIM + ONES_ROWS, LANES), F32)],
        compiler_params=pltpu.CompilerParams(
            dimension_semantics=("arbitrary", "arbitrary"), vmem_limit_bytes=VMEM_LIMIT),
    )(q, k, vt, qi, ki, wit, bias_tiles)


def _t5_bucket(rel):
    nb = N_BUCKETS // 2
    max_exact = nb // 2
    ret = jnp.where(rel > 0, nb, 0)
    n = jnp.abs(rel)
    nf = jnp.maximum(n, 1).astype(F32)
    large = max_exact + (jnp.log(nf / max_exact) / math.log(MAX_DISTANCE / max_exact)
                         * (nb - max_exact)).astype(jnp.int32)
    large = jnp.minimum(large, nb - 1)
    return ret + jnp.where(n < max_exact, n, large)


def _bias_tiles(rel_bias):
    j = jnp.arange(TILE, dtype=jnp.int32)[:, None]
    t = jnp.arange(TILE, dtype=jnp.int32)[None, :]
    tiles = [rel_bias[_t5_bucket((dd - N_FAR) * TILE + j - t)] for dd in range(N_FAR + 1)]
    arr = jnp.stack(tiles).astype(F32)
    return arr.transpose(3, 0, 1, 2).reshape(H_A * (N_FAR + 1), TILE, TILE)


CONV_HALO = 32


def _conv_kernel(cur_ref, halo_ref, w_ref, b_ref, g_ref, beta_ref, o_ref, u_scr, *, tt):
    i = pl.program_id(1)

    def glu(t):
        return t[:, :D_B] * _sigmoid(t[:, D_B:])

    u_scr[0:CONV_HALO, :] = jnp.where(i > 0, glu(halo_ref[0]), 0.0)
    u_scr[CONV_HALO:CONV_HALO + tt, :] = glu(cur_ref[0])
    rows = 64
    first = CONV_HALO - (CONV_WIDTH - 1)
    for r0 in range(0, tt, rows):
        acc = jnp.broadcast_to(b_ref[...], (rows, D_B))
        for j in range(CONV_WIDTH):
            acc = acc + w_ref[j:j + 1, :] * u_scr[r0 + first + j:r0 + first + j + rows, :]
        mu = jnp.mean(acc, axis=-1, keepdims=True)
        cen = acc - mu
        var = jnp.mean(cen * cen, axis=-1, keepdims=True)
        y = cen * lax.rsqrt(var + EPS) * g_ref[...] + beta_ref[...]
        o_ref[0, r0:r0 + rows, :] = _silu(y)


def _conv(conv_in, w, b, g, beta, tt):
    bsz, s, _ = conv_in.shape
    per = tt // CONV_HALO
    vec = lambda bb, i: (0, 0)
    return pl.pallas_call(
        functools.partial(_conv_kernel, tt=tt),
        grid=(bsz, s // tt),
        in_specs=[pl.BlockSpec((1, tt, 2 * D_B), lambda bb, i: (bb, i, 0)),
                  pl.BlockSpec((1, CONV_HALO, 2 * D_B),
                               lambda bb, i: (bb, jnp.maximum(i * per - 1, 0), 0)),
                  pl.BlockSpec((CONV_WIDTH, D_B), vec),
                  pl.BlockSpec((1, D_B), vec),
                  pl.BlockSpec((1, D_B), vec),
                  pl.BlockSpec((1, D_B), vec)],
        out_specs=pl.BlockSpec((1, tt, D_B), lambda bb, i: (bb, i, 0)),
        out_shape=jax.ShapeDtypeStruct((bsz, s, D_B), F32),
        scratch_shapes=[pltpu.VMEM((CONV_HALO + tt, D_B), F32)],
        compiler_params=pltpu.CompilerParams(
            dimension_semantics=("arbitrary", "arbitrary"), vmem_limit_bytes=VMEM_LIMIT),
    )(conv_in, conv_in, w, b, g, beta)


def _rwkv_kernel(rw_ref, mu_ref, lora_ref, w0_ref, a0_ref, kk_ref, ka_ref, rk_ref, lg_ref, lb_ref,
                 o_ref, s_scr, prev_scr):
    c = pl.program_id(1)
    n = CHUNK
    nb = RW_CHUNKS * CHUNK

    @pl.when(c == 0)
    def _():
        s_scr[...] = jnp.zeros_like(s_scr)
        prev_scr[...] = jnp.zeros_like(prev_scr)

    ps = rw_ref[0]
    row = lax.broadcasted_iota(jnp.int32, (nb, 1), 0)
    prev = jnp.where(row == 0, prev_scr[0:1, :], pltpu.roll(ps, 1, axis=0))
    prev_scr[0:1, :] = ps[nb - 1:nb, :]
    xs = ps + mu_ref[...] * (prev - ps)
    r = xs[:, 0:D_C]
    k = xs[:, D_C:2 * D_C]
    v = xs[:, 2 * D_C:3 * D_C]
    dn = xs[:, 3 * D_C:3 * D_C + 2 * D_LORA]
    lane64 = lax.broadcasted_iota(jnp.int32, (nb, 2 * D_LORA), 1)
    dn = jnp.where(lane64 < D_LORA, jnp.tanh(dn), dn)
    pre = jnp.dot(dn, lora_ref[...], precision=lax.Precision.HIGHEST, preferred_element_type=F32)
    z = -(w0_ref[...] + pre[:, :D_C])
    softplus = jnp.maximum(z, 0.0) + jnp.log(1.0 + jnp.exp(-jnp.abs(z)))
    logdec = -jnp.exp(-softplus - 0.5)
    a = _sigmoid(a0_ref[...] + pre[:, D_C:])
    kkraw = k * kk_ref[...]
    k2 = k * (1.0 + (a - 1.0) * ka_ref[...])
    bonus_pre = r * k2 * rk_ref[...]

    rb = lax.broadcasted_iota(jnp.int32, (nb, nb), 0)
    cb = lax.broadcasted_iota(jnp.int32, (nb, nb), 1)
    tri = jnp.where(jnp.logical_and(rb // n == cb // n, cb <= rb), 1.0, 0.0)
    cum_all = jnp.dot(tri, logdec, precision=lax.Precision.HIGHEST, preferred_element_type=F32)
    p_in_all = jnp.exp(cum_all)
    p_inv_all = jnp.exp(-cum_all)
    p_ex_all = jnp.exp(cum_all - logdec)

    row_n = lax.broadcasted_iota(jnp.int32, (n, LANES), 0)
    lane_n = lax.broadcasted_iota(jnp.int32, (n, LANES), 1)
    lo = lane_n < HEAD_DIM
    col = lane_n & (HEAD_DIM - 1)
    incl = col <= row_n
    strict = col < row_n
    mask2 = jnp.concatenate([strict, incl], axis=0)
    r2 = lax.broadcasted_iota(jnp.int32, (2 * n, LANES), 0)
    l2 = lax.broadcasted_iota(jnp.int32, (2 * n, LANES), 1)
    same_head = (r2 // HEAD_DIM) == (l2 // HEAD_DIM)
    ones_bd = jnp.where(same_head, 1.0, 0.0).astype(BF16)

    def bd(t):
        tb = t.astype(BF16)
        z = jnp.zeros_like(tb)
        return jnp.concatenate([jnp.where(lo, tb, z), jnp.where(lo, z, tb)], axis=0)

    def hsum(t):
        return jnp.dot(t.astype(BF16), ones_bd, preferred_element_type=F32)

    pairs = range(H_C // 2)
    chains = [(j, g) for j in range(RW_CHUNKS) for g in pairs]
    ids = range(len(chains))
    sl = lambda t, i: t[chains[i][0] * n:(chains[i][0] + 1) * n,
                        chains[i][1] * LANES:(chains[i][1] + 1) * LANES]
    v_p = [sl(v, i) for i in ids]
    p_in = [sl(p_in_all, i) for i in ids]
    kkn = []
    for i in ids:
        kr = sl(kkraw, i)
        kkn.append(kr * lax.rsqrt(jnp.maximum(hsum(kr * kr), 1e-12)))
    rt = [sl(r, i) * p_in[i] for i in ids]
    at = [kkn[i] * sl(p_ex_all, i) for i in ids]
    kt = [sl(k2, i) * sl(p_inv_all, i) for i in ids]
    bt = [kkn[i] * sl(a, i) * sl(p_inv_all, i) for i in ids]
    lhs = [jnp.concatenate([at[i], rt[i]], axis=0).astype(BF16) for i in ids]
    ak = [lax.dot_general(lhs[i], bd(kt[i]), NT_DIMS, preferred_element_type=F32) for i in ids]
    ab = [lax.dot_general(lhs[i], bd(bt[i]), NT_DIMS, preferred_element_type=F32) for i in ids]
    av = [_bdot(jnp.where(mask2, ak[i], 0.0), bd(v_p[i])) for i in ids]
    a_rb = [jnp.where(incl, ab[i][n:], 0.0) for i in ids]
    npow = [jnp.where(strict, -ab[i][:n], 0.0) for i in ids]
    tm = list(npow)
    for _ in range(5):
        both = [_bdot(jnp.concatenate([npow[i], tm[i]], axis=0), bd(npow[i])) for i in ids]
        npow = [both[i][:n] for i in ids]
        tm = [tm[i] + both[i][n:] for i in ids]
    tm = [tm[i] + _bdot(tm[i], bd(npow[i])) for i in ids]
    wt = [at[i] + _bdot(tm[i], bd(at[i])) for i in ids]
    uloc = [av[i][:n] + _bdot(tm[i], bd(av[i][:n])) for i in ids]
    mcat = [jnp.concatenate([kt[i], bt[i]], axis=0).astype(BF16) for i in ids]
    bonus = [hsum(sl(bonus_pre, i)) * v_p[i] for i in ids]

    state = [s_scr[g] for g in pairs]
    for j in range(RW_CHUNKS):
        cid = [j * (H_C // 2) + g for g in pairs]
        ws = [lax.dot_general(jnp.concatenate([wt[i], rt[i]], axis=0).astype(BF16),
                              state[g].astype(BF16), NT_DIMS, preferred_element_type=F32)
              for g, i in zip(pairs, cid)]
        u = [ws[g][:n] + uloc[i] for g, i in zip(pairs, cid)]
        o = [ws[g][n:] + av[i][n:] - _bdot(a_rb[i], bd(u[g])) for g, i in zip(pairs, cid)]
        for g, i in zip(pairs, cid):
            ncat = jnp.concatenate([v_p[i], -u[g]], axis=0).astype(BF16)
            upd = lax.dot_general(ncat, mcat[i], TN_DIMS, preferred_element_type=F32)
            state[g] = (state[g] + jnp.where(same_head, upd, 0.0)) * p_in[i][n - 1:n, :]
        for g, i in zip(pairs, cid):
            lanes = slice(g * LANES, (g + 1) * LANES)
            cen = o[g] - hsum(o[g]) * (1.0 / HEAD_DIM)
            var = hsum(cen * cen) * (1.0 / HEAD_DIM)
            y = cen * lax.rsqrt(var + LNX_EPS) * lg_ref[:, lanes] + lb_ref[:, lanes]
            o_ref[0, j * n:(j + 1) * n, lanes] = y + bonus[i]
    for g in pairs:
        s_scr[g] = state[g]


def _rwkv(rw, mu, lora, w0, a0, kk, ka, rk, lg, lb):
    bsz, s, _ = rw.shape
    vec = lambda bb, i: (0, 0)
    return pl.pallas_call(
        _rwkv_kernel,
        grid=(bsz, s // (RW_CHUNKS * CHUNK)),
        in_specs=[pl.BlockSpec((1, RW_CHUNKS * CHUNK, RW_COLS), lambda bb, i: (bb, i, 0)),
                  pl.BlockSpec((1, RW_COLS), vec),
                  pl.BlockSpec((2 * D_LORA, 2 * D_C), vec)] +
                 [pl.BlockSpec((1, D_C), vec)] * 7,
        out_specs=pl.BlockSpec((1, RW_CHUNKS * CHUNK, D_C), lambda bb, i: (bb, i, 0)),
        out_shape=jax.ShapeDtypeStruct((bsz, s, D_C), F32),
        scratch_shapes=[pltpu.VMEM((H_C // 2, 2 * HEAD_DIM, LANES), F32),
                        pltpu.VMEM((8, RW_COLS), F32)],
        compiler_params=pltpu.CompilerParams(
            dimension_semantics=("arbitrary", "arbitrary"), vmem_limit_bytes=VMEM_LIMIT),
    )(rw, mu, lora, w0, a0, kk, ka, rk, lg, lb)


def _outproj_kernel(x_ref, attn_ref, conv_ref, rw_ref, gates_ref, gm_ref, w_ref, o_ref):
    sg = _silu(gates_ref[...])
    ya = (attn_ref[...] * sg[:, 0:D_A]).astype(BF16)
    yb = (conv_ref[...] * sg[:, D_A:D_A + D_B]).astype(BF16)
    yc = (rw_ref[...] * sg[:, D_A + D_B:D_MIX]).astype(BF16)
    y = (jnp.dot(ya, w_ref[0:D_A, :], preferred_element_type=F32)
         + jnp.dot(yb, w_ref[D_A:D_A + D_B, :], preferred_element_type=F32)
         + jnp.dot(yc, w_ref[D_A + D_B:D_MIX, :], preferred_element_type=F32))
    o_ref[...] = x_ref[...] + gm_ref[0] * y


def _outproj(x2, attn, conv, rw, gates, gate_mod, w, seq, tm):
    n, d = x2.shape
    per_b = seq // tm
    row = lambda i: (i, 0)
    return pl.pallas_call(
        _outproj_kernel,
        grid=(n // tm,),
        in_specs=[pl.BlockSpec((tm, d), row),
                  pl.BlockSpec((tm, D_A), row),
                  pl.BlockSpec((tm, D_B), row),
                  pl.BlockSpec((tm, D_C), row),
                  pl.BlockSpec((tm, D_MIX), row),
                  pl.BlockSpec((1, 1, d), lambda i: (i // per_b, 0, 0)),
                  pl.BlockSpec((D_MIX, d), lambda i: (0, 0))],
        out_specs=pl.BlockSpec((tm, d), row),
        out_shape=jax.ShapeDtypeStruct((n, d), F32),
        compiler_params=pltpu.CompilerParams(
            dimension_semantics=("arbitrary",), vmem_limit_bytes=VMEM_LIMIT),
    )(x2, attn, conv, rw, gates, gate_mod, w)


def _reorder_w_in(w_in):
    depth, d, _ = w_in.shape
    a_cols = 4 * D_A + N_IDX_HEADS * D_IDX + D_IDX + N_IDX_HEADS
    b0 = a_cols
    c0 = a_cols + 3 * D_B
    c_shift = 3 * D_C + 2 * D_LORA
    zeros = lambda m: jnp.zeros((depth, d, m), w_in.dtype)
    qi0 = 4 * D_A
    ki0 = qi0 + N_IDX_HEADS * D_IDX
    wi0 = ki0 + D_IDX
    parts = [w_in[..., 0:3 * D_A],
             w_in[..., qi0:ki0],
             w_in[..., ki0:wi0], w_in[..., ki0:wi0],
             w_in[..., wi0:a_cols], zeros(LANES - N_IDX_HEADS),
             w_in[..., 3 * D_A:4 * D_A], w_in[..., b0 + 2 * D_B:b0 + 3 * D_B],
             w_in[..., c0 + c_shift:c0 + c_shift + D_C],
             w_in[..., b0:b0 + 2 * D_B],
             w_in[..., c0:c0 + c_shift], zeros(RW_COLS - c_shift)]
    w = jnp.concatenate(parts, axis=-1)
    assert w.shape[-1] == N_COLS
    return w.astype(BF16)


def kernel(x, c, norm_g, w_ada, b_ada, w_in, w_out, q_norm_g, k_norm_g, rel_bias, conv_w, conv_b,
           conv_ln_g, conv_ln_b, shift_mu, decay_w0, decay_up, iclr_a0, iclr_up, key_k, key_a,
           bonus_r_k, lnx_g, lnx_b):
    bsz, seq, d = x.shape
    depth = w_in.shape[0]
    assert seq % (GROUP * TILE) == 0 and d == w_out.shape[-1]
    topk = min(TOPK_MAX, seq // 4)
    tm = GROUP * TILE
    conv_tt = 256

    mod = _modulation(c, w_ada, b_ada)
    shift = mod[:, :, None, 0:d]
    sc1p = 1.0 + mod[:, :, None, d:2 * d]
    gate_mod = mod[:, :, None, 2 * d:3 * d]

    w_all = _reorder_w_in(w_in)
    head = jnp.arange(D_A) // HEAD_DIM
    avg = (head[:, None] == head[None, :]).astype(BF16) * (1.0 / HEAD_DIM)
    qg = jnp.tile(q_norm_g, (1, H_A))[:, None, :] * (HEAD_DIM ** -0.5)
    kg = jnp.tile(k_norm_g, (1, H_A))[:, None, :]
    bias_tiles = _bias_tiles(rel_bias)
    c_shift = 3 * D_C + 2 * D_LORA
    mu_pad = jnp.pad(shift_mu, ((0, 0), (0, RW_COLS - c_shift)))[:, None, :]
    zl = jnp.zeros((depth, D_LORA, D_C), F32)
    lora = jnp.concatenate([jnp.concatenate([decay_up, zl], axis=2),
                            jnp.concatenate([zl, iclr_up], axis=2)], axis=1)
    row = lambda t: t[:, None, :]
    layers = dict(
        shift=shift, sc1p=sc1p, gate_mod=gate_mod, norm_g=row(norm_g), w=w_all, qg=qg, kg=kg,
        w_out=w_out.astype(BF16), conv_w=conv_w, conv_b=row(conv_b), conv_g=row(conv_ln_g),
        conv_beta=row(conv_ln_b), mu=mu_pad, lora=lora, w0=row(decay_w0), a0=row(iclr_a0),
        kk=row(key_k), ka=row(key_a), rk=row(bonus_r_k.reshape(depth, D_C)), lg=row(lnx_g),
        lb=row(lnx_b))

    def layer(x2, p):
        q, k, vt, qi, ki, wit, gates, conv_in, rw_in = _inproj(
            x2, p["sc1p"], p["shift"], p["norm_g"], p["w"], avg, p["qg"], p["kg"], seq, tm)
        r3 = lambda t: t.reshape(bsz, seq, t.shape[-1])
        attn = _attention(r3(q), r3(k), vt, r3(qi), r3(ki), wit, bias_tiles, topk)
        conv = _conv(r3(conv_in), p["conv_w"], p["conv_b"], p["conv_g"], p["conv_beta"], conv_tt)
        rwo = _rwkv(r3(rw_in), p["mu"], p["lora"], p["w0"], p["a0"], p["kk"], p["ka"], p["rk"],
                    p["lg"], p["lb"])
        x2 = _outproj(x2, attn.reshape(-1, D_A), conv.reshape(-1, D_B), rwo.reshape(-1, D_C),
                      gates, p["gate_mod"], p["w_out"], seq, tm)
        return x2

    x2 = x.reshape(bsz * seq, d)
    for l in range(depth):
        x2 = layer(x2, {name: t[l] for name, t in layers.items()})
    return x2.reshape(bsz, seq, d)
```

```python
import functools
import math

import jax
import jax.numpy as jnp
import numpy as np
from jax import lax
from jax.experimental import pallas as pl
from jax.experimental.pallas import tpu as pltpu

F32 = jnp.float32
BF16 = jnp.bfloat16

HEAD_DIM = 64
CHUNK = 64
EPS = 1e-6
H_A = 6
D_A = H_A * HEAD_DIM
N_IDX_HEADS = 8
D_IDX = 64
TOPK_MAX = 256
N_BUCKETS = 32
MAX_DISTANCE = 1024
D_B = 256
CONV_WIDTH = 31
H_C = 6
D_C = H_C * HEAD_DIM
D_LORA = 32
LNX_EPS = 64e-5
D_MIX = D_A + D_B + D_C

LANES = 128
TILE = 128
GROUP = 4
BRACKET = 16.0
NARROW_STEPS = 26
LATE_STEPS = 10
ONES_ROWS = 16
RW_CHUNKS = 4
N_FAR = 9
VMEM_LIMIT = 48 * 1024 * 1024

C_Q, C_K, C_V = 0, 384, 768
C_QI = 1152
C_KI = 1664
C_WI = 1792
C_GATES = 1920
C_CONV = 2944
C_RW = 3456
N_COLS = 4736
RW_COLS = 1280

INT_MIN = -(2 ** 31)
INT_MAX = 2 ** 31 - 1
NEG_KEY = int(np.array(-np.inf, np.float32).view(np.int32)) ^ 0x7FFFFFFF
NEG_BIG = -1e30

NT_DIMS = (((1,), (1,)), ((), ()))
TN_DIMS = (((0,), (0,)), ((), ()))


def _bdot(a, b):
    return jnp.dot(a.astype(BF16), b.astype(BF16), preferred_element_type=F32)


def _bdot_nt(a, b):
    return lax.dot_general(a.astype(BF16), b.astype(BF16), NT_DIMS, preferred_element_type=F32)


def _sigmoid(x):
    return 1.0 / (1.0 + jnp.exp(-x))


def _silu(x):
    return x * _sigmoid(x)


def _mod_kernel(c_ref, w_ref, b_ref, o_ref):
    ca = _silu(c_ref[...])
    o_ref[0] = jnp.dot(ca, w_ref[0], precision=lax.Precision.HIGHEST,
                       preferred_element_type=F32) + b_ref[0]


def _modulation(c, w_ada, b_ada):
    depth, d, d3 = w_ada.shape
    b = c.shape[0]
    return pl.pallas_call(
        _mod_kernel,
        grid=(depth, d3 // d),
        in_specs=[pl.BlockSpec((b, d), lambda l, j: (0, 0)),
                  pl.BlockSpec((1, d, d), lambda l, j: (l, 0, j)),
                  pl.BlockSpec((1, 1, d), lambda l, j: (l, 0, j))],
        out_specs=pl.BlockSpec((1, b, d), lambda l, j: (l, 0, j)),
        out_shape=jax.ShapeDtypeStruct((depth, b, d3), F32),
        compiler_params=pltpu.CompilerParams(
            dimension_semantics=("arbitrary", "arbitrary"), vmem_limit_bytes=VMEM_LIMIT),
    )(c, w_ada, b_ada.reshape(depth, 1, d3))


def _inproj_kernel(x_ref, sc_ref, sh_ref, g_ref, w_ref, avg_ref, qg_ref, kg_ref,
                   q_ref, k_ref, vt_ref, qi_ref, ki_ref, wit_ref, gates_ref, conv_ref, rw_ref):
    x = x_ref[...]
    ms = jnp.mean(x * x, axis=-1, keepdims=True)
    h = x * lax.rsqrt(ms + EPS) * g_ref[...]
    hb = (h * sc_ref[0] + sh_ref[0]).astype(BF16)

    def mm(c0, c1):
        return jnp.dot(hb, w_ref[:, c0:c1], preferred_element_type=F32)

    def head_rms(t, g):
        m2 = jnp.dot((t * t).astype(BF16), avg_ref[...], preferred_element_type=F32)
        return t * lax.rsqrt(m2 + EPS) * g

    q_ref[...] = head_rms(mm(C_Q, C_Q + D_A), qg_ref[...]).astype(BF16)
    k_ref[...] = head_rms(mm(C_K, C_K + D_A), kg_ref[...]).astype(BF16)
    vt_ref[0] = mm(C_V, C_V + D_A).T.astype(BF16)
    qi_ref[...] = mm(C_QI, C_QI + 512).astype(BF16)
    ki_ref[...] = mm(C_KI, C_KI + LANES).astype(BF16)
    wit_ref[...] = mm(C_WI, C_WI + LANES).T[:N_IDX_HEADS, :]
    gates_ref[:, 0:512] = mm(C_GATES, C_GATES + 512)
    gates_ref[:, 512:1024] = mm(C_GATES + 512, C_GATES + 1024)
    conv_ref[...] = mm(C_CONV, C_CONV + 512)
    rw_ref[:, 0:512] = mm(C_RW, C_RW + 512)
    rw_ref[:, 512:1024] = mm(C_RW + 512, C_RW + 1024)
    rw_ref[:, 1024:RW_COLS] = mm(C_RW + 1024, C_RW + RW_COLS)


def _inproj(x2, sc1p, shift, norm_g, w, avg, qg, kg, seq, tm):
    n, d = x2.shape
    per_b = seq // tm
    row = lambda i: (i, 0)
    full = lambda i: (0, 0)
    bat = lambda i: (i // per_b, 0, 0)
    rows = lambda wd, dt: (pl.BlockSpec((tm, wd), row), jax.ShapeDtypeStruct((n, wd), dt))
    outs = [rows(D_A, BF16), rows(D_A, BF16),
            (pl.BlockSpec((1, D_A, tm), lambda i: (i, 0, 0)),
             jax.ShapeDtypeStruct((n // tm, D_A, tm), BF16)),
            rows(512, BF16), rows(LANES, BF16),
            (pl.BlockSpec((N_IDX_HEADS, tm), lambda i: (0, i)),
             jax.ShapeDtypeStruct((N_IDX_HEADS, n), F32)),
            rows(D_MIX, F32), rows(512, F32), rows(RW_COLS, F32)]
    return pl.pallas_call(
        _inproj_kernel,
        grid=(n // tm,),
        in_specs=[pl.BlockSpec((tm, d), row),
                  pl.BlockSpec((1, 1, d), bat),
                  pl.BlockSpec((1, 1, d), bat),
                  pl.BlockSpec((1, d), full),
                  pl.BlockSpec((d, N_COLS), full),
                  pl.BlockSpec((D_A, D_A), full),
                  pl.BlockSpec((1, D_A), full),
                  pl.BlockSpec((1, D_A), full)],
        out_specs=[o[0] for o in outs],
        out_shape=[o[1] for o in outs],
        compiler_params=pltpu.CompilerParams(
            dimension_semantics=("arbitrary",), vmem_limit_bytes=VMEM_LIMIT),
    )(x2, sc1p, shift, norm_g, w, avg, qg, kg)


def _sortable(s):
    b = lax.bitcast_convert_type(s, jnp.int32)
    return b ^ ((b >> 31) & jnp.int32(0x7FFFFFFF))


def _fold(t, op):
    out = t[0:8, :]
    for i in range(1, TILE // 8):
        out = op(out, t[8 * i:8 * i + 8, :])
    return out


def _attn_kernel(q_ref, k_ref, vt_ref, qi_ref, ki_ref, wit_ref, bias_ref, o_ref,
                 key_scr, qih_scr, qh_scr, mb_scr, s_scr, m_scr, acc_scr,
                 *, topk, pos_bits):
    qb = pl.program_id(1)
    nst = qb // GROUP + 1
    kf = float(topk)
    lane = lax.broadcasted_iota(jnp.int32, (TILE, LANES), 1)
    krow = lax.broadcasted_iota(jnp.int32, (TILE, LANES), 0)
    lo_half = lane < HEAD_DIM

    wit = wit_ref[...]
    qi = qi_ref[0]
    q = q_ref[0]
    for src, dst, n_pairs in ((qi, qih_scr, N_IDX_HEADS // 2), (q, qh_scr, H_A // 2)):
        for g in range(n_pairs):
            pair = src[:, g * LANES:(g + 1) * LANES]
            dst[g, 0:TILE, :] = jnp.where(lo_half, pair, jnp.zeros_like(pair))
            dst[g, TILE:2 * TILE, :] = jnp.where(lo_half, jnp.zeros_like(pair), pair)

    adm = (krow // CHUNK) <= (lane // CHUNK)

    def p1(st, c):
        base = st * GROUP
        kis = ki_ref[0, pl.ds(pl.multiple_of(base * TILE, GROUP * TILE), GROUP * TILE), :]
        acc = [None] * GROUP
        for g in range(N_IDX_HEADS // 2):
            d = lax.dot_general(kis, qih_scr[g], NT_DIMS, preferred_element_type=F32)
            for par in range(2):
                w_row = wit[2 * g + par:2 * g + par + 1, :]
                for u in range(GROUP):
                    t = jnp.maximum(d[u * TILE:(u + 1) * TILE, par * TILE:(par + 1) * TILE], 0.0) * w_row
                    acc[u] = t if acc[u] is None else acc[u] + t
        for u in range(GROUP):
            kt = base + u
            diag = jnp.where(adm, acc[u], -jnp.inf)
            sc = jnp.where(kt < qb, acc[u], jnp.where(kt == qb, diag, -jnp.inf))
            key_scr[kt] = _sortable(sc)
            c = jnp.maximum(c, _fold(sc, jnp.maximum))
        return c

    smax8 = lax.fori_loop(0, nst, p1, jnp.full((8, LANES), -jnp.inf, F32))
    fmax = jnp.max(smax8, axis=0, keepdims=True)

    def count_ge(thr_row):
        def body(st, c):
            for u in range(GROUP):
                c = c + _fold(jnp.where(key_scr[st * GROUP + u] >= thr_row, 1.0, 0.0), jnp.add)
            return c
        c = lax.fori_loop(0, nst, body, jnp.zeros((8, LANES), F32))
        return jnp.sum(c, axis=0, keepdims=True)

    def bisect(lo, hi):
        mid = (lo >> 1) + (hi >> 1) + (lo & hi & 1)
        cnt = count_ge(mid)
        ge = cnt >= kf
        hi = jnp.where(cnt == kf, mid + 1, jnp.where(ge, hi, mid))
        return jnp.where(ge, mid, lo), hi

    kmax = _sortable(fmax)
    lo_try = _sortable(jnp.where(fmax > 0.0, fmax * (1.0 / BRACKET), fmax * BRACKET))
    narrow = count_ge(lo_try) >= kf
    lo0 = jnp.where(narrow, lo_try, INT_MIN)
    hi0 = jnp.where(narrow, jnp.minimum(kmax, INT_MAX - 1) + 1, INT_MAX)
    all_narrow = jnp.min(jnp.where(narrow, 1.0, 0.0)) > 0.0
    n_steps = jnp.where(all_narrow, NARROW_STEPS, 32)
    lo, hi = lax.fori_loop(0, n_steps - LATE_STEPS, lambda _, lh: bisect(*lh), (lo0, hi0))

    def unsettled(lo, hi):
        return jnp.max(jnp.where(hi != lo + 1, 1.0, 0.0)) > 0.0

    def late(carry):
        lo, hi, it, _ = carry
        lo, hi = bisect(*bisect(lo, hi))
        return lo, hi, it + 2, unsettled(lo, hi)

    thr, _, _, _ = lax.while_loop(lambda carry: jnp.logical_and(carry[2] < n_steps, carry[3]),
                                  late, (lo, hi, n_steps - LATE_STEPS, unsettled(lo, hi)))
    cnt_ge = count_ge(thr)
    tie = jnp.logical_and(cnt_ge > kf, thr > NEG_KEY)

    @pl.when(jnp.max(jnp.where(tie, 1.0, 0.0)) > 0.0)
    def _():
        need = kf - count_ge(thr + 1)

        def count_eq_le(pmax):
            def body(kt, c):
                hit = jnp.where(key_scr[kt] == thr, krow + kt * TILE, INT_MAX) <= pmax
                return c + _fold(jnp.where(hit, 1.0, 0.0), jnp.add)
            c = lax.fori_loop(0, nst * GROUP, body, jnp.zeros((8, LANES), F32))
            return jnp.sum(c, axis=0, keepdims=True)

        def pbisect(_, lohi):
            plo, phi = lohi
            pmid = (plo + phi) >> 1
            ok = count_eq_le(pmid) >= need
            return jnp.where(ok, plo, pmid), jnp.where(ok, pmid, phi)

        plo0 = jnp.full((1, LANES), -1, jnp.int32)
        phi0 = jnp.full((1, LANES), (1 << pos_bits) - 1, jnp.int32)
        _, pthr = lax.fori_loop(0, pos_bits + 1, pbisect, (plo0, phi0))
        pthr = jnp.where(tie, pthr, INT_MAX)

        def drop(kt, c):
            kk = key_scr[kt]
            pos = jnp.where(kk == thr, krow + kt * TILE, -1)
            key_scr[kt] = jnp.where(pos > pthr, NEG_KEY, kk)
            return c

        lax.fori_loop(0, nst * GROUP, drop, 0)

    thr_eff = jnp.maximum(thr, NEG_KEY + 1)

    for h in range(H_A):
        m_scr[h] = jnp.full((8, LANES), NEG_BIG, F32)
        acc_scr[h] = jnp.zeros((HEAD_DIM + ONES_ROWS, LANES), F32)

    def p3a(st, c):
        base = st * GROUP
        k_s = k_ref[0, pl.ds(pl.multiple_of(base * TILE, GROUP * TILE), GROUP * TILE), :]
        for u in range(GROUP):
            mb_scr[u] = jnp.where(key_scr[base + u] >= thr_eff, 0.0, NEG_BIG)
        for g in range(H_A // 2):
            s = lax.dot_general(k_s[:, g * LANES:(g + 1) * LANES], qh_scr[g], NT_DIMS,
                                preferred_element_type=F32)
            for par in range(2):
                h = 2 * g + par
                mx = m_scr[h]
                for u in range(GROUP):
                    dd = jnp.clip(base + u - qb, -N_FAR, 0) + N_FAR
                    su = (s[u * TILE:(u + 1) * TILE, par * TILE:(par + 1) * TILE]
                          + bias_ref[h * (N_FAR + 1) + dd] + mb_scr[u])
                    s_scr[h, base + u] = su
                    mx = jnp.maximum(mx, _fold(su, jnp.maximum))
                m_scr[h] = mx
        return c

    lax.fori_loop(0, nst, p3a, 0)
    m_row = [jnp.max(m_scr[h], axis=0, keepdims=True) for h in range(H_A)]

    ones_rows = jnp.ones((ONES_ROWS, GROUP * TILE), BF16)

    def p3b(st, c):
        base = st * GROUP
        for h in range(H_A):
            ps = [jnp.exp((s_scr[h, base + u] - m_row[h]).astype(BF16)) for u in range(GROUP)]
            lhs = jnp.concatenate([vt_ref[st, h * HEAD_DIM:(h + 1) * HEAD_DIM, :], ones_rows], axis=0)
            acc_scr[h] += jnp.dot(lhs, jnp.concatenate(ps, axis=0), preferred_element_type=F32)
        return c

    lax.fori_loop(0, nst, p3b, 0)

    outs = []
    for h in range(H_A):
        acc = acc_scr[h]
        outs.append(acc[0:HEAD_DIM, :] / acc[HEAD_DIM:HEAD_DIM + 1, :])
    o_ref[0] = jnp.concatenate(outs, axis=0).T


def _attention(q, k, vt, qi, ki, wit, bias_tiles, topk):
    b, s, _ = q.shape
    nt = s // TILE
    ng = s // (GROUP * TILE)
    kern = functools.partial(_attn_kernel, topk=topk, pos_bits=max(1, int(math.ceil(math.log2(s)))))
    qtile = lambda bb, i: (bb, i, 0)
    whole = lambda bb, i: (bb, 0, 0)
    once = pl.Buffered(1)
    return pl.pallas_call(
        kern,
        grid=(b, nt),
        in_specs=[pl.BlockSpec((1, TILE, D_A), qtile),
                  pl.BlockSpec((1, s, D_A), whole, pipeline_mode=once),
                  pl.BlockSpec((ng, D_A, GROUP * TILE), whole, pipeline_mode=once),
                  pl.BlockSpec((1, TILE, 512), qtile),
                  pl.BlockSpec((1, s, LANES), whole, pipeline_mode=once),
                  pl.BlockSpec((N_IDX_HEADS, TILE), lambda bb, i: (0, bb * nt + i)),
                  pl.BlockSpec(bias_tiles.shape, lambda bb, i: (0, 0, 0), pipeline_mode=once)],
        out_specs=pl.BlockSpec((1, TILE, D_A), qtile),
        out_shape=jax.ShapeDtypeStruct((b, s, D_A), F32),
        scratch_shapes=[pltpu.VMEM((nt, TILE, LANES), jnp.int32),
                        pltpu.VMEM((N_IDX_HEADS // 2, 2 * TILE, LANES), BF16),
                        pltpu.VMEM((H_A // 2, 2 * TILE, LANES), BF16),
                        pltpu.VMEM((GROUP, TILE, LANES), F32),
                        pltpu.VMEM((H_A, nt, TILE, LANES), F32),
                        pltpu.VMEM((H_A, 8, LANES), F32),
                        pltpu.VMEM((H_A, HEAD_DIM + ONES_ROWS, LANES), F32)],
        compiler_params=pltpu.CompilerParams(
            dimension_semantics=("arbitrary", "arbitrary"), vmem_limit_bytes=VMEM_LIMIT),
    )(q, k, vt, qi, ki, wit, bias_tiles)


def _t5_bucket(rel):
    nb = N_BUCKETS // 2
    max_exact = nb // 2
    ret = jnp.where(rel > 0, nb, 0)
    n = jnp.abs(rel)
    nf = jnp.maximum(n, 1).astype(F32)
    large = max_exact + (jnp.log(nf / max_exact) / math.log(MAX_DISTANCE / max_exact)
                         * (nb - max_exact)).astype(jnp.int32)
    large = jnp.minimum(large, nb - 1)
    return ret + jnp.where(n < max_exact, n, large)


def _bias_tiles(rel_bias):
    j = jnp.arange(TILE, dtype=jnp.int32)[:, None]
    t = jnp.arange(TILE, dtype=jnp.int32)[None, :]
    tiles = [rel_bias[_t5_bucket((dd - N_FAR) * TILE + j - t)] for dd in range(N_FAR + 1)]
    arr = jnp.stack(tiles).astype(F32)
    return arr.transpose(3, 0, 1, 2).reshape(H_A * (N_FAR + 1), TILE, TILE)


CONV_HALO = 32


def _conv_kernel(cur_ref, halo_ref, w_ref, b_ref, g_ref, beta_ref, o_ref, u_scr, *, tt):
    i = pl.program_id(1)

    def glu(t):
        return t[:, :D_B] * _sigmoid(t[:, D_B:])

    u_scr[0:CONV_HALO, :] = jnp.where(i > 0, glu(halo_ref[0]), 0.0)
    u_scr[CONV_HALO:CONV_HALO + tt, :] = glu(cur_ref[0])
    rows = 64
    first = CONV_HALO - (CONV_WIDTH - 1)
    for r0 in range(0, tt, rows):
        acc = jnp.broadcast_to(b_ref[...], (rows, D_B))
        for j in range(CONV_WIDTH):
            acc = acc + w_ref[j:j + 1, :] * u_scr[r0 + first + j:r0 + first + j + rows, :]
        mu = jnp.mean(acc, axis=-1, keepdims=True)
        cen = acc - mu
        var = jnp.mean(cen * cen, axis=-1, keepdims=True)
        y = cen * lax.rsqrt(var + EPS) * g_ref[...] + beta_ref[...]
        o_ref[0, r0:r0 + rows, :] = _silu(y)


def _conv(conv_in, w, b, g, beta, tt):
    bsz, s, _ = conv_in.shape
    per = tt // CONV_HALO
    vec = lambda bb, i: (0, 0)
    return pl.pallas_call(
        functools.partial(_conv_kernel, tt=tt),
        grid=(bsz, s // tt),
        in_specs=[pl.BlockSpec((1, tt, 2 * D_B), lambda bb, i: (bb, i, 0)),
                  pl.BlockSpec((1, CONV_HALO, 2 * D_B),
                               lambda bb, i: (bb, jnp.maximum(i * per - 1, 0), 0)),
                  pl.BlockSpec((CONV_WIDTH, D_B), vec),
                  pl.BlockSpec((1, D_B), vec),
                  pl.BlockSpec((1, D_B), vec),
                  pl.BlockSpec((1, D_B), vec)],
        out_specs=pl.BlockSpec((1, tt, D_B), lambda bb, i: (bb, i, 0)),
        out_shape=jax.ShapeDtypeStruct((bsz, s, D_B), F32),
        scratch_shapes=[pltpu.VMEM((CONV_HALO + tt, D_B), F32)],
        compiler_params=pltpu.CompilerParams(
            dimension_semantics=("arbitrary", "arbitrary"), vmem_limit_bytes=VMEM_LIMIT),
    )(conv_in, conv_in, w, b, g, beta)


def _rwkv_kernel(rw_ref, mu_ref, lora_ref, w0_ref, a0_ref, kk_ref, ka_ref, rk_ref, lg_ref, lb_ref,
                 o_ref, s_scr, prev_scr):
    c = pl.program_id(1)
    n = CHUNK
    nb = RW_CHUNKS * CHUNK

    @pl.when(c == 0)
    def _():
        s_scr[...] = jnp.zeros_like(s_scr)
        prev_scr[...] = jnp.zeros_like(prev_scr)

    ps = rw_ref[0]
    row = lax.broadcasted_iota(jnp.int32, (nb, 1), 0)
    prev = jnp.where(row == 0, prev_scr[0:1, :], pltpu.roll(ps, 1, axis=0))
    prev_scr[0:1, :] = ps[nb - 1:nb, :]
    xs = ps + mu_ref[...] * (prev - ps)
    r = xs[:, 0:D_C]
    k = xs[:, D_C:2 * D_C]
    v = xs[:, 2 * D_C:3 * D_C]
    dn = xs[:, 3 * D_C:3 * D_C + 2 * D_LORA]
    lane64 = lax.broadcasted_iota(jnp.int32, (nb, 2 * D_LORA), 1)
    dn = jnp.where(lane64 < D_LORA, jnp.tanh(dn), dn)
    pre = jnp.dot(dn, lora_ref[...], precision=lax.Precision.HIGHEST, preferred_element_type=F32)
    z = -(w0_ref[...] + pre[:, :D_C])
    softplus = jnp.maximum(z, 0.0) + jnp.log(1.0 + jnp.exp(-jnp.abs(z)))
    logdec = -jnp.exp(-softplus - 0.5)
    a = _sigmoid(a0_ref[...] + pre[:, D_C:])
    kkraw = k * kk_ref[...]
    k2 = k * (1.0 + (a - 1.0) * ka_ref[...])
    bonus_pre = r * k2 * rk_ref[...]

    rb = lax.broadcasted_iota(jnp.int32, (nb, nb), 0)
    cb = lax.broadcasted_iota(jnp.int32, (nb, nb), 1)
    tri = jnp.where(jnp.logical_and(rb // n == cb // n, cb <= rb), 1.0, 0.0)
    cum_all = jnp.dot(tri, logdec, precision=lax.Precision.HIGHEST, preferred_element_type=F32)
    p_in_all = jnp.exp(cum_all)
    p_inv_all = jnp.exp(-cum_all)
    p_ex_all = jnp.exp(cum_all - logdec)

    row_n = lax.broadcasted_iota(jnp.int32, (n, LANES), 0)
    lane_n = lax.broadcasted_iota(jnp.int32, (n, LANES), 1)
    lo = lane_n < HEAD_DIM
    col = lane_n & (HEAD_DIM - 1)
    incl = col <= row_n
    strict = col < row_n
    mask2 = jnp.concatenate([strict, incl], axis=0)
    r2 = lax.broadcasted_iota(jnp.int32, (2 * n, LANES), 0)
    l2 = lax.broadcasted_iota(jnp.int32, (2 * n, LANES), 1)
    same_head = (r2 // HEAD_DIM) == (l2 // HEAD_DIM)
    ones_bd = jnp.where(same_head, 1.0, 0.0).astype(BF16)

    def bd(t):
        tb = t.astype(BF16)
        z = jnp.zeros_like(tb)
        return jnp.concatenate([jnp.where(lo, tb, z), jnp.where(lo, z, tb)], axis=0)

    def hsum(t):
        return jnp.dot(t.astype(BF16), ones_bd, preferred_element_type=F32)

    pairs = range(H_C // 2)
    chains = [(j, g) for j in range(RW_CHUNKS) for g in pairs]
    ids = range(len(chains))
    sl = lambda t, i: t[chains[i][0] * n:(chains[i][0] + 1) * n,
                        chains[i][1] * LANES:(chains[i][1] + 1) * LANES]
    v_p = [sl(v, i) for i in ids]
    p_in = [sl(p_in_all, i) for i in ids]
    kkn = []
    for i in ids:
        kr = sl(kkraw, i)
        kkn.append(kr * lax.rsqrt(jnp.maximum(hsum(kr * kr), 1e-12)))
    rt = [sl(r, i) * p_in[i] for i in ids]
    at = [kkn[i] * sl(p_ex_all, i) for i in ids]
    kt = [sl(k2, i) * sl(p_inv_all, i) for i in ids]
    bt = [kkn[i] * sl(a, i) * sl(p_inv_all, i) for i in ids]
    lhs = [jnp.concatenate([at[i], rt[i]], axis=0).astype(BF16) for i in ids]
    ak = [lax.dot_general(lhs[i], bd(kt[i]), NT_DIMS, preferred_element_type=F32) for i in ids]
    ab = [lax.dot_general(lhs[i], bd(bt[i]), NT_DIMS, preferred_element_type=F32) for i in ids]
    av = [_bdot(jnp.where(mask2, ak[i], 0.0), bd(v_p[i])) for i in ids]
    a_rb = [jnp.where(incl, ab[i][n:], 0.0) for i in ids]
    npow = [jnp.where(strict, -ab[i][:n], 0.0) for i in ids]
    tm = list(npow)
    for _ in range(5):
        both = [_bdot(jnp.concatenate([npow[i], tm[i]], axis=0), bd(npow[i])) for i in ids]
        npow = [both[i][:n] for i in ids]
        tm = [tm[i] + both[i][n:] for i in ids]
    tm = [tm[i] + _bdot(tm[i], bd(npow[i])) for i in ids]
    wt = [at[i] + _bdot(tm[i], bd(at[i])) for i in ids]
    uloc = [av[i][:n] + _bdot(tm[i], bd(av[i][:n])) for i in ids]
    mcat = [jnp.concatenate([kt[i], bt[i]], axis=0).astype(BF16) for i in ids]
    bonus = [hsum(sl(bonus_pre, i)) * v_p[i] for i in ids]

    state = [s_scr[g] for g in pairs]
    for j in range(RW_CHUNKS):
        cid = [j * (H_C // 2) + g for g in pairs]
        ws = [lax.dot_general(jnp.concatenate([wt[i], rt[i]], axis=0).astype(BF16),
                              state[g].astype(BF16), NT_DIMS, preferred_element_type=F32)
              for g, i in zip(pairs, cid)]
        u = [ws[g][:n] + uloc[i] for g, i in zip(pairs, cid)]
        o = [ws[g][n:] + av[i][n:] - _bdot(a_rb[i], bd(u[g])) for g, i in zip(pairs, cid)]
        for g, i in zip(pairs, cid):
            ncat = jnp.concatenate([v_p[i], -u[g]], axis=0).astype(BF16)
            upd = lax.dot_general(ncat, mcat[i], TN_DIMS, preferred_element_type=F32)
            state[g] = (state[g] + jnp.where(same_head, upd, 0.0)) * p_in[i][n - 1:n, :]
        for g, i in zip(pairs, cid):
            lanes = slice(g * LANES, (g + 1) * LANES)
            cen = o[g] - hsum(o[g]) * (1.0 / HEAD_DIM)
            var = hsum(cen * cen) * (1.0 / HEAD_DIM)
            y = cen * lax.rsqrt(var + LNX_EPS) * lg_ref[:, lanes] + lb_ref[:, lanes]
            o_ref[0, j * n:(j + 1) * n, lanes] = y + bonus[i]
    for g in pairs:
        s_scr[g] = state[g]


def _rwkv(rw, mu, lora, w0, a0, kk, ka, rk, lg, lb):
    bsz, s, _ = rw.shape
    vec = lambda bb, i: (0, 0)
    return pl.pallas_call(
        _rwkv_kernel,
        grid=(bsz, s // (RW_CHUNKS * CHUNK)),
        in_specs=[pl.BlockSpec((1, RW_CHUNKS * CHUNK, RW_COLS), lambda bb, i: (bb, i, 0)),
                  pl.BlockSpec((1, RW_COLS), vec),
                  pl.BlockSpec((2 * D_LORA, 2 * D_C), vec)] +
                 [pl.BlockSpec((1, D_C), vec)] * 7,
        out_specs=pl.BlockSpec((1, RW_CHUNKS * CHUNK, D_C), lambda bb, i: (bb, i, 0)),
        out_shape=jax.ShapeDtypeStruct((bsz, s, D_C), F32),
        scratch_shapes=[pltpu.VMEM((H_C // 2, 2 * HEAD_DIM, LANES), F32),
                        pltpu.VMEM((8, RW_COLS), F32)],
        compiler_params=pltpu.CompilerParams(
            dimension_semantics=("arbitrary", "arbitrary"), vmem_limit_bytes=VMEM_LIMIT),
    )(rw, mu, lora, w0, a0, kk, ka, rk, lg, lb)


def _outproj_kernel(x_ref, attn_ref, conv_ref, rw_ref, gates_ref, gm_ref, w_ref, o_ref):
    sg = _silu(gates_ref[...])
    ya = (attn_ref[...] * sg[:, 0:D_A]).astype(BF16)
    yb = (conv_ref[...] * sg[:, D_A:D_A + D_B]).astype(BF16)
    yc = (rw_ref[...] * sg[:, D_A + D_B:D_MIX]).astype(BF16)
    y = (jnp.dot(ya, w_ref[0:D_A, :], preferred_element_type=F32)
         + jnp.dot(yb, w_ref[D_A:D_A + D_B, :], preferred_element_type=F32)
         + jnp.dot(yc, w_ref[D_A + D_B:D_MIX, :], preferred_element_type=F32))
    o_ref[...] = x_ref[...] + gm_ref[0] * y


def _outproj(x2, attn, conv, rw, gates, gate_mod, w, seq, tm):
    n, d = x2.shape
    per_b = seq // tm
    row = lambda i: (i, 0)
    return pl.pallas_call(
        _outproj_kernel,
        grid=(n // tm,),
        in_specs=[pl.BlockSpec((tm, d), row),
                  pl.BlockSpec((tm, D_A), row),
                  pl.BlockSpec((tm, D_B), row),
                  pl.BlockSpec((tm, D_C), row),
                  pl.BlockSpec((tm, D_MIX), row),
                  pl.BlockSpec((1, 1, d), lambda i: (i // per_b, 0, 0)),
                  pl.BlockSpec((D_MIX, d), lambda i: (0, 0))],
        out_specs=pl.BlockSpec((tm, d), row),
        out_shape=jax.ShapeDtypeStruct((n, d), F32),
        compiler_params=pltpu.CompilerParams(
            dimension_semantics=("arbitrary",), vmem_limit_bytes=VMEM_LIMIT),
    )(x2, attn, conv, rw, gates, gate_mod, w)


def _reorder_w_in(w_in):
    depth, d, _ = w_in.shape
    a_cols = 4 * D_A + N_IDX_HEADS * D_IDX + D_IDX + N_IDX_HEADS
    b0 = a_cols
    c0 = a_cols + 3 * D_B
    c_shift = 3 * D_C + 2 * D_LORA
    zeros = lambda m: jnp.zeros((depth, d, m), w_in.dtype)
    qi0 = 4 * D_A
    ki0 = qi0 + N_IDX_HEADS * D_IDX
    wi0 = ki0 + D_IDX
    parts = [w_in[..., 0:3 * D_A],
             w_in[..., qi0:ki0],
             w_in[..., ki0:wi0], w_in[..., ki0:wi0],
             w_in[..., wi0:a_cols], zeros(LANES - N_IDX_HEADS),
             w_in[..., 3 * D_A:4 * D_A], w_in[..., b0 + 2 * D_B:b0 + 3 * D_B],
             w_in[..., c0 + c_shift:c0 + c_shift + D_C],
             w_in[..., b0:b0 + 2 * D_B],
             w_in[..., c0:c0 + c_shift], zeros(RW_COLS - c_shift)]
    w = jnp.concatenate(parts, axis=-1)
    assert w.shape[-1] == N_COLS
    return w.astype(BF16)


def kernel(x, c, norm_g, w_ada, b_ada, w_in, w_out, q_norm_g, k_norm_g, rel_bias, conv_w, conv_b,
           conv_ln_g, conv_ln_b, shift_mu, decay_w0, decay_up, iclr_a0, iclr_up, key_k, key_a,
           bonus_r_k, lnx_g, lnx_b):
    bsz, seq, d = x.shape
    depth = w_in.shape[0]
    assert seq % (GROUP * TILE) == 0 and d == w_out.shape[-1]
    topk = min(TOPK_MAX, seq // 4)
    tm = GROUP * TILE
    conv_tt = 256

    mod = _modulation(c, w_ada, b_ada)
    shift = mod[:, :, None, 0:d]
    sc1p = 1.0 + mod[:, :, None, d:2 * d]
    gate_mod = mod[:, :, None, 2 * d:3 * d]

    w_all = _reorder_w_in(w_in)
    head = jnp.arange(D_A) // HEAD_DIM
    avg = (head[:, None] == head[None, :]).astype(BF16) * (1.0 / HEAD_DIM)
    qg = jnp.tile(q_norm_g, (1, H_A))[:, None, :] * (HEAD_DIM ** -0.5)
    kg = jnp.tile(k_norm_g, (1, H_A))[:, None, :]
    bias_tiles = _bias_tiles(rel_bias)
    c_shift = 3 * D_C + 2 * D_LORA
    mu_pad = jnp.pad(shift_mu, ((0, 0), (0, RW_COLS - c_shift)))[:, None, :]
    zl = jnp.zeros((depth, D_LORA, D_C), F32)
    lora = jnp.concatenate([jnp.concatenate([decay_up, zl], axis=2),
                            jnp.concatenate([zl, iclr_up], axis=2)], axis=1)
    row = lambda t: t[:, None, :]
    layers = dict(
        shift=shift, sc1p=sc1p, gate_mod=gate_mod, norm_g=row(norm_g), w=w_all, qg=qg, kg=kg,
        w_out=w_out.astype(BF16), conv_w=conv_w, conv_b=row(conv_b), conv_g=row(conv_ln_g),
        conv_beta=row(conv_ln_b), mu=mu_pad, lora=lora, w0=row(decay_w0), a0=row(iclr_a0),
        kk=row(key_k), ka=row(key_a), rk=row(bonus_r_k.reshape(depth, D_C)), lg=row(lnx_g),
        lb=row(lnx_b))

    def layer(x2, p):
        q, k, vt, qi, ki, wit, gates, conv_in, rw_in = _inproj(
            x2, p["sc1p"], p["shift"], p["norm_g"], p["w"], avg, p["qg"], p["kg"], seq, tm)
        r3 = lambda t: t.reshape(bsz, seq, t.shape[-1])
        attn = _attention(r3(q), r3(k), vt, r3(qi), r3(ki), wit, bias_tiles, topk)
        conv = _conv(r3(conv_in), p["conv_w"], p["conv_b"], p["conv_g"], p["conv_beta"], conv_tt)
        rwo = _rwkv(r3(rw_in), p["mu"], p["lora"], p["w0"], p["a0"], p["kk"], p["ka"], p["rk"],
                    p["lg"], p["lb"])
        x2 = _outproj(x2, attn.reshape(-1, D_A), conv.reshape(-1, D_B), rwo.reshape(-1, D_C),
                      gates, p["gate_mod"], p["w_out"], seq, tm)
        return x2

    x2 = x.reshape(bsz * seq, d)
    for l in range(depth):
        x2 = layer(x2, {name: t[l] for name, t in layers.items()})
    return x2.reshape(bsz, seq, d)
```

```python
import functools
import math

import jax
import jax.numpy as jnp
import numpy as np
from jax import lax
from jax.experimental import pallas as pl
from jax.experimental.pallas import tpu as pltpu

F32 = jnp.float32
BF16 = jnp.bfloat16

HEAD_DIM = 64
CHUNK = 64
EPS = 1e-6
H_A = 6
D_A = H_A * HEAD_DIM
N_IDX_HEADS = 8
D_IDX = 64
TOPK_MAX = 256
N_BUCKETS = 32
MAX_DISTANCE = 1024
D_B = 256
CONV_WIDTH = 31
H_C = 6
D_C = H_C * HEAD_DIM
D_LORA = 32
LNX_EPS = 64e-5
D_MIX = D_A + D_B + D_C

LANES = 128
TILE = 128
GROUP = 4
BRACKET = 16.0
NARROW_STEPS = 26
LATE_STEPS = 10
ONES_ROWS = 16
RW_CHUNKS = 4
N_FAR = 9
VMEM_LIMIT = 48 * 1024 * 1024

C_Q, C_K, C_V = 0, 384, 768
C_QI = 1152
C_KI = 1664
C_WI = 1792
C_GATES = 1920
C_CONV = 2944
C_RW = 3456
N_COLS = 4736
RW_COLS = 1280

INT_MIN = -(2 ** 31)
INT_MAX = 2 ** 31 - 1
NEG_KEY = int(np.array(-np.inf, np.float32).view(np.int32)) ^ 0x7FFFFFFF
NEG_BIG = -1e30

NT_DIMS = (((1,), (1,)), ((), ()))
TN_DIMS = (((0,), (0,)), ((), ()))


def _bdot(a, b):
    return jnp.dot(a.astype(BF16), b.astype(BF16), preferred_element_type=F32)


def _bdot_nt(a, b):
    return lax.dot_general(a.astype(BF16), b.astype(BF16), NT_DIMS, preferred_element_type=F32)


def _sigmoid(x):
    return 1.0 / (1.0 + jnp.exp(-x))


def _silu(x):
    return x * _sigmoid(x)


def _mod_kernel(c_ref, w_ref, b_ref, o_ref):
    ca = _silu(c_ref[...])
    o_ref[0] = jnp.dot(ca, w_ref[0], precision=lax.Precision.HIGHEST,
                       preferred_element_type=F32) + b_ref[0]


def _modulation(c, w_ada, b_ada):
    depth, d, d3 = w_ada.shape
    b = c.shape[0]
    return pl.pallas_call(
        _mod_kernel,
        grid=(depth, d3 // d),
        in_specs=[pl.BlockSpec((b, d), lambda l, j: (0, 0)),
                  pl.BlockSpec((1, d, d), lambda l, j: (l, 0, j)),
                  pl.BlockSpec((1, 1, d), lambda l, j: (l, 0, j))],
        out_specs=pl.BlockSpec((1, b, d), lambda l, j: (l, 0, j)),
        out_shape=jax.ShapeDtypeStruct((depth, b, d3), F32),
        compiler_params=pltpu.CompilerParams(
            dimension_semantics=("arbitrary", "arbitrary"), vmem_limit_bytes=VMEM_LIMIT),
    )(c, w_ada, b_ada.reshape(depth, 1, d3))


def _inproj_kernel(x_ref, sc_ref, sh_ref, g_ref, w_ref, avg_ref, qg_ref, kg_ref,
                   q_ref, k_ref, vt_ref, qi_ref, ki_ref, wit_ref, gates_ref, conv_ref, rw_ref):
    x = x_ref[...]
    ms = jnp.mean(x * x, axis=-1, keepdims=True)
    h = x * lax.rsqrt(ms + EPS) * g_ref[...]
    hb = (h * sc_ref[0] + sh_ref[0]).astype(BF16)

    def mm(c0, c1):
        return jnp.dot(hb, w_ref[:, c0:c1], preferred_element_type=F32)

    def head_rms(t, g):
        m2 = jnp.dot((t * t).astype(BF16), avg_ref[...], preferred_element_type=F32)
        return t * lax.rsqrt(m2 + EPS) * g

    q_ref[...] = head_rms(mm(C_Q, C_Q + D_A), qg_ref[...]).astype(BF16)
    k_ref[...] = head_rms(mm(C_K, C_K + D_A), kg_ref[...]).astype(BF16)
    vt_ref[0] = mm(C_V, C_V + D_A).T.astype(BF16)
    qi_ref[...] = mm(C_QI, C_QI + 512).astype(BF16)
    ki_ref[...] = mm(C_KI, C_KI + LANES).astype(BF16)
    wit_ref[...] = mm(C_WI, C_WI + LANES).T[:N_IDX_HEADS, :]
    gates_ref[:, 0:512] = mm(C_GATES, C_GATES + 512)
    gates_ref[:, 512:1024] = mm(C_GATES + 512, C_GATES + 1024)
    conv_ref[...] = mm(C_CONV, C_CONV + 512)
    rw_ref[:, 0:512] = mm(C_RW, C_RW + 512)
    rw_ref[:, 512:1024] = mm(C_RW + 512, C_RW + 1024)
    rw_ref[:, 1024:RW_COLS] = mm(C_RW + 1024, C_RW + RW_COLS)


def _inproj(x2, sc1p, shift, norm_g, w, avg, qg, kg, seq, tm):
    n, d = x2.shape
    per_b = seq // tm
    row = lambda i: (i, 0)
    full = lambda i: (0, 0)
    bat = lambda i: (i // per_b, 0, 0)
    rows = lambda wd, dt: (pl.BlockSpec((tm, wd), row), jax.ShapeDtypeStruct((n, wd), dt))
    outs = [rows(D_A, BF16), rows(D_A, BF16),
            (pl.BlockSpec((1, D_A, tm), lambda i: (i, 0, 0)),
             jax.ShapeDtypeStruct((n // tm, D_A, tm), BF16)),
            rows(512, BF16), rows(LANES, BF16),
            (pl.BlockSpec((N_IDX_HEADS, tm), lambda i: (0, i)),
             jax.ShapeDtypeStruct((N_IDX_HEADS, n), F32)),
            rows(D_MIX, F32), rows(512, F32), rows(RW_COLS, F32)]
    return pl.pallas_call(
        _inproj_kernel,
        grid=(n // tm,),
        in_specs=[pl.BlockSpec((tm, d), row),
                  pl.BlockSpec((1, 1, d), bat),
                  pl.BlockSpec((1, 1, d), bat),
                  pl.BlockSpec((1, d), full),
                  pl.BlockSpec((d, N_COLS), full),
                  pl.BlockSpec((D_A, D_A), full),
                  pl.BlockSpec((1, D_A), full),
                  pl.BlockSpec((1, D_A), full)],
        out_specs=[o[0] for o in outs],
        out_shape=[o[1] for o in outs],
        compiler_params=pltpu.CompilerParams(
            dimension_semantics=("arbitrary",), vmem_limit_bytes=VMEM_LIMIT),
    )(x2, sc1p, shift, norm_g, w, avg, qg, kg)


def _sortable(s):
    b = lax.bitcast_convert_type(s, jnp.int32)
    return b ^ ((b >> 31) & jnp.int32(0x7FFFFFFF))


def _fold(t, op):
    out = t[0:8, :]
    for i in range(1, TILE // 8):
        out = op(out, t[8 * i:8 * i + 8, :])
    return out


def _attn_kernel(q_ref, k_ref, vt_ref, qi_ref, ki_ref, wit_ref, bias_ref, o_ref,
                 key_scr, qih_scr, qh_scr, mb_scr, s_scr, m_scr, acc_scr,
                 *, topk, pos_bits):
    qb = pl.program_id(1)
    nst = qb // GROUP + 1
    kf = float(topk)
    zgap = 1 << pos_bits
    lane = lax.broadcasted_iota(jnp.int32, (TILE, LANES), 1)
    krow = lax.broadcasted_iota(jnp.int32, (TILE, LANES), 0)
    lo_half = lane < HEAD_DIM

    wit = wit_ref[...]
    qi = qi_ref[0]
    q = q_ref[0]
    for src, dst, n_pairs in ((qi, qih_scr, N_IDX_HEADS // 2), (q, qh_scr, H_A // 2)):
        for g in range(n_pairs):
            pair = src[:, g * LANES:(g + 1) * LANES]
            dst[g, 0:TILE, :] = jnp.where(lo_half, pair, jnp.zeros_like(pair))
            dst[g, TILE:2 * TILE, :] = jnp.where(lo_half, jnp.zeros_like(pair), pair)

    adm = (krow // CHUNK) <= (lane // CHUNK)
    inadm = NEG_KEY - zgap - 1

    def score_key(s):
        key = _sortable(s)
        return jnp.where(key < 0, key - zgap, key)

    def p1(st, c):
        base = st * GROUP
        kis = ki_ref[0, pl.ds(pl.multiple_of(base * TILE, GROUP * TILE), GROUP * TILE), :]
        acc = [None] * GROUP
        for g in range(N_IDX_HEADS // 2):
            d = lax.dot_general(kis, qih_scr[g], NT_DIMS, preferred_element_type=F32)
            for par in range(2):
                w_row = wit[2 * g + par:2 * g + par + 1, :]
                for u in range(GROUP):
                    t = jnp.maximum(d[u * TILE:(u + 1) * TILE, par * TILE:(par + 1) * TILE], 0.0) * w_row
                    acc[u] = t if acc[u] is None else acc[u] + t
        for u in range(GROUP):
            kt = base + u
            key = jnp.where(acc[u] == 0.0, -1 - (krow + kt * TILE), score_key(acc[u]))
            key_scr[kt] = jnp.where(kt < qb, key, jnp.where(kt == qb, jnp.where(adm, key, inadm), inadm))
            diag = jnp.where(adm, acc[u], -jnp.inf)
            sc = jnp.where(kt < qb, acc[u], jnp.where(kt == qb, diag, -jnp.inf))
            c = jnp.maximum(c, _fold(sc, jnp.maximum))
        return c

    smax8 = lax.fori_loop(0, nst, p1, jnp.full((8, LANES), -jnp.inf, F32))
    fmax = jnp.max(smax8, axis=0, keepdims=True)

    def count_ge(thr_row):
        def body(st, c):
            for u in range(GROUP):
                c = c + _fold(jnp.where(key_scr[st * GROUP + u] >= thr_row, 1.0, 0.0), jnp.add)
            return c
        c = lax.fori_loop(0, nst, body, jnp.zeros((8, LANES), F32))
        return jnp.sum(c, axis=0, keepdims=True)

    def bisect(lo, hi):
        mid = (lo >> 1) + (hi >> 1) + (lo & hi & 1)
        cnt = count_ge(mid)
        ge = cnt >= kf
        hi = jnp.where(cnt == kf, mid + 1, jnp.where(ge, hi, mid))
        return jnp.where(ge, mid, lo), hi

    kmax = jnp.where(fmax == 0.0, 0, score_key(fmax))
    lo_try = score_key(jnp.where(fmax > 0.0, fmax * (1.0 / BRACKET), fmax * BRACKET))
    narrow = count_ge(lo_try) >= kf
    lo0 = jnp.where(narrow, lo_try, INT_MIN)
    hi0 = jnp.where(narrow, jnp.minimum(kmax, INT_MAX - 1) + 1, INT_MAX)
    all_narrow = jnp.min(jnp.where(narrow, 1.0, 0.0)) > 0.0
    n_steps = jnp.where(all_narrow, NARROW_STEPS, 32)
    lo, hi = lax.fori_loop(0, n_steps - LATE_STEPS, lambda _, lh: bisect(*lh), (lo0, hi0))

    def unsettled(lo, hi):
        return jnp.max(jnp.where(hi != lo + 1, 1.0, 0.0)) > 0.0

    def late(carry):
        lo, hi, it, _ = carry
        lo, hi = bisect(*bisect(lo, hi))
        return lo, hi, it + 2, unsettled(lo, hi)

    thr, _, _, _ = lax.while_loop(lambda carry: jnp.logical_and(carry[2] < n_steps, carry[3]),
                                  late, (lo, hi, n_steps - LATE_STEPS, unsettled(lo, hi)))
    cnt_ge = count_ge(thr)
    tie = jnp.logical_and(cnt_ge > kf, thr > inadm)

    @pl.when(jnp.max(jnp.where(tie, 1.0, 0.0)) > 0.0)
    def _():
        need = kf - count_ge(thr + 1)

        def count_eq_le(pmax):
            def body(kt, c):
                hit = jnp.where(key_scr[kt] == thr, krow + kt * TILE, INT_MAX) <= pmax
                return c + _fold(jnp.where(hit, 1.0, 0.0), jnp.add)
            c = lax.fori_loop(0, nst * GROUP, body, jnp.zeros((8, LANES), F32))
            return jnp.sum(c, axis=0, keepdims=True)

        def pbisect(_, lohi):
            plo, phi = lohi
            pmid = (plo + phi) >> 1
            ok = count_eq_le(pmid) >= need
            return jnp.where(ok, plo, pmid), jnp.where(ok, pmid, phi)

        plo0 = jnp.full((1, LANES), -1, jnp.int32)
        phi0 = jnp.full((1, LANES), (1 << pos_bits) - 1, jnp.int32)
        _, pthr = lax.fori_loop(0, pos_bits + 1, pbisect, (plo0, phi0))
        pthr = jnp.where(tie, pthr, INT_MAX)

        def drop(kt, c):
            kk = key_scr[kt]
            pos = jnp.where(kk == thr, krow + kt * TILE, -1)
            key_scr[kt] = jnp.where(pos > pthr, inadm, kk)
            return c

        lax.fori_loop(0, nst * GROUP, drop, 0)

    thr_eff = jnp.maximum(thr, inadm + 1)

    for h in range(H_A):
        m_scr[h] = jnp.full((8, LANES), NEG_BIG, F32)
        acc_scr[h] = jnp.zeros((HEAD_DIM + ONES_ROWS, LANES), F32)

    def p3a(st, c):
        base = st * GROUP
        k_s = k_ref[0, pl.ds(pl.multiple_of(base * TILE, GROUP * TILE), GROUP * TILE), :]
        for u in range(GROUP):
            mb_scr[u] = jnp.where(key_scr[base + u] >= thr_eff, 0.0, NEG_BIG)
        for g in range(H_A // 2):
            s = lax.dot_general(k_s[:, g * LANES:(g + 1) * LANES], qh_scr[g], NT_DIMS,
                                preferred_element_type=F32)
            for par in range(2):
                h = 2 * g + par
                mx = m_scr[h]
                for u in range(GROUP):
                    dd = jnp.clip(base + u - qb, -N_FAR, 0) + N_FAR
                    su = (s[u * TILE:(u + 1) * TILE, par * TILE:(par + 1) * TILE]
                          + bias_ref[h * (N_FAR + 1) + dd] + mb_scr[u])
                    s_scr[h, base + u] = su
                    mx = jnp.maximum(mx, _fold(su, jnp.maximum))
                m_scr[h] = mx
        return c

    lax.fori_loop(0, nst, p3a, 0)
    m_row = [jnp.max(m_scr[h], axis=0, keepdims=True) for h in range(H_A)]

    ones_rows = jnp.ones((ONES_ROWS, GROUP * TILE), BF16)

    def p3b(st, c):
        base = st * GROUP
        for h in range(H_A):
            ps = [jnp.exp((s_scr[h, base + u] - m_row[h]).astype(BF16)) for u in range(GROUP)]
            lhs = jnp.concatenate([vt_ref[st, h * HEAD_DIM:(h + 1) * HEAD_DIM, :], ones_rows], axis=0)
            acc_scr[h] += jnp.dot(lhs, jnp.concatenate(ps, axis=0), preferred_element_type=F32)
        return c

    lax.fori_loop(0, nst, p3b, 0)

    outs = []
    for h in range(H_A):
        acc = acc_scr[h]
        outs.append(acc[0:HEAD_DIM, :] / acc[HEAD_DIM:HEAD_DIM + 1, :])
    o_ref[0] = jnp.concatenate(outs, axis=0).T


def _attention(q, k, vt, qi, ki, wit, bias_tiles, topk):
    b, s, _ = q.shape
    nt = s // TILE
    ng = s // (GROUP * TILE)
    kern = functools.partial(_attn_kernel, topk=topk, pos_bits=max(1, int(math.ceil(math.log2(s)))))
    qtile = lambda bb, i: (bb, i, 0)
    whole = lambda bb, i: (bb, 0, 0)
    once = pl.Buffered(1)
    return pl.pallas_call(
        kern,
        grid=(b, nt),
        in_specs=[pl.BlockSpec((1, TILE, D_A), qtile),
                  pl.BlockSpec((1, s, D_A), whole, pipeline_mode=once),
                  pl.BlockSpec((ng, D_A, GROUP * TILE), whole, pipeline_mode=once),
                  pl.BlockSpec((1, TILE, 512), qtile),
                  pl.BlockSpec((1, s, LANES), whole, pipeline_mode=once),
                  pl.BlockSpec((N_IDX_HEADS, TILE), lambda bb, i: (0, bb * nt + i)),
                  pl.BlockSpec(bias_tiles.shape, lambda bb, i: (0, 0, 0), pipeline_mode=once)],
        out_specs=pl.BlockSpec((1, TILE, D_A), qtile),
        out_shape=jax.ShapeDtypeStruct((b, s, D_A), F32),
        scratch_shapes=[pltpu.VMEM((nt, TILE, LANES), jnp.int32),
                        pltpu.VMEM((N_IDX_HEADS // 2, 2 * TILE, LANES), BF16),
                        pltpu.VMEM((H_A // 2, 2 * TILE, LANES), BF16),
                        pltpu.VMEM((GROUP, TILE, LANES), F32),
                        pltpu.VMEM((H_A, nt, TILE, LANES), F32),
                        pltpu.VMEM((H_A, 8, LANES), F32),
                        pltpu.VMEM((H_A, HEAD_DIM + ONES_ROWS, LANES), F32)],
        compiler_params=pltpu.CompilerParams(
            dimension_semantics=("arbitrary", "arbitrary"), vmem_limit_bytes=VMEM_LIMIT),
    )(q, k, vt, qi, ki, wit, bias_tiles)


def _t5_bucket(rel):
    nb = N_BUCKETS // 2
    max_exact = nb // 2
    ret = jnp.where(rel > 0, nb, 0)
    n = jnp.abs(rel)
    nf = jnp.maximum(n, 1).astype(F32)
    large = max_exact + (jnp.log(nf / max_exact) / math.log(MAX_DISTANCE / max_exact)
                         * (nb - max_exact)).astype(jnp.int32)
    large = jnp.minimum(large, nb - 1)
    return ret + jnp.where(n < max_exact, n, large)


def _bias_tiles(rel_bias):
    j = jnp.arange(TILE, dtype=jnp.int32)[:, None]
    t = jnp.arange(TILE, dtype=jnp.int32)[None, :]
    tiles = [rel_bias[_t5_bucket((dd - N_FAR) * TILE + j - t)] for dd in range(N_FAR + 1)]
    arr = jnp.stack(tiles).astype(F32)
    return arr.transpose(3, 0, 1, 2).reshape(H_A * (N_FAR + 1), TILE, TILE)


CONV_HALO = 32


def _conv_kernel(cur_ref, halo_ref, w_ref, b_ref, g_ref, beta_ref, o_ref, u_scr, *, tt):
    i = pl.program_id(1)

    def glu(t):
        return t[:, :D_B] * _sigmoid(t[:, D_B:])

    u_scr[0:CONV_HALO, :] = jnp.where(i > 0, glu(halo_ref[0]), 0.0)
    u_scr[CONV_HALO:CONV_HALO + tt, :] = glu(cur_ref[0])
    rows = 64
    first = CONV_HALO - (CONV_WIDTH - 1)
    for r0 in range(0, tt, rows):
        acc = jnp.broadcast_to(b_ref[...], (rows, D_B))
        for j in range(CONV_WIDTH):
            acc = acc + w_ref[j:j + 1, :] * u_scr[r0 + first + j:r0 + first + j + rows, :]
        mu = jnp.mean(acc, axis=-1, keepdims=True)
        cen = acc - mu
        var = jnp.mean(cen * cen, axis=-1, keepdims=True)
        y = cen * lax.rsqrt(var + EPS) * g_ref[...] + beta_ref[...]
        o_ref[0, r0:r0 + rows, :] = _silu(y)


def _conv(conv_in, w, b, g, beta, tt):
    bsz, s, _ = conv_in.shape
    per = tt // CONV_HALO
    vec = lambda bb, i: (0, 0)
    return pl.pallas_call(
        functools.partial(_conv_kernel, tt=tt),
        grid=(bsz, s // tt),
        in_specs=[pl.BlockSpec((1, tt, 2 * D_B), lambda bb, i: (bb, i, 0)),
                  pl.BlockSpec((1, CONV_HALO, 2 * D_B),
                               lambda bb, i: (bb, jnp.maximum(i * per - 1, 0), 0)),
                  pl.BlockSpec((CONV_WIDTH, D_B), vec),
                  pl.BlockSpec((1, D_B), vec),
                  pl.BlockSpec((1, D_B), vec),
                  pl.BlockSpec((1, D_B), vec)],
        out_specs=pl.BlockSpec((1, tt, D_B), lambda bb, i: (bb, i, 0)),
        out_shape=jax.ShapeDtypeStruct((bsz, s, D_B), F32),
        scratch_shapes=[pltpu.VMEM((CONV_HALO + tt, D_B), F32)],
        compiler_params=pltpu.CompilerParams(
            dimension_semantics=("arbitrary", "arbitrary"), vmem_limit_bytes=VMEM_LIMIT),
    )(conv_in, conv_in, w, b, g, beta)


def _rwkv_kernel(rw_ref, mu_ref, lora_ref, w0_ref, a0_ref, kk_ref, ka_ref, rk_ref, lg_ref, lb_ref,
                 o_ref, s_scr, prev_scr):
    c = pl.program_id(1)
    n = CHUNK
    nb = RW_CHUNKS * CHUNK

    @pl.when(c == 0)
    def _():
        s_scr[...] = jnp.zeros_like(s_scr)
        prev_scr[...] = jnp.zeros_like(prev_scr)

    ps = rw_ref[0]
    row = lax.broadcasted_iota(jnp.int32, (nb, 1), 0)
    prev = jnp.where(row == 0, prev_scr[0:1, :], pltpu.roll(ps, 1, axis=0))
    prev_scr[0:1, :] = ps[nb - 1:nb, :]
    xs = ps + mu_ref[...] * (prev - ps)
    r = xs[:, 0:D_C]
    k = xs[:, D_C:2 * D_C]
    v = xs[:, 2 * D_C:3 * D_C]
    dn = xs[:, 3 * D_C:3 * D_C + 2 * D_LORA]
    lane64 = lax.broadcasted_iota(jnp.int32, (nb, 2 * D_LORA), 1)
    dn = jnp.where(lane64 < D_LORA, jnp.tanh(dn), dn)
    pre = jnp.dot(dn, lora_ref[...], precision=lax.Precision.HIGHEST, preferred_element_type=F32)
    z = -(w0_ref[...] + pre[:, :D_C])
    softplus = jnp.maximum(z, 0.0) + jnp.log(1.0 + jnp.exp(-jnp.abs(z)))
    logdec = -jnp.exp(-softplus - 0.5)
    a = _sigmoid(a0_ref[...] + pre[:, D_C:])
    kkraw = k * kk_ref[...]
    k2 = k * (1.0 + (a - 1.0) * ka_ref[...])
    bonus_pre = r * k2 * rk_ref[...]

    rb = lax.broadcasted_iota(jnp.int32, (nb, nb), 0)
    cb = lax.broadcasted_iota(jnp.int32, (nb, nb), 1)
    tri = jnp.where(jnp.logical_and(rb // n == cb // n, cb <= rb), 1.0, 0.0)
    cum_all = jnp.dot(tri, logdec, precision=lax.Precision.HIGHEST, preferred_element_type=F32)
    p_in_all = jnp.exp(cum_all)
    p_inv_all = jnp.exp(-cum_all)
    p_ex_all = jnp.exp(cum_all - logdec)

    row_n = lax.broadcasted_iota(jnp.int32, (n, LANES), 0)
    lane_n = lax.broadcasted_iota(jnp.int32, (n, LANES), 1)
    lo = lane_n < HEAD_DIM
    col = lane_n & (HEAD_DIM - 1)
    incl = col <= row_n
    strict = col < row_n
    mask2 = jnp.concatenate([strict, incl], axis=0)
    r2 = lax.broadcasted_iota(jnp.int32, (2 * n, LANES), 0)
    l2 = lax.broadcasted_iota(jnp.int32, (2 * n, LANES), 1)
    same_head = (r2 // HEAD_DIM) == (l2 // HEAD_DIM)
    ones_bd = jnp.where(same_head, 1.0, 0.0).astype(BF16)

    def bd(t):
        tb = t.astype(BF16)
        z = jnp.zeros_like(tb)
        return jnp.concatenate([jnp.where(lo, tb, z), jnp.where(lo, z, tb)], axis=0)

    def hsum(t):
        return jnp.dot(t.astype(BF16), ones_bd, preferred_element_type=F32)

    pairs = range(H_C // 2)
    chains = [(j, g) for j in range(RW_CHUNKS) for g in pairs]
    ids = range(len(chains))
    sl = lambda t, i: t[chains[i][0] * n:(chains[i][0] + 1) * n,
                        chains[i][1] * LANES:(chains[i][1] + 1) * LANES]
    v_p = [sl(v, i) for i in ids]
    p_in = [sl(p_in_all, i) for i in ids]
    kkn = []
    for i in ids:
        kr = sl(kkraw, i)
        kkn.append(kr * lax.rsqrt(jnp.maximum(hsum(kr * kr), 1e-12)))
    rt = [sl(r, i) * p_in[i] for i in ids]
    at = [kkn[i] * sl(p_ex_all, i) for i in ids]
    kt = [sl(k2, i) * sl(p_inv_all, i) for i in ids]
    bt = [kkn[i] * sl(a, i) * sl(p_inv_all, i) for i in ids]
    lhs = [jnp.concatenate([at[i], rt[i]], axis=0).astype(BF16) for i in ids]
    ak = [lax.dot_general(lhs[i], bd(kt[i]), NT_DIMS, preferred_element_type=F32) for i in ids]
    ab = [lax.dot_general(lhs[i], bd(bt[i]), NT_DIMS, preferred_element_type=F32) for i in ids]
    av = [_bdot(jnp.where(mask2, ak[i], 0.0), bd(v_p[i])) for i in ids]
    a_rb = [jnp.where(incl, ab[i][n:], 0.0) for i in ids]
    npow = [jnp.where(strict, -ab[i][:n], 0.0) for i in ids]
    tm = list(npow)
    for _ in range(5):
        both = [_bdot(jnp.concatenate([npow[i], tm[i]], axis=0), bd(npow[i])) for i in ids]
        npow = [both[i][:n] for i in ids]
        tm = [tm[i] + both[i][n:] for i in ids]
    tm = [tm[i] + _bdot(tm[i], bd(npow[i])) for i in ids]
    wt = [at[i] + _bdot(tm[i], bd(at[i])) for i in ids]
    uloc = [av[i][:n] + _bdot(tm[i], bd(av[i][:n])) for i in ids]
    mcat = [jnp.concatenate([kt[i], bt[i]], axis=0).astype(BF16) for i in ids]
    bonus = [hsum(sl(bonus_pre, i)) * v_p[i] for i in ids]

    state = [s_scr[g] for g in pairs]
    for j in range(RW_CHUNKS):
        cid = [j * (H_C // 2) + g for g in pairs]
        ws = [lax.dot_general(jnp.concatenate([wt[i], rt[i]], axis=0).astype(BF16),
                              state[g].astype(BF16), NT_DIMS, preferred_element_type=F32)
              for g, i in zip(pairs, cid)]
        u = [ws[g][:n] + uloc[i] for g, i in zip(pairs, cid)]
        o = [ws[g][n:] + av[i][n:] - _bdot(a_rb[i], bd(u[g])) for g, i in zip(pairs, cid)]
        for g, i in zip(pairs, cid):
            ncat = jnp.concatenate([v_p[i], -u[g]], axis=0).astype(BF16)
            upd = lax.dot_general(ncat, mcat[i], TN_DIMS, preferred_element_type=F32)
            state[g] = (state[g] + jnp.where(same_head, upd, 0.0)) * p_in[i][n - 1:n, :]
        for g, i in zip(pairs, cid):
            lanes = slice(g * LANES, (g + 1) * LANES)
            cen = o[g] - hsum(o[g]) * (1.0 / HEAD_DIM)
            var = hsum(cen * cen) * (1.0 / HEAD_DIM)
            y = cen * lax.rsqrt(var + LNX_EPS) * lg_ref[:, lanes] + lb_ref[:, lanes]
            o_ref[0, j * n:(j + 1) * n, lanes] = y + bonus[i]
    for g in pairs:
        s_scr[g] = state[g]


def _rwkv(rw, mu, lora, w0, a0, kk, ka, rk, lg, lb):
    bsz, s, _ = rw.shape
    vec = lambda bb, i: (0, 0)
    return pl.pallas_call(
        _rwkv_kernel,
        grid=(bsz, s // (RW_CHUNKS * CHUNK)),
        in_specs=[pl.BlockSpec((1, RW_CHUNKS * CHUNK, RW_COLS), lambda bb, i: (bb, i, 0)),
                  pl.BlockSpec((1, RW_COLS), vec),
                  pl.BlockSpec((2 * D_LORA, 2 * D_C), vec)] +
                 [pl.BlockSpec((1, D_C), vec)] * 7,
        out_specs=pl.BlockSpec((1, RW_CHUNKS * CHUNK, D_C), lambda bb, i: (bb, i, 0)),
        out_shape=jax.ShapeDtypeStruct((bsz, s, D_C), F32),
        scratch_shapes=[pltpu.VMEM((H_C // 2, 2 * HEAD_DIM, LANES), F32),
                        pltpu.VMEM((8, RW_COLS), F32)],
        compiler_params=pltpu.CompilerParams(
            dimension_semantics=("arbitrary", "arbitrary"), vmem_limit_bytes=VMEM_LIMIT),
    )(rw, mu, lora, w0, a0, kk, ka, rk, lg, lb)


def _outproj_kernel(x_ref, attn_ref, conv_ref, rw_ref, gates_ref, gm_ref, w_ref, o_ref):
    sg = _silu(gates_ref[...])
    ya = (attn_ref[...] * sg[:, 0:D_A]).astype(BF16)
    yb = (conv_ref[...] * sg[:, D_A:D_A + D_B]).astype(BF16)
    yc = (rw_ref[...] * sg[:, D_A + D_B:D_MIX]).astype(BF16)
    y = (jnp.dot(ya, w_ref[0:D_A, :], preferred_element_type=F32)
         + jnp.dot(yb, w_ref[D_A:D_A + D_B, :], preferred_element_type=F32)
         + jnp.dot(yc, w_ref[D_A + D_B:D_MIX, :], preferred_element_type=F32))
    o_ref[...] = x_ref[...] + gm_ref[0] * y


def _outproj(x2, attn, conv, rw, gates, gate_mod, w, seq, tm):
    n, d = x2.shape
    per_b = seq // tm
    row = lambda i: (i, 0)
    return pl.pallas_call(
        _outproj_kernel,
        grid=(n // tm,),
        in_specs=[pl.BlockSpec((tm, d), row),
                  pl.BlockSpec((tm, D_A), row),
                  pl.BlockSpec((tm, D_B), row),
                  pl.BlockSpec((tm, D_C), row),
                  pl.BlockSpec((tm, D_MIX), row),
                  pl.BlockSpec((1, 1, d), lambda i: (i // per_b, 0, 0)),
                  pl.BlockSpec((D_MIX, d), lambda i: (0, 0))],
        out_specs=pl.BlockSpec((tm, d), row),
        out_shape=jax.ShapeDtypeStruct((n, d), F32),
        compiler_params=pltpu.CompilerParams(
            dimension_semantics=("arbitrary",), vmem_limit_bytes=VMEM_LIMIT),
    )(x2, attn, conv, rw, gates, gate_mod, w)


def _reorder_w_in(w_in):
    depth, d, _ = w_in.shape
    a_cols = 4 * D_A + N_IDX_HEADS * D_IDX + D_IDX + N_IDX_HEADS
    b0 = a_cols
    c0 = a_cols + 3 * D_B
    c_shift = 3 * D_C + 2 * D_LORA
    zeros = lambda m: jnp.zeros((depth, d, m), w_in.dtype)
    qi0 = 4 * D_A
    ki0 = qi0 + N_IDX_HEADS * D_IDX
    wi0 = ki0 + D_IDX
    parts = [w_in[..., 0:3 * D_A],
             w_in[..., qi0:ki0],
             w_in[..., ki0:wi0], w_in[..., ki0:wi0],
             w_in[..., wi0:a_cols], zeros(LANES - N_IDX_HEADS),
             w_in[..., 3 * D_A:4 * D_A], w_in[..., b0 + 2 * D_B:b0 + 3 * D_B],
             w_in[..., c0 + c_shift:c0 + c_shift + D_C],
             w_in[..., b0:b0 + 2 * D_B],
             w_in[..., c0:c0 + c_shift], zeros(RW_COLS - c_shift)]
    w = jnp.concatenate(parts, axis=-1)
    assert w.shape[-1] == N_COLS
    return w.astype(BF16)


def kernel(x, c, norm_g, w_ada, b_ada, w_in, w_out, q_norm_g, k_norm_g, rel_bias, conv_w, conv_b,
           conv_ln_g, conv_ln_b, shift_mu, decay_w0, decay_up, iclr_a0, iclr_up, key_k, key_a,
           bonus_r_k, lnx_g, lnx_b):
    bsz, seq, d = x.shape
    depth = w_in.shape[0]
    assert seq % (GROUP * TILE) == 0 and d == w_out.shape[-1]
    topk = min(TOPK_MAX, seq // 4)
    tm = GROUP * TILE
    conv_tt = 256

    mod = _modulation(c, w_ada, b_ada)
    shift = mod[:, :, None, 0:d]
    sc1p = 1.0 + mod[:, :, None, d:2 * d]
    gate_mod = mod[:, :, None, 2 * d:3 * d]

    w_all = _reorder_w_in(w_in)
    head = jnp.arange(D_A) // HEAD_DIM
    avg = (head[:, None] == head[None, :]).astype(BF16) * (1.0 / HEAD_DIM)
    qg = jnp.tile(q_norm_g, (1, H_A))[:, None, :] * (HEAD_DIM ** -0.5)
    kg = jnp.tile(k_norm_g, (1, H_A))[:, None, :]
    bias_tiles = _bias_tiles(rel_bias)
    c_shift = 3 * D_C + 2 * D_LORA
    mu_pad = jnp.pad(shift_mu, ((0, 0), (0, RW_COLS - c_shift)))[:, None, :]
    zl = jnp.zeros((depth, D_LORA, D_C), F32)
    lora = jnp.concatenate([jnp.concatenate([decay_up, zl], axis=2),
                            jnp.concatenate([zl, iclr_up], axis=2)], axis=1)
    row = lambda t: t[:, None, :]
    layers = dict(
        shift=shift, sc1p=sc1p, gate_mod=gate_mod, norm_g=row(norm_g), w=w_all, qg=qg, kg=kg,
        w_out=w_out.astype(BF16), conv_w=conv_w, conv_b=row(conv_b), conv_g=row(conv_ln_g),
        conv_beta=row(conv_ln_b), mu=mu_pad, lora=lora, w0=row(decay_w0), a0=row(iclr_a0),
        kk=row(key_k), ka=row(key_a), rk=row(bonus_r_k.reshape(depth, D_C)), lg=row(lnx_g),
        lb=row(lnx_b))

    def layer(x2, p):
        q, k, vt, qi, ki, wit, gates, conv_in, rw_in = _inproj(
            x2, p["sc1p"], p["shift"], p["norm_g"], p["w"], avg, p["qg"], p["kg"], seq, tm)
        r3 = lambda t: t.reshape(bsz, seq, t.shape[-1])
        attn = _attention(r3(q), r3(k), vt, r3(qi), r3(ki), wit, bias_tiles, topk)
        conv = _conv(r3(conv_in), p["conv_w"], p["conv_b"], p["conv_g"], p["conv_beta"], conv_tt)
        rwo = _rwkv(r3(rw_in), p["mu"], p["lora"], p["w0"], p["a0"], p["kk"], p["ka"], p["rk"],
                    p["lg"], p["lb"])
        x2 = _outproj(x2, attn.reshape(-1, D_A), conv.reshape(-1, D_B), rwo.reshape(-1, D_C),
                      gates, p["gate_mod"], p["w_out"], seq, tm)
        return x2

    x2 = x.reshape(bsz * seq, d)
    for l in range(depth):
        x2 = layer(x2, {name: t[l] for name, t in layers.items()})
    return x2.reshape(bsz, seq, d)
```

```python
import functools
import math

import jax
import jax.numpy as jnp
import numpy as np
from jax import lax
from jax.experimental import pallas as pl
from jax.experimental.pallas import tpu as pltpu

F32 = jnp.float32
BF16 = jnp.bfloat16

HEAD_DIM = 64
CHUNK = 64
EPS = 1e-6
H_A = 6
D_A = H_A * HEAD_DIM
N_IDX_HEADS = 8
D_IDX = 64
TOPK_MAX = 256
N_BUCKETS = 32
MAX_DISTANCE = 1024
D_B = 256
CONV_WIDTH = 31
H_C = 6
D_C = H_C * HEAD_DIM
D_LORA = 32
LNX_EPS = 64e-5
D_MIX = D_A + D_B + D_C

LANES = 128
TILE = 128
GROUP = 4
BRACKET = 16.0
NARROW_STEPS = 26
DEN_FLOOR = 1e-30
LATE_STEPS = 14
ONES_ROWS = 16
RW_CHUNKS = 4
N_FAR = 9
VMEM_LIMIT = 48 * 1024 * 1024

C_Q, C_K, C_V = 0, 384, 768
C_QI = 1152
C_KI = 1664
C_WI = 1792
C_GATES = 1920
C_CONV = 2944
C_RW = 3456
N_COLS = 4736
RW_COLS = 1280

INT_MIN = -(2 ** 31)
INT_MAX = 2 ** 31 - 1
NEG_KEY = int(np.array(-np.inf, np.float32).view(np.int32)) ^ 0x7FFFFFFF
NEG_BIG = -1e30

NT_DIMS = (((1,), (1,)), ((), ()))
TN_DIMS = (((0,), (0,)), ((), ()))


def _bdot(a, b):
    return jnp.dot(a.astype(BF16), b.astype(BF16), preferred_element_type=F32)


def _bdot_nt(a, b):
    return lax.dot_general(a.astype(BF16), b.astype(BF16), NT_DIMS, preferred_element_type=F32)


def _split2(x):
    hi = x.astype(BF16).astype(F32)
    return hi, x - hi


def _sigmoid(x):
    return 1.0 / (1.0 + jnp.exp(-x))


def _silu(x):
    return x * _sigmoid(x)


def _mod_kernel(c_ref, w_ref, b_ref, o_ref):
    ca = _silu(c_ref[...])
    o_ref[0] = jnp.dot(ca, w_ref[0], precision=lax.Precision.HIGHEST,
                       preferred_element_type=F32) + b_ref[0]


def _modulation(c, w_ada, b_ada):
    depth, d, d3 = w_ada.shape
    b = c.shape[0]
    return pl.pallas_call(
        _mod_kernel,
        grid=(depth, d3 // d),
        in_specs=[pl.BlockSpec((b, d), lambda l, j: (0, 0)),
                  pl.BlockSpec((1, d, d), lambda l, j: (l, 0, j)),
                  pl.BlockSpec((1, 1, d), lambda l, j: (l, 0, j))],
        out_specs=pl.BlockSpec((1, b, d), lambda l, j: (l, 0, j)),
        out_shape=jax.ShapeDtypeStruct((depth, b, d3), F32),
        compiler_params=pltpu.CompilerParams(
            dimension_semantics=("arbitrary", "arbitrary"), vmem_limit_bytes=VMEM_LIMIT),
    )(c, w_ada, b_ada.reshape(depth, 1, d3))


def _inproj_kernel(x_ref, sc_ref, sh_ref, g_ref, w_ref, avg_ref, qg_ref, kg_ref,
                   q_ref, k_ref, vt_ref, qi_ref, ki_ref, wit_ref, gates_ref, conv_ref, rw_ref):
    x = x_ref[...]
    ms = jnp.mean(x * x, axis=-1, keepdims=True)
    h = x * lax.rsqrt(ms + EPS) * g_ref[...]
    hb = (h * sc_ref[0] + sh_ref[0]).astype(BF16)

    def mm(c0, c1):
        return jnp.dot(hb, w_ref[:, c0:c1], preferred_element_type=F32)

    def head_rms(t, g):
        m2 = jnp.dot((t * t).astype(BF16), avg_ref[...], preferred_element_type=F32)
        return t * lax.rsqrt(m2 + EPS) * g

    q_ref[...] = head_rms(mm(C_Q, C_Q + D_A), qg_ref[...]).astype(BF16)
    k_ref[...] = head_rms(mm(C_K, C_K + D_A), kg_ref[...]).astype(BF16)
    vt_ref[0] = mm(C_V, C_V + D_A).T.astype(BF16)
    qi_ref[...] = mm(C_QI, C_QI + 512).astype(BF16)
    ki_ref[...] = mm(C_KI, C_KI + LANES).astype(BF16)
    wit_ref[...] = mm(C_WI, C_WI + LANES).T[:N_IDX_HEADS, :]
    gates_ref[:, 0:512] = mm(C_GATES, C_GATES + 512)
    gates_ref[:, 512:1024] = mm(C_GATES + 512, C_GATES + 1024)
    conv_ref[...] = mm(C_CONV, C_CONV + 512)
    rw_ref[:, 0:512] = mm(C_RW, C_RW + 512)
    rw_ref[:, 512:1024] = mm(C_RW + 512, C_RW + 1024)
    rw_ref[:, 1024:RW_COLS] = mm(C_RW + 1024, C_RW + RW_COLS)


def _inproj(x2, sc1p, shift, norm_g, w, avg, qg, kg, seq, tm):
    n, d = x2.shape
    per_b = seq // tm
    row = lambda i: (i, 0)
    full = lambda i: (0, 0)
    bat = lambda i: (i // per_b, 0, 0)
    rows = lambda wd, dt: (pl.BlockSpec((tm, wd), row), jax.ShapeDtypeStruct((n, wd), dt))
    outs = [rows(D_A, BF16), rows(D_A, BF16),
            (pl.BlockSpec((1, D_A, tm), lambda i: (i, 0, 0)),
             jax.ShapeDtypeStruct((n // tm, D_A, tm), BF16)),
            rows(512, BF16), rows(LANES, BF16),
            (pl.BlockSpec((N_IDX_HEADS, tm), lambda i: (0, i)),
             jax.ShapeDtypeStruct((N_IDX_HEADS, n), F32)),
            rows(D_MIX, F32), rows(512, F32), rows(RW_COLS, F32)]
    return pl.pallas_call(
        _inproj_kernel,
        grid=(n // tm,),
        in_specs=[pl.BlockSpec((tm, d), row),
                  pl.BlockSpec((1, 1, d), bat),
                  pl.BlockSpec((1, 1, d), bat),
                  pl.BlockSpec((1, d), full),
                  pl.BlockSpec((d, N_COLS), full),
                  pl.BlockSpec((D_A, D_A), full),
                  pl.BlockSpec((1, D_A), full),
                  pl.BlockSpec((1, D_A), full)],
        out_specs=[o[0] for o in outs],
        out_shape=[o[1] for o in outs],
        compiler_params=pltpu.CompilerParams(
            dimension_semantics=("arbitrary",), vmem_limit_bytes=VMEM_LIMIT),
    )(x2, sc1p, shift, norm_g, w, avg, qg, kg)


def _sortable(s):
    b = lax.bitcast_convert_type(s, jnp.int32)
    return b ^ ((b >> 31) & jnp.int32(0x7FFFFFFF))


def _fold(t, op):
    out = t[0:8, :]
    for i in range(1, TILE // 8):
        out = op(out, t[8 * i:8 * i + 8, :])
    return out


def _attn_kernel(q_ref, k_ref, vt_ref, qi_ref, ki_ref, wit_ref, bias_ref, o_ref,
                 key_scr, qih_scr, qh_scr, mb_scr, s_scr, m_scr, acc_scr,
                 *, topk, pos_bits):
    qb = pl.program_id(1)
    nst = qb // GROUP + 1
    kf = float(topk)
    zgap = 1 << pos_bits
    lane = lax.broadcasted_iota(jnp.int32, (TILE, LANES), 1)
    krow = lax.broadcasted_iota(jnp.int32, (TILE, LANES), 0)
    lo_half = lane < HEAD_DIM

    wit = wit_ref[...]
    qi = qi_ref[0]
    q = q_ref[0]
    for src, dst, n_pairs in ((qi, qih_scr, N_IDX_HEADS // 2), (q, qh_scr, H_A // 2)):
        for g in range(n_pairs):
            pair = src[:, g * LANES:(g + 1) * LANES]
            dst[g, 0:TILE, :] = jnp.where(lo_half, pair, jnp.zeros_like(pair))
            dst[g, TILE:2 * TILE, :] = jnp.where(lo_half, jnp.zeros_like(pair), pair)

    adm = (krow // CHUNK) <= (lane // CHUNK)
    inadm = NEG_KEY - zgap - 1

    def score_key(s):
        key = _sortable(s)
        return jnp.where(key < 0, key - zgap, key)

    def p1(st, c):
        base = st * GROUP
        kis = ki_ref[0, pl.ds(pl.multiple_of(base * TILE, GROUP * TILE), GROUP * TILE), :]
        acc = [None] * GROUP
        for g in range(N_IDX_HEADS // 2):
            d = lax.dot_general(kis, qih_scr[g], NT_DIMS, preferred_element_type=F32)
            for par in range(2):
                w_row = wit[2 * g + par:2 * g + par + 1, :]
                for u in range(GROUP):
                    t = jnp.maximum(d[u * TILE:(u + 1) * TILE, par * TILE:(par + 1) * TILE], 0.0) * w_row
                    acc[u] = t if acc[u] is None else acc[u] + t
        for u in range(GROUP):
            kt = base + u
            key = jnp.where(acc[u] == 0.0, -1 - (krow + kt * TILE), score_key(acc[u]))
            ok = jnp.logical_or(kt < qb, jnp.logical_and(kt == qb, adm))
            key_scr[kt] = jnp.where(ok, key, inadm)
            c = jnp.maximum(c, _fold(jnp.where(ok, acc[u], -jnp.inf), jnp.maximum))
            mb_scr[u] = jnp.where(ok, 0.0, NEG_BIG)
        k_s = k_ref[0, pl.ds(pl.multiple_of(base * TILE, GROUP * TILE), GROUP * TILE), :]
        for g in range(H_A // 2):
            s = lax.dot_general(k_s[:, g * LANES:(g + 1) * LANES], qh_scr[g], NT_DIMS,
                                preferred_element_type=F32)
            for par in range(2):
                h = 2 * g + par
                mx = m_scr[h]
                for u in range(GROUP):
                    dd = jnp.clip(base + u - qb, -N_FAR, 0) + N_FAR
                    su = (s[u * TILE:(u + 1) * TILE, par * TILE:(par + 1) * TILE]
                          + bias_ref[h * (N_FAR + 1) + dd] + mb_scr[u])
                    s_scr[h, base + u] = su
                    mx = jnp.maximum(mx, _fold(su, jnp.maximum))
                m_scr[h] = mx
        return c

    for h in range(H_A):
        m_scr[h] = jnp.full((8, LANES), NEG_BIG, F32)
    smax8 = lax.fori_loop(0, nst, p1, jnp.full((8, LANES), -jnp.inf, F32))
    fmax = jnp.max(smax8, axis=0, keepdims=True)

    def count_ge(thr_row):
        def body(st, c):
            for u in range(GROUP):
                c = c + _fold(jnp.where(key_scr[st * GROUP + u] >= thr_row, 1.0, 0.0), jnp.add)
            return c
        c = lax.fori_loop(0, nst, body, jnp.zeros((8, LANES), F32))
        return jnp.sum(c, axis=0, keepdims=True)

    def bisect(lo, hi):
        mid = (lo >> 1) + (hi >> 1) + (lo & hi & 1)
        cnt = count_ge(mid)
        ge = cnt >= kf
        hi = jnp.where(cnt == kf, mid + 1, jnp.where(ge, hi, mid))
        return jnp.where(ge, mid, lo), hi

    kmax = jnp.where(fmax == 0.0, 0, score_key(fmax))
    lo_try = score_key(jnp.where(fmax > 0.0, fmax * (1.0 / BRACKET), fmax * BRACKET))
    narrow = count_ge(lo_try) >= kf
    lo0 = jnp.where(narrow, lo_try, INT_MIN)
    hi0 = jnp.where(narrow, jnp.minimum(kmax, INT_MAX - 1) + 1, INT_MAX)
    all_narrow = jnp.min(jnp.where(narrow, 1.0, 0.0)) > 0.0
    n_steps = jnp.where(all_narrow, NARROW_STEPS, 32)
    lo, hi = lax.fori_loop(0, n_steps - LATE_STEPS, lambda _, lh: bisect(*lh), (lo0, hi0))

    def unsettled(lo, hi):
        return jnp.max(jnp.where(hi != lo + 1, 1.0, 0.0)) > 0.0

    def late(carry):
        lo, hi, it, _ = carry
        lo, hi = bisect(*bisect(lo, hi))
        return lo, hi, it + 2, unsettled(lo, hi)

    thr, _, _, _ = lax.while_loop(lambda carry: jnp.logical_and(carry[2] < n_steps, carry[3]),
                                  late, (lo, hi, n_steps - LATE_STEPS, unsettled(lo, hi)))
    cnt_ge = count_ge(thr)
    tie = jnp.logical_and(cnt_ge > kf, thr > inadm)

    @pl.when(jnp.max(jnp.where(tie, 1.0, 0.0)) > 0.0)
    def _():
        need = kf - count_ge(thr + 1)

        def count_eq_le(pmax):
            def body(kt, c):
                hit = jnp.where(key_scr[kt] == thr, krow + kt * TILE, INT_MAX) <= pmax
                return c + _fold(jnp.where(hit, 1.0, 0.0), jnp.add)
            c = lax.fori_loop(0, nst * GROUP, body, jnp.zeros((8, LANES), F32))
            return jnp.sum(c, axis=0, keepdims=True)

        def pbisect(_, lohi):
            plo, phi = lohi
            pmid = (plo + phi) >> 1
            ok = count_eq_le(pmid) >= need
            return jnp.where(ok, plo, pmid), jnp.where(ok, pmid, phi)

        plo0 = jnp.full((1, LANES), -1, jnp.int32)
        phi0 = jnp.full((1, LANES), (1 << pos_bits) - 1, jnp.int32)
        _, pthr = lax.fori_loop(0, pos_bits + 1, pbisect, (plo0, phi0))
        pthr = jnp.where(tie, pthr, INT_MAX)

        def drop(kt, c):
            kk = key_scr[kt]
            pos = jnp.where(kk == thr, krow + kt * TILE, -1)
            key_scr[kt] = jnp.where(pos > pthr, inadm, kk)
            return c

        lax.fori_loop(0, nst * GROUP, drop, 0)

    thr_eff = jnp.maximum(thr, inadm + 1)

    ones_rows = jnp.ones((ONES_ROWS, GROUP * TILE), BF16)

    def collapse_max():
        for h in range(H_A):
            m_scr[h] = jnp.broadcast_to(jnp.max(m_scr[h], axis=0, keepdims=True), (8, LANES))

    def select_bias(base):
        for u in range(GROUP):
            mb_scr[u] = jnp.where(key_scr[base + u] >= thr_eff, 0.0, NEG_BIG)

    def pv(st, c):
        select_bias(st * GROUP)
        for h in range(H_A):
            shift = m_scr[h][0:1, :]
            ps = [jnp.exp((s_scr[h, st * GROUP + u] + mb_scr[u] - shift).astype(BF16))
                  for u in range(GROUP)]
            lhs = jnp.concatenate([vt_ref[st, h * HEAD_DIM:(h + 1) * HEAD_DIM, :], ones_rows], axis=0)
            acc_scr[h] += jnp.dot(lhs, jnp.concatenate(ps, axis=0), preferred_element_type=F32)
        return c

    def run_pv():
        for h in range(H_A):
            acc_scr[h] = jnp.zeros((HEAD_DIM + ONES_ROWS, LANES), F32)
        lax.fori_loop(0, nst, pv, 0)

    collapse_max()
    run_pv()
    den_min = acc_scr[0][HEAD_DIM:HEAD_DIM + 1, :]
    for h in range(1, H_A):
        den_min = jnp.minimum(den_min, acc_scr[h][HEAD_DIM:HEAD_DIM + 1, :])

    @pl.when(jnp.logical_not(jnp.min(den_min) > DEN_FLOOR))
    def _():
        for h in range(H_A):
            m_scr[h] = jnp.full((8, LANES), NEG_BIG, F32)

        def selected_max(st, c):
            select_bias(st * GROUP)
            for h in range(H_A):
                mx = m_scr[h]
                for u in range(GROUP):
                    mx = jnp.maximum(mx, _fold(s_scr[h, st * GROUP + u] + mb_scr[u], jnp.maximum))
                m_scr[h] = mx
            return c

        lax.fori_loop(0, nst, selected_max, 0)
        collapse_max()
        run_pv()

    outs = []
    for h in range(H_A):
        acc = acc_scr[h]
        outs.append(acc[0:HEAD_DIM, :] / acc[HEAD_DIM:HEAD_DIM + 1, :])
    o_ref[0] = jnp.concatenate(outs, axis=0).T


def _attention(q, k, vt, qi, ki, wit, bias_tiles, topk):
    b, s, _ = q.shape
    nt = s // TILE
    ng = s // (GROUP * TILE)
    kern = functools.partial(_attn_kernel, topk=topk, pos_bits=max(1, int(math.ceil(math.log2(s)))))
    qtile = lambda bb, i: (bb, i, 0)
    whole = lambda bb, i: (bb, 0, 0)
    once = pl.Buffered(1)
    return pl.pallas_call(
        kern,
        grid=(b, nt),
        in_specs=[pl.BlockSpec((1, TILE, D_A), qtile),
                  pl.BlockSpec((1, s, D_A), whole, pipeline_mode=once),
                  pl.BlockSpec((ng, D_A, GROUP * TILE), whole, pipeline_mode=once),
                  pl.BlockSpec((1, TILE, 512), qtile),
                  pl.BlockSpec((1, s, LANES), whole, pipeline_mode=once),
                  pl.BlockSpec((N_IDX_HEADS, TILE), lambda bb, i: (0, bb * nt + i)),
                  pl.BlockSpec(bias_tiles.shape, lambda bb, i: (0, 0, 0), pipeline_mode=once)],
        out_specs=pl.BlockSpec((1, TILE, D_A), qtile),
        out_shape=jax.ShapeDtypeStruct((b, s, D_A), F32),
        scratch_shapes=[pltpu.VMEM((nt, TILE, LANES), jnp.int32),
                        pltpu.VMEM((N_IDX_HEADS // 2, 2 * TILE, LANES), BF16),
                        pltpu.VMEM((H_A // 2, 2 * TILE, LANES), BF16),
                        pltpu.VMEM((GROUP, TILE, LANES), F32),
                        pltpu.VMEM((H_A, nt, TILE, LANES), F32),
                        pltpu.VMEM((H_A, 8, LANES), F32),
                        pltpu.VMEM((H_A, HEAD_DIM + ONES_ROWS, LANES), F32)],
        compiler_params=pltpu.CompilerParams(
            dimension_semantics=("arbitrary", "arbitrary"), vmem_limit_bytes=VMEM_LIMIT),
    )(q, k, vt, qi, ki, wit, bias_tiles)


def _t5_bucket(rel):
    nb = N_BUCKETS // 2
    max_exact = nb // 2
    ret = jnp.where(rel > 0, nb, 0)
    n = jnp.abs(rel)
    nf = jnp.maximum(n, 1).astype(F32)
    large = max_exact + (jnp.log(nf / max_exact) / math.log(MAX_DISTANCE / max_exact)
                         * (nb - max_exact)).astype(jnp.int32)
    large = jnp.minimum(large, nb - 1)
    return ret + jnp.where(n < max_exact, n, large)


def _bias_tiles(rel_bias):
    j = jnp.arange(TILE, dtype=jnp.int32)[:, None]
    t = jnp.arange(TILE, dtype=jnp.int32)[None, :]
    tiles = [rel_bias[_t5_bucket((dd - N_FAR) * TILE + j - t)] for dd in range(N_FAR + 1)]
    arr = jnp.stack(tiles).astype(F32)
    return arr.transpose(3, 0, 1, 2).reshape(H_A * (N_FAR + 1), TILE, TILE)


CONV_HALO = 32


def _conv_kernel(cur_ref, halo_ref, w_ref, b_ref, g_ref, beta_ref, o_ref, u_scr, *, tt):
    i = pl.program_id(1)

    def glu(t):
        return t[:, :D_B] * _sigmoid(t[:, D_B:])

    u_scr[0:CONV_HALO, :] = jnp.where(i > 0, glu(halo_ref[0]), 0.0)
    u_scr[CONV_HALO:CONV_HALO + tt, :] = glu(cur_ref[0])
    rows = 64
    first = CONV_HALO - (CONV_WIDTH - 1)
    for r0 in range(0, tt, rows):
        acc = jnp.broadcast_to(b_ref[...], (rows, D_B))
        for j in range(CONV_WIDTH):
            acc = acc + w_ref[j:j + 1, :] * u_scr[r0 + first + j:r0 + first + j + rows, :]
        mu = jnp.mean(acc, axis=-1, keepdims=True)
        cen = acc - mu
        var = jnp.mean(cen * cen, axis=-1, keepdims=True)
        y = cen * lax.rsqrt(var + EPS) * g_ref[...] + beta_ref[...]
        o_ref[0, r0:r0 + rows, :] = _silu(y)


def _conv(conv_in, w, b, g, beta, tt):
    bsz, s, _ = conv_in.shape
    per = tt // CONV_HALO
    vec = lambda bb, i: (0, 0)
    return pl.pallas_call(
        functools.partial(_conv_kernel, tt=tt),
        grid=(bsz, s // tt),
        in_specs=[pl.BlockSpec((1, tt, 2 * D_B), lambda bb, i: (bb, i, 0)),
                  pl.BlockSpec((1, CONV_HALO, 2 * D_B),
                               lambda bb, i: (bb, jnp.maximum(i * per - 1, 0), 0)),
                  pl.BlockSpec((CONV_WIDTH, D_B), vec),
                  pl.BlockSpec((1, D_B), vec),
                  pl.BlockSpec((1, D_B), vec),
                  pl.BlockSpec((1, D_B), vec)],
        out_specs=pl.BlockSpec((1, tt, D_B), lambda bb, i: (bb, i, 0)),
        out_shape=jax.ShapeDtypeStruct((bsz, s, D_B), F32),
        scratch_shapes=[pltpu.VMEM((CONV_HALO + tt, D_B), F32)],
        compiler_params=pltpu.CompilerParams(
            dimension_semantics=("arbitrary", "arbitrary"), vmem_limit_bytes=VMEM_LIMIT),
    )(conv_in, conv_in, w, b, g, beta)


def _rwkv_kernel(rw_ref, mu_ref, lora_ref, w0_ref, a0_ref, kk_ref, ka_ref, rk_ref, lg_ref, lb_ref,
                 o_ref, s_scr, prev_scr):
    c = pl.program_id(1)
    n = CHUNK
    nb = RW_CHUNKS * CHUNK

    @pl.when(c == 0)
    def _():
        s_scr[...] = jnp.zeros_like(s_scr)
        prev_scr[...] = jnp.zeros_like(prev_scr)

    ps = rw_ref[0]
    row = lax.broadcasted_iota(jnp.int32, (nb, 1), 0)
    prev = jnp.where(row == 0, prev_scr[0:1, :], pltpu.roll(ps, 1, axis=0))
    prev_scr[0:1, :] = ps[nb - 1:nb, :]
    xs = ps + mu_ref[...] * (prev - ps)
    r = xs[:, 0:D_C]
    k = xs[:, D_C:2 * D_C]
    v = xs[:, 2 * D_C:3 * D_C]
    dn = xs[:, 3 * D_C:3 * D_C + 2 * D_LORA]
    lane64 = lax.broadcasted_iota(jnp.int32, (nb, 2 * D_LORA), 1)
    dn = jnp.where(lane64 < D_LORA, jnp.tanh(dn), dn)
    dn_hi, dn_lo = _split2(dn)
    lora_hi, lora_lo = _split2(lora_ref[...])
    pre = _bdot(dn_hi, lora_hi) + _bdot(dn_hi, lora_lo) + _bdot(dn_lo, lora_hi)
    z = -(w0_ref[...] + pre[:, :D_C])
    softplus = jnp.maximum(z, 0.0) + jnp.log(1.0 + jnp.exp(-jnp.abs(z)))
    logdec = -jnp.exp(-softplus - 0.5)
    a = _sigmoid(a0_ref[...] + pre[:, D_C:])
    kkraw = k * kk_ref[...]
    k2 = k * (1.0 + (a - 1.0) * ka_ref[...])
    bonus_pre = r * k2 * rk_ref[...]

    rb = lax.broadcasted_iota(jnp.int32, (nb, nb), 0)
    cb = lax.broadcasted_iota(jnp.int32, (nb, nb), 1)
    tri = jnp.where(jnp.logical_and(rb // n == cb // n, cb <= rb), 1.0, 0.0)
    ld_hi, ld_rest = _split2(logdec)
    ld_mid, ld_lo = _split2(ld_rest)
    cum_all = _bdot(tri, ld_hi) + _bdot(tri, ld_mid) + _bdot(tri, ld_lo)
    p_in_all = jnp.exp(cum_all)
    p_inv_all = jnp.exp(-cum_all)
    p_ex_all = jnp.exp(cum_all - logdec)

    row_n = lax.broadcasted_iota(jnp.int32, (n, LANES), 0)
    lane_n = lax.broadcasted_iota(jnp.int32, (n, LANES), 1)
    lo = lane_n < HEAD_DIM
    col = lane_n & (HEAD_DIM - 1)
    incl = col <= row_n
    strict = col < row_n
    mask2 = jnp.concatenate([strict, incl], axis=0)
    r2 = lax.broadcasted_iota(jnp.int32, (2 * n, LANES), 0)
    l2 = lax.broadcasted_iota(jnp.int32, (2 * n, LANES), 1)
    same_head = (r2 // HEAD_DIM) == (l2 // HEAD_DIM)
    ones_bd = jnp.where(same_head, 1.0, 0.0).astype(BF16)

    def bd(t):
        tb = t.astype(BF16)
        z = jnp.zeros_like(tb)
        return jnp.concatenate([jnp.where(lo, tb, z), jnp.where(lo, z, tb)], axis=0)

    def hsum(t):
        return jnp.dot(t.astype(BF16), ones_bd, preferred_element_type=F32)

    pairs = range(H_C // 2)
    chains = [(j, g) for j in range(RW_CHUNKS) for g in pairs]
    ids = range(len(chains))
    sl = lambda t, i: t[chains[i][0] * n:(chains[i][0] + 1) * n,
                        chains[i][1] * LANES:(chains[i][1] + 1) * LANES]
    v_p = [sl(v, i) for i in ids]
    p_in = [sl(p_in_all, i) for i in ids]
    kkn = []
    for i in ids:
        kr = sl(kkraw, i)
        kkn.append(kr * lax.rsqrt(jnp.maximum(hsum(kr * kr), 1e-12)))
    rt = [sl(r, i) * p_in[i] for i in ids]
    at = [kkn[i] * sl(p_ex_all, i) for i in ids]
    kt = [sl(k2, i) * sl(p_inv_all, i) for i in ids]
    bt = [kkn[i] * sl(a, i) * sl(p_inv_all, i) for i in ids]
    lhs = [jnp.concatenate([at[i], rt[i]], axis=0).astype(BF16) for i in ids]
    ak = [lax.dot_general(lhs[i], bd(kt[i]), NT_DIMS, preferred_element_type=F32) for i in ids]
    ab = [lax.dot_general(lhs[i], bd(bt[i]), NT_DIMS, preferred_element_type=F32) for i in ids]
    av = [_bdot(jnp.where(mask2, ak[i], 0.0), bd(v_p[i])) for i in ids]
    a_rb = [jnp.where(incl, ab[i][n:], 0.0) for i in ids]
    npow = [jnp.where(strict, -ab[i][:n], 0.0) for i in ids]
    tm = list(npow)
    for _ in range(5):
        both = [_bdot(jnp.concatenate([npow[i], tm[i]], axis=0), bd(npow[i])) for i in ids]
        npow = [both[i][:n] for i in ids]
        tm = [tm[i] + both[i][n:] for i in ids]
    tm = [tm[i] + _bdot(tm[i], bd(npow[i])) for i in ids]
    wt = [at[i] + _bdot(tm[i], bd(at[i])) for i in ids]
    uloc = [av[i][:n] + _bdot(tm[i], bd(av[i][:n])) for i in ids]
    mcat = [jnp.concatenate([kt[i], bt[i]], axis=0).astype(BF16) for i in ids]
    bonus = [hsum(sl(bonus_pre, i)) * v_p[i] for i in ids]

    state = [s_scr[g] for g in pairs]
    for j in range(RW_CHUNKS):
        cid = [j * (H_C // 2) + g for g in pairs]
        ws = [lax.dot_general(jnp.concatenate([wt[i], rt[i]], axis=0).astype(BF16),
                              state[g].astype(BF16), NT_DIMS, preferred_element_type=F32)
              for g, i in zip(pairs, cid)]
        u = [ws[g][:n] + uloc[i] for g, i in zip(pairs, cid)]
        o = [ws[g][n:] + av[i][n:] - _bdot(a_rb[i], bd(u[g])) for g, i in zip(pairs, cid)]
        for g, i in zip(pairs, cid):
            ncat = jnp.concatenate([v_p[i], -u[g]], axis=0).astype(BF16)
            upd = lax.dot_general(ncat, mcat[i], TN_DIMS, preferred_element_type=F32)
            state[g] = (state[g] + jnp.where(same_head, upd, 0.0)) * p_in[i][n - 1:n, :]
        for g, i in zip(pairs, cid):
            lanes = slice(g * LANES, (g + 1) * LANES)
            cen = o[g] - hsum(o[g]) * (1.0 / HEAD_DIM)
            var = hsum(cen * cen) * (1.0 / HEAD_DIM)
            y = cen * lax.rsqrt(var + LNX_EPS) * lg_ref[:, lanes] + lb_ref[:, lanes]
            o_ref[0, j * n:(j + 1) * n, lanes] = y + bonus[i]
    for g in pairs:
        s_scr[g] = state[g]


def _rwkv(rw, mu, lora, w0, a0, kk, ka, rk, lg, lb):
    bsz, s, _ = rw.shape
    vec = lambda bb, i: (0, 0)
    return pl.pallas_call(
        _rwkv_kernel,
        grid=(bsz, s // (RW_CHUNKS * CHUNK)),
        in_specs=[pl.BlockSpec((1, RW_CHUNKS * CHUNK, RW_COLS), lambda bb, i: (bb, i, 0)),
                  pl.BlockSpec((1, RW_COLS), vec),
                  pl.BlockSpec((2 * D_LORA, 2 * D_C), vec)] +
                 [pl.BlockSpec((1, D_C), vec)] * 7,
        out_specs=pl.BlockSpec((1, RW_CHUNKS * CHUNK, D_C), lambda bb, i: (bb, i, 0)),
        out_shape=jax.ShapeDtypeStruct((bsz, s, D_C), F32),
        scratch_shapes=[pltpu.VMEM((H_C // 2, 2 * HEAD_DIM, LANES), F32),
                        pltpu.VMEM((8, RW_COLS), F32)],
        compiler_params=pltpu.CompilerParams(
            dimension_semantics=("arbitrary", "arbitrary"), vmem_limit_bytes=VMEM_LIMIT),
    )(rw, mu, lora, w0, a0, kk, ka, rk, lg, lb)


def _outproj_kernel(x_ref, attn_ref, conv_ref, rw_ref, gates_ref, gm_ref, w_ref, o_ref):
    sg = _silu(gates_ref[...])
    ya = (attn_ref[...] * sg[:, 0:D_A]).astype(BF16)
    yb = (conv_ref[...] * sg[:, D_A:D_A + D_B]).astype(BF16)
    yc = (rw_ref[...] * sg[:, D_A + D_B:D_MIX]).astype(BF16)
    y = (jnp.dot(ya, w_ref[0:D_A, :], preferred_element_type=F32)
         + jnp.dot(yb, w_ref[D_A:D_A + D_B, :], preferred_element_type=F32)
         + jnp.dot(yc, w_ref[D_A + D_B:D_MIX, :], preferred_element_type=F32))
    o_ref[...] = x_ref[...] + gm_ref[0] * y


def _outproj(x2, attn, conv, rw, gates, gate_mod, w, seq, tm):
    n, d = x2.shape
    per_b = seq // tm
    row = lambda i: (i, 0)
    return pl.pallas_call(
        _outproj_kernel,
        grid=(n // tm,),
        in_specs=[pl.BlockSpec((tm, d), row),
                  pl.BlockSpec((tm, D_A), row),
                  pl.BlockSpec((tm, D_B), row),
                  pl.BlockSpec((tm, D_C), row),
                  pl.BlockSpec((tm, D_MIX), row),
                  pl.BlockSpec((1, 1, d), lambda i: (i // per_b, 0, 0)),
                  pl.BlockSpec((D_MIX, d), lambda i: (0, 0))],
        out_specs=pl.BlockSpec((tm, d), row),
        out_shape=jax.ShapeDtypeStruct((n, d), F32),
        compiler_params=pltpu.CompilerParams(
            dimension_semantics=("arbitrary",), vmem_limit_bytes=VMEM_LIMIT),
    )(x2, attn, conv, rw, gates, gate_mod, w)


def _reorder_w_in(w_in):
    depth, d, _ = w_in.shape
    a_cols = 4 * D_A + N_IDX_HEADS * D_IDX + D_IDX + N_IDX_HEADS
    b0 = a_cols
    c0 = a_cols + 3 * D_B
    c_shift = 3 * D_C + 2 * D_LORA
    zeros = lambda m: jnp.zeros((depth, d, m), w_in.dtype)
    qi0 = 4 * D_A
    ki0 = qi0 + N_IDX_HEADS * D_IDX
    wi0 = ki0 + D_IDX
    parts = [w_in[..., 0:3 * D_A],
             w_in[..., qi0:ki0],
             w_in[..., ki0:wi0], w_in[..., ki0:wi0],
             w_in[..., wi0:a_cols], zeros(LANES - N_IDX_HEADS),
             w_in[..., 3 * D_A:4 * D_A], w_in[..., b0 + 2 * D_B:b0 + 3 * D_B],
             w_in[..., c0 + c_shift:c0 + c_shift + D_C],
             w_in[..., b0:b0 + 2 * D_B],
             w_in[..., c0:c0 + c_shift], zeros(RW_COLS - c_shift)]
    w = jnp.concatenate(parts, axis=-1)
    assert w.shape[-1] == N_COLS
    return w.astype(BF16)


def kernel(x, c, norm_g, w_ada, b_ada, w_in, w_out, q_norm_g, k_norm_g, rel_bias, conv_w, conv_b,
           conv_ln_g, conv_ln_b, shift_mu, decay_w0, decay_up, iclr_a0, iclr_up, key_k, key_a,
           bonus_r_k, lnx_g, lnx_b):
    bsz, seq, d = x.shape
    depth = w_in.shape[0]
    assert seq % (GROUP * TILE) == 0 and d == w_out.shape[-1]
    topk = min(TOPK_MAX, seq // 4)
    tm = GROUP * TILE
    conv_tt = 256

    mod = _modulation(c, w_ada, b_ada)
    shift = mod[:, :, None, 0:d]
    sc1p = 1.0 + mod[:, :, None, d:2 * d]
    gate_mod = mod[:, :, None, 2 * d:3 * d]

    w_all = _reorder_w_in(w_in)
    head = jnp.arange(D_A) // HEAD_DIM
    avg = (head[:, None] == head[None, :]).astype(BF16) * (1.0 / HEAD_DIM)
    qg = jnp.tile(q_norm_g, (1, H_A))[:, None, :] * (HEAD_DIM ** -0.5)
    kg = jnp.tile(k_norm_g, (1, H_A))[:, None, :]
    bias_tiles = _bias_tiles(rel_bias)
    c_shift = 3 * D_C + 2 * D_LORA
    mu_pad = jnp.pad(shift_mu, ((0, 0), (0, RW_COLS - c_shift)))[:, None, :]
    zl = jnp.zeros((depth, D_LORA, D_C), F32)
    lora = jnp.concatenate([jnp.concatenate([decay_up, zl], axis=2),
                            jnp.concatenate([zl, iclr_up], axis=2)], axis=1)
    row = lambda t: t[:, None, :]
    layers = dict(
        shift=shift, sc1p=sc1p, gate_mod=gate_mod, norm_g=row(norm_g), w=w_all, qg=qg, kg=kg,
        w_out=w_out.astype(BF16), conv_w=conv_w, conv_b=row(conv_b), conv_g=row(conv_ln_g),
        conv_beta=row(conv_ln_b), mu=mu_pad, lora=lora, w0=row(decay_w0), a0=row(iclr_a0),
        kk=row(key_k), ka=row(key_a), rk=row(bonus_r_k.reshape(depth, D_C)), lg=row(lnx_g),
        lb=row(lnx_b))

    def layer(x2, p):
        q, k, vt, qi, ki, wit, gates, conv_in, rw_in = _inproj(
            x2, p["sc1p"], p["shift"], p["norm_g"], p["w"], avg, p["qg"], p["kg"], seq, tm)
        r3 = lambda t: t.reshape(bsz, seq, t.shape[-1])
        attn = _attention(r3(q), r3(k), vt, r3(qi), r3(ki), wit, bias_tiles, topk)
        conv = _conv(r3(conv_in), p["conv_w"], p["conv_b"], p["conv_g"], p["conv_beta"], conv_tt)
        rwo = _rwkv(r3(rw_in), p["mu"], p["lora"], p["w0"], p["a0"], p["kk"], p["ka"], p["rk"],
                    p["lg"], p["lb"])
        x2 = _outproj(x2, attn.reshape(-1, D_A), conv.reshape(-1, D_B), rwo.reshape(-1, D_C),
                      gates, p["gate_mod"], p["w_out"], seq, tm)
        return x2

    x2 = x.reshape(bsz * seq, d)
    for l in range(depth):
        x2 = layer(x2, {name: t[l] for name, t in layers.items()})
    return x2.reshape(bsz, seq, d)
```

```python
import functools
import math

import jax
import jax.numpy as jnp
import numpy as np
from jax import lax
from jax.experimental import pallas as pl
from jax.experimental.pallas import tpu as pltpu

F32 = jnp.float32
BF16 = jnp.bfloat16

HEAD_DIM = 64
CHUNK = 64
EPS = 1e-6
H_A = 6
D_A = H_A * HEAD_DIM
N_IDX_HEADS = 8
D_IDX = 64
TOPK_MAX = 256
N_BUCKETS = 32
MAX_DISTANCE = 1024
D_B = 256
CONV_WIDTH = 31
H_C = 6
D_C = H_C * HEAD_DIM
D_LORA = 32
LNX_EPS = 64e-5
D_MIX = D_A + D_B + D_C

LANES = 128
TILE = 128
GROUP = 4
BRACKET = 16.0
NARROW_STEPS = 26
DEN_FLOOR = 1e-30
LATE_STEPS = 14
ONES_ROWS = 16
RW_CHUNKS = 4
N_FAR = 9
VMEM_LIMIT = 48 * 1024 * 1024

C_Q, C_K, C_V = 0, 384, 768
C_QI = 1152
C_KI = 1664
C_WI = 1792
C_GATES = 1920
C_CONV = 2944
C_RW = 3456
N_COLS = 4736
RW_COLS = 1280

INT_MIN = -(2 ** 31)
INT_MAX = 2 ** 31 - 1
NEG_KEY = int(np.array(-np.inf, np.float32).view(np.int32)) ^ 0x7FFFFFFF
NEG_BIG = -1e30

NT_DIMS = (((1,), (1,)), ((), ()))
TN_DIMS = (((0,), (0,)), ((), ()))


def _bdot(a, b):
    return jnp.dot(a.astype(BF16), b.astype(BF16), preferred_element_type=F32)


def _bdot_nt(a, b):
    return lax.dot_general(a.astype(BF16), b.astype(BF16), NT_DIMS, preferred_element_type=F32)


def _split2(x):
    hi = x.astype(BF16).astype(F32)
    return hi, x - hi


def _sigmoid(x):
    return 1.0 / (1.0 + jnp.exp(-x))


def _silu(x):
    return x * _sigmoid(x)


def _mod_kernel(c_ref, w_ref, b_ref, o_ref):
    ca = _silu(c_ref[...])
    o_ref[0] = jnp.dot(ca, w_ref[0], precision=lax.Precision.HIGHEST,
                       preferred_element_type=F32) + b_ref[0]


def _modulation(c, w_ada, b_ada):
    depth, d, d3 = w_ada.shape
    b = c.shape[0]
    return pl.pallas_call(
        _mod_kernel,
        grid=(depth, d3 // d),
        in_specs=[pl.BlockSpec((b, d), lambda l, j: (0, 0)),
                  pl.BlockSpec((1, d, d), lambda l, j: (l, 0, j)),
                  pl.BlockSpec((1, 1, d), lambda l, j: (l, 0, j))],
        out_specs=pl.BlockSpec((1, b, d), lambda l, j: (l, 0, j)),
        out_shape=jax.ShapeDtypeStruct((depth, b, d3), F32),
        compiler_params=pltpu.CompilerParams(
            dimension_semantics=("arbitrary", "arbitrary"), vmem_limit_bytes=VMEM_LIMIT),
    )(c, w_ada, b_ada.reshape(depth, 1, d3))


def _inproj_kernel(x_ref, sc_ref, sh_ref, g_ref, w_ref, avg_ref, qg_ref, kg_ref,
                   q_ref, k_ref, vt_ref, qi_ref, ki_ref, wit_ref, gates_ref, conv_ref, rw_ref):
    x = x_ref[...]
    ms = jnp.mean(x * x, axis=-1, keepdims=True)
    h = x * lax.rsqrt(ms + EPS) * g_ref[...]
    hb = (h * sc_ref[0] + sh_ref[0]).astype(BF16)

    def mm(c0, c1):
        return jnp.dot(hb, w_ref[:, c0:c1], preferred_element_type=F32)

    def head_rms(t, g):
        m2 = jnp.dot((t * t).astype(BF16), avg_ref[...], preferred_element_type=F32)
        return t * lax.rsqrt(m2 + EPS) * g

    q_ref[...] = head_rms(mm(C_Q, C_Q + D_A), qg_ref[...]).astype(BF16)
    k_ref[...] = head_rms(mm(C_K, C_K + D_A), kg_ref[...]).astype(BF16)
    vt_ref[0] = mm(C_V, C_V + D_A).T.astype(BF16)
    qi_ref[...] = mm(C_QI, C_QI + 512).astype(BF16)
    ki_ref[...] = mm(C_KI, C_KI + LANES).astype(BF16)
    wit_ref[...] = mm(C_WI, C_WI + LANES).T[:N_IDX_HEADS, :]
    gates_ref[:, 0:512] = mm(C_GATES, C_GATES + 512)
    gates_ref[:, 512:1024] = mm(C_GATES + 512, C_GATES + 1024)
    conv_ref[...] = mm(C_CONV, C_CONV + 512)
    rw_ref[:, 0:512] = mm(C_RW, C_RW + 512)
    rw_ref[:, 512:1024] = mm(C_RW + 512, C_RW + 1024)
    rw_ref[:, 1024:RW_COLS] = mm(C_RW + 1024, C_RW + RW_COLS)


def _inproj(x2, sc1p, shift, norm_g, w, avg, qg, kg, seq, tm):
    n, d = x2.shape
    per_b = seq // tm
    row = lambda i: (i, 0)
    full = lambda i: (0, 0)
    bat = lambda i: (i // per_b, 0, 0)
    rows = lambda wd, dt: (pl.BlockSpec((tm, wd), row), jax.ShapeDtypeStruct((n, wd), dt))
    outs = [rows(D_A, BF16), rows(D_A, BF16),
            (pl.BlockSpec((1, D_A, tm), lambda i: (i, 0, 0)),
             jax.ShapeDtypeStruct((n // tm, D_A, tm), BF16)),
            rows(512, BF16), rows(LANES, BF16),
            (pl.BlockSpec((N_IDX_HEADS, tm), lambda i: (0, i)),
             jax.ShapeDtypeStruct((N_IDX_HEADS, n), F32)),
            rows(D_MIX, F32), rows(512, F32), rows(RW_COLS, F32)]
    return pl.pallas_call(
        _inproj_kernel,
        grid=(n // tm,),
        in_specs=[pl.BlockSpec((tm, d), row),
                  pl.BlockSpec((1, 1, d), bat),
                  pl.BlockSpec((1, 1, d), bat),
                  pl.BlockSpec((1, d), full),
                  pl.BlockSpec((d, N_COLS), full),
                  pl.BlockSpec((D_A, D_A), full),
                  pl.BlockSpec((1, D_A), full),
                  pl.BlockSpec((1, D_A), full)],
        out_specs=[o[0] for o in outs],
        out_shape=[o[1] for o in outs],
        compiler_params=pltpu.CompilerParams(
            dimension_semantics=("arbitrary",), vmem_limit_bytes=VMEM_LIMIT),
    )(x2, sc1p, shift, norm_g, w, avg, qg, kg)


def _fold(t, op):
    out = t[0:8, :]
    for i in range(1, TILE // 8):
        out = op(out, t[8 * i:8 * i + 8, :])
    return out


def _attn_kernel(q_ref, k_ref, vt_ref, qi_ref, ki_ref, wit_ref, bias_ref, o_ref,
                 key_scr, qih_scr, qh_scr, mb_scr, s_scr, m_scr, acc_scr,
                 *, topk, pos_bits):
    qb = pl.program_id(1)
    nst = qb // GROUP + 1
    kf = float(topk)
    zgap = 1 << pos_bits
    lane = lax.broadcasted_iota(jnp.int32, (TILE, LANES), 1)
    krow = lax.broadcasted_iota(jnp.int32, (TILE, LANES), 0)
    lo_half = lane < HEAD_DIM

    wit = wit_ref[...]
    qi = qi_ref[0]
    q = q_ref[0]
    for src, dst, n_pairs in ((qi, qih_scr, N_IDX_HEADS // 2), (q, qh_scr, H_A // 2)):
        for g in range(n_pairs):
            pair = src[:, g * LANES:(g + 1) * LANES]
            dst[g, 0:TILE, :] = jnp.where(lo_half, pair, jnp.zeros_like(pair))
            dst[g, TILE:2 * TILE, :] = jnp.where(lo_half, jnp.zeros_like(pair), pair)

    adm = (krow // CHUNK) <= (lane // CHUNK)
    inadm = NEG_KEY - zgap - 1

    def score_key(s):
        b = lax.bitcast_convert_type(s, jnp.int32)
        return jnp.where(b < 0, (b ^ jnp.int32(0x7FFFFFFF)) - zgap, b)

    def p1(st, c):
        base = st * GROUP
        kis = ki_ref[0, pl.ds(pl.multiple_of(base * TILE, GROUP * TILE), GROUP * TILE), :]
        acc = [None] * GROUP
        for g in range(N_IDX_HEADS // 2):
            d = lax.dot_general(kis, qih_scr[g], NT_DIMS, preferred_element_type=F32)
            for par in range(2):
                w_row = wit[2 * g + par:2 * g + par + 1, :]
                for u in range(GROUP):
                    t = jnp.maximum(d[u * TILE:(u + 1) * TILE, par * TILE:(par + 1) * TILE], 0.0) * w_row
                    acc[u] = t if acc[u] is None else acc[u] + t
        for u in range(GROUP):
            kt = base + u
            key = jnp.where(acc[u] == 0.0, -1 - (krow + kt * TILE), score_key(acc[u]))
            ok = jnp.logical_or(kt < qb, jnp.logical_and(kt == qb, adm))
            key_scr[kt] = jnp.where(ok, key, inadm)
            c = jnp.maximum(c, _fold(jnp.where(ok, acc[u], -jnp.inf), jnp.maximum))
        k_s = k_ref[0, pl.ds(pl.multiple_of(base * TILE, GROUP * TILE), GROUP * TILE), :]
        for g in range(H_A // 2):
            s = lax.dot_general(k_s[:, g * LANES:(g + 1) * LANES], qh_scr[g], NT_DIMS,
                                preferred_element_type=F32)
            for par in range(2):
                h = 2 * g + par
                mx = m_scr[h]
                for u in range(GROUP):
                    dd = jnp.clip(base + u - qb, -N_FAR, 1) + N_FAR
                    su = (s[u * TILE:(u + 1) * TILE, par * TILE:(par + 1) * TILE]
                          + bias_ref[h * (N_FAR + 2) + dd])
                    s_scr[h, base + u] = su
                    mx = jnp.maximum(mx, _fold(su, jnp.maximum))
                m_scr[h] = mx
        return c

    for h in range(H_A):
        m_scr[h] = jnp.full((8, LANES), NEG_BIG, F32)
    smax8 = lax.fori_loop(0, nst, p1, jnp.full((8, LANES), -jnp.inf, F32))
    fmax = jnp.max(smax8, axis=0, keepdims=True)

    def count_ge(thr_row):
        def body(st, c):
            for u in range(GROUP):
                c = c + _fold(jnp.where(key_scr[st * GROUP + u] >= thr_row, 1.0, 0.0), jnp.add)
            return c
        c = lax.fori_loop(0, nst, body, jnp.zeros((8, LANES), F32))
        return jnp.sum(c, axis=0, keepdims=True)

    def bisect(lo, hi):
        mid = (lo >> 1) + (hi >> 1) + (lo & hi & 1)
        cnt = count_ge(mid)
        ge = cnt >= kf
        hi = jnp.where(cnt == kf, mid + 1, jnp.where(ge, hi, mid))
        return jnp.where(ge, mid, lo), hi

    kmax = jnp.where(fmax == 0.0, 0, score_key(fmax))
    lo_try = score_key(jnp.where(fmax > 0.0, fmax * (1.0 / BRACKET), fmax * BRACKET))
    narrow = count_ge(lo_try) >= kf
    lo0 = jnp.where(narrow, lo_try, INT_MIN)
    hi0 = jnp.where(narrow, jnp.minimum(kmax, INT_MAX - 1) + 1, INT_MAX)
    all_narrow = jnp.min(jnp.where(narrow, 1.0, 0.0)) > 0.0
    n_steps = jnp.where(all_narrow, NARROW_STEPS, 32)
    lo, hi = lax.fori_loop(0, n_steps - LATE_STEPS, lambda _, lh: bisect(*lh), (lo0, hi0))

    def unsettled(lo, hi):
        return jnp.max(jnp.where(hi != lo + 1, 1.0, 0.0)) > 0.0

    def late(carry):
        lo, hi, it, _ = carry
        lo, hi = bisect(*bisect(lo, hi))
        return lo, hi, it + 2, unsettled(lo, hi)

    thr, _, _, _ = lax.while_loop(lambda carry: jnp.logical_and(carry[2] < n_steps, carry[3]),
                                  late, (lo, hi, n_steps - LATE_STEPS, unsettled(lo, hi)))
    cnt_ge = count_ge(thr)
    tie = jnp.logical_and(cnt_ge > kf, thr > inadm)

    @pl.when(jnp.max(jnp.where(tie, 1.0, 0.0)) > 0.0)
    def _():
        need = kf - count_ge(thr + 1)

        def count_eq_le(pmax):
            def body(kt, c):
                hit = jnp.where(key_scr[kt] == thr, krow + kt * TILE, INT_MAX) <= pmax
                return c + _fold(jnp.where(hit, 1.0, 0.0), jnp.add)
            c = lax.fori_loop(0, nst * GROUP, body, jnp.zeros((8, LANES), F32))
            return jnp.sum(c, axis=0, keepdims=True)

        def pbisect(_, lohi):
            plo, phi = lohi
            pmid = (plo + phi) >> 1
            ok = count_eq_le(pmid) >= need
            return jnp.where(ok, plo, pmid), jnp.where(ok, pmid, phi)

        plo0 = jnp.full((1, LANES), -1, jnp.int32)
        phi0 = jnp.full((1, LANES), (1 << pos_bits) - 1, jnp.int32)
        _, pthr = lax.fori_loop(0, pos_bits + 1, pbisect, (plo0, phi0))
        pthr = jnp.where(tie, pthr, INT_MAX)

        def drop(kt, c):
            kk = key_scr[kt]
            pos = jnp.where(kk == thr, krow + kt * TILE, -1)
            key_scr[kt] = jnp.where(pos > pthr, inadm, kk)
            return c

        lax.fori_loop(0, nst * GROUP, drop, 0)

    thr_eff = jnp.maximum(thr, inadm + 1)

    ones_rows = jnp.ones((ONES_ROWS, GROUP * TILE), BF16)

    def collapse_max():
        for h in range(H_A):
            m_scr[h] = jnp.broadcast_to(jnp.max(m_scr[h], axis=0, keepdims=True), (8, LANES))

    def select_bias(base):
        for u in range(GROUP):
            mb_scr[u] = jnp.where(key_scr[base + u] >= thr_eff, 0.0, NEG_BIG)

    def pv(st, c):
        select_bias(st * GROUP)
        for h in range(H_A):
            shift = m_scr[h][0:1, :]
            ps = [jnp.exp((s_scr[h, st * GROUP + u] + mb_scr[u] - shift).astype(BF16))
                  for u in range(GROUP)]
            lhs = jnp.concatenate([vt_ref[st, h * HEAD_DIM:(h + 1) * HEAD_DIM, :], ones_rows], axis=0)
            acc_scr[h] += jnp.dot(lhs, jnp.concatenate(ps, axis=0), preferred_element_type=F32)
        return c

    def run_pv():
        for h in range(H_A):
            acc_scr[h] = jnp.zeros((HEAD_DIM + ONES_ROWS, LANES), F32)
        lax.fori_loop(0, nst, pv, 0)

    collapse_max()
    run_pv()
    den_min = acc_scr[0][HEAD_DIM:HEAD_DIM + 1, :]
    for h in range(1, H_A):
        den_min = jnp.minimum(den_min, acc_scr[h][HEAD_DIM:HEAD_DIM + 1, :])

    @pl.when(jnp.logical_not(jnp.min(den_min) > DEN_FLOOR))
    def _():
        for h in range(H_A):
            m_scr[h] = jnp.full((8, LANES), NEG_BIG, F32)

        def selected_max(st, c):
            select_bias(st * GROUP)
            for h in range(H_A):
                mx = m_scr[h]
                for u in range(GROUP):
                    mx = jnp.maximum(mx, _fold(s_scr[h, st * GROUP + u] + mb_scr[u], jnp.maximum))
                m_scr[h] = mx
            return c

        lax.fori_loop(0, nst, selected_max, 0)
        collapse_max()
        run_pv()

    outs = []
    for h in range(H_A):
        acc = acc_scr[h]
        outs.append(acc[0:HEAD_DIM, :] / acc[HEAD_DIM:HEAD_DIM + 1, :])
    o_ref[0] = jnp.concatenate(outs, axis=0).T


def _attention(q, k, vt, qi, ki, wit, bias_tiles, topk):
    b, s, _ = q.shape
    nt = s // TILE
    ng = s // (GROUP * TILE)
    kern = functools.partial(_attn_kernel, topk=topk, pos_bits=max(1, int(math.ceil(math.log2(s)))))
    qtile = lambda bb, i: (bb, i, 0)
    whole = lambda bb, i: (bb, 0, 0)
    once = pl.Buffered(1)
    return pl.pallas_call(
        kern,
        grid=(b, nt),
        in_specs=[pl.BlockSpec((1, TILE, D_A), qtile),
                  pl.BlockSpec((1, s, D_A), whole, pipeline_mode=once),
                  pl.BlockSpec((ng, D_A, GROUP * TILE), whole, pipeline_mode=once),
                  pl.BlockSpec((1, TILE, 512), qtile),
                  pl.BlockSpec((1, s, LANES), whole, pipeline_mode=once),
                  pl.BlockSpec((N_IDX_HEADS, TILE), lambda bb, i: (0, bb * nt + i)),
                  pl.BlockSpec(bias_tiles.shape, lambda bb, i: (0, 0, 0), pipeline_mode=once)],
        out_specs=pl.BlockSpec((1, TILE, D_A), qtile),
        out_shape=jax.ShapeDtypeStruct((b, s, D_A), F32),
        scratch_shapes=[pltpu.VMEM((nt, TILE, LANES), jnp.int32),
                        pltpu.VMEM((N_IDX_HEADS // 2, 2 * TILE, LANES), BF16),
                        pltpu.VMEM((H_A // 2, 2 * TILE, LANES), BF16),
                        pltpu.VMEM((GROUP, TILE, LANES), F32),
                        pltpu.VMEM((H_A, nt, TILE, LANES), F32),
                        pltpu.VMEM((H_A, 8, LANES), F32),
                        pltpu.VMEM((H_A, HEAD_DIM + ONES_ROWS, LANES), F32)],
        compiler_params=pltpu.CompilerParams(
            dimension_semantics=("arbitrary", "arbitrary"), vmem_limit_bytes=VMEM_LIMIT),
    )(q, k, vt, qi, ki, wit, bias_tiles)


def _t5_bucket(rel):
    nb = N_BUCKETS // 2
    max_exact = nb // 2
    ret = jnp.where(rel > 0, nb, 0)
    n = jnp.abs(rel)
    nf = jnp.maximum(n, 1).astype(F32)
    large = max_exact + (jnp.log(nf / max_exact) / math.log(MAX_DISTANCE / max_exact)
                         * (nb - max_exact)).astype(jnp.int32)
    large = jnp.minimum(large, nb - 1)
    return ret + jnp.where(n < max_exact, n, large)


def _bias_tiles(rel_bias):
    j = jnp.arange(TILE, dtype=jnp.int32)[:, None]
    t = jnp.arange(TILE, dtype=jnp.int32)[None, :]
    tiles = [rel_bias[_t5_bucket((dd - N_FAR) * TILE + j - t)].astype(F32) for dd in range(N_FAR + 1)]
    tiles[N_FAR] = jnp.where((j // CHUNK <= t // CHUNK)[:, :, None], tiles[N_FAR], NEG_BIG)
    tiles.append(jnp.full_like(tiles[0], NEG_BIG))
    arr = jnp.stack(tiles)
    return arr.transpose(3, 0, 1, 2).reshape(H_A * (N_FAR + 2), TILE, TILE)


CONV_HALO = 32


def _conv_kernel(cur_ref, halo_ref, w_ref, b_ref, g_ref, beta_ref, o_ref, u_scr, sh_scr, *, tt):
    i = pl.program_id(1)

    def glu(t):
        return t[:, :D_B] * _sigmoid(t[:, D_B:])

    u_scr[0:CONV_HALO, :] = jnp.where(i > 0, glu(halo_ref[0]), 0.0)
    u_scr[CONV_HALO:CONV_HALO + tt, :] = glu(cur_ref[0])
    rows = 64
    first = CONV_HALO - (CONV_WIDTH - 1)
    for r0 in range(0, tt, rows):
        acc = jnp.broadcast_to(b_ref[...], (rows, D_B))
        for res in range(8):
            offs = [first + j for j in range(CONV_WIDTH) if (first + j) % 8 == res]
            span = offs[-1] - offs[0] + rows
            sh_scr[0:span, :] = u_scr[r0 + offs[0]:r0 + offs[0] + span, :]
            for off in offs:
                acc = acc + w_ref[off - first:off - first + 1, :] * sh_scr[off - offs[0]:off - offs[0] + rows, :]
        mu = jnp.mean(acc, axis=-1, keepdims=True)
        cen = acc - mu
        var = jnp.mean(cen * cen, axis=-1, keepdims=True)
        y = cen * lax.rsqrt(var + EPS) * g_ref[...] + beta_ref[...]
        o_ref[0, r0:r0 + rows, :] = _silu(y)


def _conv(conv_in, w, b, g, beta, tt):
    bsz, s, _ = conv_in.shape
    per = tt // CONV_HALO
    vec = lambda bb, i: (0, 0)
    return pl.pallas_call(
        functools.partial(_conv_kernel, tt=tt),
        grid=(bsz, s // tt),
        in_specs=[pl.BlockSpec((1, tt, 2 * D_B), lambda bb, i: (bb, i, 0)),
                  pl.BlockSpec((1, CONV_HALO, 2 * D_B),
                               lambda bb, i: (bb, jnp.maximum(i * per - 1, 0), 0)),
                  pl.BlockSpec((CONV_WIDTH, D_B), vec),
                  pl.BlockSpec((1, D_B), vec),
                  pl.BlockSpec((1, D_B), vec),
                  pl.BlockSpec((1, D_B), vec)],
        out_specs=pl.BlockSpec((1, tt, D_B), lambda bb, i: (bb, i, 0)),
        out_shape=jax.ShapeDtypeStruct((bsz, s, D_B), F32),
        scratch_shapes=[pltpu.VMEM((CONV_HALO + tt, D_B), F32),
                        pltpu.VMEM((CONV_HALO + 64, D_B), F32)],
        compiler_params=pltpu.CompilerParams(
            dimension_semantics=("arbitrary", "arbitrary"), vmem_limit_bytes=VMEM_LIMIT),
    )(conv_in, conv_in, w, b, g, beta)


def _rwkv_kernel(rw_ref, mu_ref, lora_ref, w0_ref, a0_ref, kk_ref, ka_ref, rk_ref, lg_ref, lb_ref,
                 o_ref, s_scr, prev_scr):
    c = pl.program_id(1)
    n = CHUNK
    nb = RW_CHUNKS * CHUNK

    @pl.when(c == 0)
    def _():
        s_scr[...] = jnp.zeros_like(s_scr)
        prev_scr[...] = jnp.zeros_like(prev_scr)

    ps = rw_ref[0]
    row = lax.broadcasted_iota(jnp.int32, (nb, 1), 0)
    prev = jnp.where(row == 0, prev_scr[0:1, :], pltpu.roll(ps, 1, axis=0))
    prev_scr[0:1, :] = ps[nb - 1:nb, :]
    xs = ps + mu_ref[...] * (prev - ps)
    r = xs[:, 0:D_C]
    k = xs[:, D_C:2 * D_C]
    v = xs[:, 2 * D_C:3 * D_C]
    dn = xs[:, 3 * D_C:3 * D_C + 2 * D_LORA]
    lane64 = lax.broadcasted_iota(jnp.int32, (nb, 2 * D_LORA), 1)
    dn = jnp.where(lane64 < D_LORA, jnp.tanh(dn), dn)
    dn_hi, dn_lo = _split2(dn)
    lora_hi, lora_lo = _split2(lora_ref[...])
    pre = _bdot(dn_hi, lora_hi) + _bdot(dn_hi, lora_lo) + _bdot(dn_lo, lora_hi)
    z = -(w0_ref[...] + pre[:, :D_C])
    softplus = jnp.maximum(z, 0.0) + jnp.log(1.0 + jnp.exp(-jnp.abs(z)))
    logdec = -jnp.exp(-softplus - 0.5)
    a = _sigmoid(a0_ref[...] + pre[:, D_C:])
    kkraw = k * kk_ref[...]
    k2 = k * (1.0 + (a - 1.0) * ka_ref[...])
    bonus_pre = r * k2 * rk_ref[...]

    rb = lax.broadcasted_iota(jnp.int32, (nb, nb), 0)
    cb = lax.broadcasted_iota(jnp.int32, (nb, nb), 1)
    tri = jnp.where(jnp.logical_and(rb // n == cb // n, cb <= rb), 1.0, 0.0)
    ld_hi, ld_rest = _split2(logdec)
    ld_mid, ld_lo = _split2(ld_rest)
    cum_all = _bdot(tri, ld_hi) + _bdot(tri, ld_mid) + _bdot(tri, ld_lo)
    p_in_all = jnp.exp(cum_all)
    p_inv_all = jnp.exp(-cum_all)
    p_ex_all = jnp.exp(cum_all - logdec)

    row_n = lax.broadcasted_iota(jnp.int32, (n, LANES), 0)
    lane_n = lax.broadcasted_iota(jnp.int32, (n, LANES), 1)
    lo = lane_n < HEAD_DIM
    col = lane_n & (HEAD_DIM - 1)
    incl = col <= row_n
    strict = col < row_n
    mask2 = jnp.concatenate([strict, incl], axis=0)
    r2 = lax.broadcasted_iota(jnp.int32, (2 * n, LANES), 0)
    l2 = lax.broadcasted_iota(jnp.int32, (2 * n, LANES), 1)
    same_head = (r2 // HEAD_DIM) == (l2 // HEAD_DIM)
    ones_bd = jnp.where(same_head, 1.0, 0.0).astype(BF16)

    def bd(t):
        tb = t.astype(BF16)
        z = jnp.zeros_like(tb)
        return jnp.concatenate([jnp.where(lo, tb, z), jnp.where(lo, z, tb)], axis=0)

    def hsum(t):
        return jnp.dot(t.astype(BF16), ones_bd, preferred_element_type=F32)

    pairs = range(H_C // 2)
    chains = [(j, g) for j in range(RW_CHUNKS) for g in pairs]
    ids = range(len(chains))
    sl = lambda t, i: t[chains[i][0] * n:(chains[i][0] + 1) * n,
                        chains[i][1] * LANES:(chains[i][1] + 1) * LANES]
    v_p = [sl(v, i) for i in ids]
    p_in = [sl(p_in_all, i) for i in ids]
    kkn = []
    for i in ids:
        kr = sl(kkraw, i)
        kkn.append(kr * lax.rsqrt(jnp.maximum(hsum(kr * kr), 1e-12)))
    rt = [sl(r, i) * p_in[i] for i in ids]
    at = [kkn[i] * sl(p_ex_all, i) for i in ids]
    kt = [sl(k2, i) * sl(p_inv_all, i) for i in ids]
    bt = [kkn[i] * sl(a, i) * sl(p_inv_all, i) for i in ids]
    lhs = [jnp.concatenate([at[i], rt[i]], axis=0).astype(BF16) for i in ids]
    ak = [lax.dot_general(lhs[i], bd(kt[i]), NT_DIMS, preferred_element_type=F32) for i in ids]
    ab = [lax.dot_general(lhs[i], bd(bt[i]), NT_DIMS, preferred_element_type=F32) for i in ids]
    av = [_bdot(jnp.where(mask2, ak[i], 0.0), bd(v_p[i])) for i in ids]
    a_rb = [jnp.where(incl, ab[i][n:], 0.0) for i in ids]
    npow = [jnp.where(strict, -ab[i][:n], 0.0) for i in ids]
    tm = list(npow)
    for _ in range(5):
        both = [_bdot(jnp.concatenate([npow[i], tm[i]], axis=0), bd(npow[i])) for i in ids]
        npow = [both[i][:n] for i in ids]
        tm = [tm[i] + both[i][n:] for i in ids]
    tm = [tm[i] + _bdot(tm[i], bd(npow[i])) for i in ids]
    wt = [at[i] + _bdot(tm[i], bd(at[i])) for i in ids]
    uloc = [av[i][:n] + _bdot(tm[i], bd(av[i][:n])) for i in ids]
    mcat = [jnp.concatenate([kt[i], bt[i]], axis=0).astype(BF16) for i in ids]
    bonus = [hsum(sl(bonus_pre, i)) * v_p[i] for i in ids]

    state = [s_scr[g] for g in pairs]
    for j in range(RW_CHUNKS):
        cid = [j * (H_C // 2) + g for g in pairs]
        ws = [lax.dot_general(jnp.concatenate([wt[i], rt[i]], axis=0).astype(BF16),
                              state[g].astype(BF16), NT_DIMS, preferred_element_type=F32)
              for g, i in zip(pairs, cid)]
        u = [ws[g][:n] + uloc[i] for g, i in zip(pairs, cid)]
        o = [ws[g][n:] + av[i][n:] - _bdot(a_rb[i], bd(u[g])) for g, i in zip(pairs, cid)]
        for g, i in zip(pairs, cid):
            ncat = jnp.concatenate([v_p[i], -u[g]], axis=0).astype(BF16)
            upd = lax.dot_general(ncat, mcat[i], TN_DIMS, preferred_element_type=F32)
            state[g] = (state[g] + jnp.where(same_head, upd, 0.0)) * p_in[i][n - 1:n, :]
        for g, i in zip(pairs, cid):
            lanes = slice(g * LANES, (g + 1) * LANES)
            cen = o[g] - hsum(o[g]) * (1.0 / HEAD_DIM)
            var = hsum(cen * cen) * (1.0 / HEAD_DIM)
            y = cen * lax.rsqrt(var + LNX_EPS) * lg_ref[:, lanes] + lb_ref[:, lanes]
            o_ref[0, j * n:(j + 1) * n, lanes] = y + bonus[i]
    for g in pairs:
        s_scr[g] = state[g]


def _rwkv(rw, mu, lora, w0, a0, kk, ka, rk, lg, lb):
    bsz, s, _ = rw.shape
    vec = lambda bb, i: (0, 0)
    return pl.pallas_call(
        _rwkv_kernel,
        grid=(bsz, s // (RW_CHUNKS * CHUNK)),
        in_specs=[pl.BlockSpec((1, RW_CHUNKS * CHUNK, RW_COLS), lambda bb, i: (bb, i, 0)),
                  pl.BlockSpec((1, RW_COLS), vec),
                  pl.BlockSpec((2 * D_LORA, 2 * D_C), vec)] +
                 [pl.BlockSpec((1, D_C), vec)] * 7,
        out_specs=pl.BlockSpec((1, RW_CHUNKS * CHUNK, D_C), lambda bb, i: (bb, i, 0)),
        out_shape=jax.ShapeDtypeStruct((bsz, s, D_C), F32),
        scratch_shapes=[pltpu.VMEM((H_C // 2, 2 * HEAD_DIM, LANES), F32),
                        pltpu.VMEM((8, RW_COLS), F32)],
        compiler_params=pltpu.CompilerParams(
            dimension_semantics=("arbitrary", "arbitrary"), vmem_limit_bytes=VMEM_LIMIT),
    )(rw, mu, lora, w0, a0, kk, ka, rk, lg, lb)


def _outproj_kernel(x_ref, attn_ref, conv_ref, rw_ref, gates_ref, gm_ref, w_ref, o_ref):
    sg = _silu(gates_ref[...])
    ya = (attn_ref[...] * sg[:, 0:D_A]).astype(BF16)
    yb = (conv_ref[...] * sg[:, D_A:D_A + D_B]).astype(BF16)
    yc = (rw_ref[...] * sg[:, D_A + D_B:D_MIX]).astype(BF16)
    y = (jnp.dot(ya, w_ref[0:D_A, :], preferred_element_type=F32)
         + jnp.dot(yb, w_ref[D_A:D_A + D_B, :], preferred_element_type=F32)
         + jnp.dot(yc, w_ref[D_A + D_B:D_MIX, :], preferred_element_type=F32))
    o_ref[...] = x_ref[...] + gm_ref[0] * y


def _outproj(x2, attn, conv, rw, gates, gate_mod, w, seq, tm):
    n, d = x2.shape
    per_b = seq // tm
    row = lambda i: (i, 0)
    return pl.pallas_call(
        _outproj_kernel,
        grid=(n // tm,),
        in_specs=[pl.BlockSpec((tm, d), row),
                  pl.BlockSpec((tm, D_A), row),
                  pl.BlockSpec((tm, D_B), row),
                  pl.BlockSpec((tm, D_C), row),
                  pl.BlockSpec((tm, D_MIX), row),
                  pl.BlockSpec((1, 1, d), lambda i: (i // per_b, 0, 0)),
                  pl.BlockSpec((D_MIX, d), lambda i: (0, 0))],
        out_specs=pl.BlockSpec((tm, d), row),
        out_shape=jax.ShapeDtypeStruct((n, d), F32),
        compiler_params=pltpu.CompilerParams(
            dimension_semantics=("arbitrary",), vmem_limit_bytes=VMEM_LIMIT),
    )(x2, attn, conv, rw, gates, gate_mod, w)


def _reorder_w_in(w_in):
    depth, d, _ = w_in.shape
    a_cols = 4 * D_A + N_IDX_HEADS * D_IDX + D_IDX + N_IDX_HEADS
    b0 = a_cols
    c0 = a_cols + 3 * D_B
    c_shift = 3 * D_C + 2 * D_LORA
    zeros = lambda m: jnp.zeros((depth, d, m), w_in.dtype)
    qi0 = 4 * D_A
    ki0 = qi0 + N_IDX_HEADS * D_IDX
    wi0 = ki0 + D_IDX
    parts = [w_in[..., 0:3 * D_A],
             w_in[..., qi0:ki0],
             w_in[..., ki0:wi0], w_in[..., ki0:wi0],
             w_in[..., wi0:a_cols], zeros(LANES - N_IDX_HEADS),
             w_in[..., 3 * D_A:4 * D_A], w_in[..., b0 + 2 * D_B:b0 + 3 * D_B],
             w_in[..., c0 + c_shift:c0 + c_shift + D_C],
             w_in[..., b0:b0 + 2 * D_B],
             w_in[..., c0:c0 + c_shift], zeros(RW_COLS - c_shift)]
    w = jnp.concatenate(parts, axis=-1)
    assert w.shape[-1] == N_COLS
    return w.astype(BF16)


def kernel(x, c, norm_g, w_ada, b_ada, w_in, w_out, q_norm_g, k_norm_g, rel_bias, conv_w, conv_b,
           conv_ln_g, conv_ln_b, shift_mu, decay_w0, decay_up, iclr_a0, iclr_up, key_k, key_a,
           bonus_r_k, lnx_g, lnx_b):
    bsz, seq, d = x.shape
    depth = w_in.shape[0]
    assert seq % (GROUP * TILE) == 0 and d == w_out.shape[-1]
    topk = min(TOPK_MAX, seq // 4)
    tm = GROUP * TILE
    conv_tt = 256

    mod = _modulation(c, w_ada, b_ada)
    shift = mod[:, :, None, 0:d]
    sc1p = 1.0 + mod[:, :, None, d:2 * d]
    gate_mod = mod[:, :, None, 2 * d:3 * d]

    w_all = _reorder_w_in(w_in)
    head = jnp.arange(D_A) // HEAD_DIM
    avg = (head[:, None] == head[None, :]).astype(BF16) * (1.0 / HEAD_DIM)
    qg = jnp.tile(q_norm_g, (1, H_A))[:, None, :] * (HEAD_DIM ** -0.5)
    kg = jnp.tile(k_norm_g, (1, H_A))[:, None, :]
    bias_tiles = _bias_tiles(rel_bias)
    c_shift = 3 * D_C + 2 * D_LORA
    mu_pad = jnp.pad(shift_mu, ((0, 0), (0, RW_COLS - c_shift)))[:, None, :]
    zl = jnp.zeros((depth, D_LORA, D_C), F32)
    lora = jnp.concatenate([jnp.concatenate([decay_up, zl], axis=2),
                            jnp.concatenate([zl, iclr_up], axis=2)], axis=1)
    row = lambda t: t[:, None, :]
    layers = dict(
        shift=shift, sc1p=sc1p, gate_mod=gate_mod, norm_g=row(norm_g), w=w_all, qg=qg, kg=kg,
        w_out=w_out.astype(BF16), conv_w=conv_w, conv_b=row(conv_b), conv_g=row(conv_ln_g),
        conv_beta=row(conv_ln_b), mu=mu_pad, lora=lora, w0=row(decay_w0), a0=row(iclr_a0),
        kk=row(key_k), ka=row(key_a), rk=row(bonus_r_k.reshape(depth, D_C)), lg=row(lnx_g),
        lb=row(lnx_b))

    def layer(x2, p):
        q, k, vt, qi, ki, wit, gates, conv_in, rw_in = _inproj(
            x2, p["sc1p"], p["shift"], p["norm_g"], p["w"], avg, p["qg"], p["kg"], seq, tm)
        r3 = lambda t: t.reshape(bsz, seq, t.shape[-1])
        attn = _attention(r3(q), r3(k), vt, r3(qi), r3(ki), wit, bias_tiles, topk)
        conv = _conv(r3(conv_in), p["conv_w"], p["conv_b"], p["conv_g"], p["conv_beta"], conv_tt)
        rwo = _rwkv(r3(rw_in), p["mu"], p["lora"], p["w0"], p["a0"], p["kk"], p["ka"], p["rk"],
                    p["lg"], p["lb"])
        x2 = _outproj(x2, attn.reshape(-1, D_A), conv.reshape(-1, D_B), rwo.reshape(-1, D_C),
                      gates, p["gate_mod"], p["w_out"], seq, tm)
        return x2

    x2 = x.reshape(bsz * seq, d)
    for l in range(depth):
        x2 = layer(x2, {name: t[l] for name, t in layers.items()})
    return x2.reshape(bsz, seq, d)
```

```python
import functools
import math

import jax
import jax.numpy as jnp
import numpy as np
from jax import lax
from jax.experimental import pallas as pl
from jax.experimental.pallas import tpu as pltpu

F32 = jnp.float32
BF16 = jnp.bfloat16

HEAD_DIM = 64
CHUNK = 64
EPS = 1e-6
H_A = 6
D_A = H_A * HEAD_DIM
N_IDX_HEADS = 8
D_IDX = 64
TOPK_MAX = 256
N_BUCKETS = 32
MAX_DISTANCE = 1024
D_B = 256
CONV_WIDTH = 31
H_C = 6
D_C = H_C * HEAD_DIM
D_LORA = 32
LNX_EPS = 64e-5
D_MIX = D_A + D_B + D_C

LANES = 128
TILE = 128
GROUP = 4
BRACKET = 16.0
NARROW_STEPS = 26
DEN_FLOOR = 1e-30
LATE_STEPS = 14
ONES_ROWS = 16
RW_CHUNKS = 4
N_FAR = 9
VMEM_LIMIT = 48 * 1024 * 1024

C_Q, C_K, C_V = 0, 384, 768
C_QI = 1152
C_KI = 1664
C_WI = 1792
C_GATES = 1920
C_CONV = 2944
C_RW = 3456
N_COLS = 4736
RW_COLS = 1280

INT_MIN = -(2 ** 31)
INT_MAX = 2 ** 31 - 1
NEG_KEY = int(np.array(-np.inf, np.float32).view(np.int32)) ^ 0x7FFFFFFF
NEG_BIG = -1e30

NT_DIMS = (((1,), (1,)), ((), ()))
TN_DIMS = (((0,), (0,)), ((), ()))


def _bdot(a, b):
    return jnp.dot(a.astype(BF16), b.astype(BF16), preferred_element_type=F32)


def _bdot_nt(a, b):
    return lax.dot_general(a.astype(BF16), b.astype(BF16), NT_DIMS, preferred_element_type=F32)


def _split2(x):
    hi = x.astype(BF16).astype(F32)
    return hi, x - hi


def _sigmoid(x):
    return 1.0 / (1.0 + jnp.exp(-x))


def _silu(x):
    return x * _sigmoid(x)


def _mod_kernel(c_ref, w_ref, b_ref, o_ref):
    ca = _silu(c_ref[...])
    o_ref[0] = jnp.dot(ca, w_ref[0], precision=lax.Precision.HIGHEST,
                       preferred_element_type=F32) + b_ref[0]


def _modulation(c, w_ada, b_ada):
    depth, d, d3 = w_ada.shape
    b = c.shape[0]
    return pl.pallas_call(
        _mod_kernel,
        grid=(depth, d3 // d),
        in_specs=[pl.BlockSpec((b, d), lambda l, j: (0, 0)),
                  pl.BlockSpec((1, d, d), lambda l, j: (l, 0, j)),
                  pl.BlockSpec((1, 1, d), lambda l, j: (l, 0, j))],
        out_specs=pl.BlockSpec((1, b, d), lambda l, j: (l, 0, j)),
        out_shape=jax.ShapeDtypeStruct((depth, b, d3), F32),
        compiler_params=pltpu.CompilerParams(
            dimension_semantics=("arbitrary", "arbitrary"), vmem_limit_bytes=VMEM_LIMIT),
    )(c, w_ada, b_ada.reshape(depth, 1, d3))


def _inproj_kernel(x_ref, sc_ref, sh_ref, g_ref, w_ref, avg_ref, qg_ref, kg_ref,
                   q_ref, k_ref, vt_ref, qi_ref, ki_ref, wit_ref, gates_ref, conv_ref, rw_ref):
    x = x_ref[...]
    ms = jnp.mean(x * x, axis=-1, keepdims=True)
    h = x * lax.rsqrt(ms + EPS) * g_ref[...]
    hb = (h * sc_ref[0] + sh_ref[0]).astype(BF16)

    def mm(c0, c1):
        return jnp.dot(hb, w_ref[:, c0:c1], preferred_element_type=F32)

    def head_rms(t, g):
        m2 = jnp.dot((t * t).astype(BF16), avg_ref[...], preferred_element_type=F32)
        return t * lax.rsqrt(m2 + EPS) * g

    q_ref[...] = head_rms(mm(C_Q, C_Q + D_A), qg_ref[...]).astype(BF16)
    k_ref[...] = head_rms(mm(C_K, C_K + D_A), kg_ref[...]).astype(BF16)
    vt_ref[0] = mm(C_V, C_V + D_A).T.astype(BF16)
    qi_ref[...] = mm(C_QI, C_QI + 512).astype(BF16)
    ki_ref[...] = mm(C_KI, C_KI + LANES).astype(BF16)
    wit_ref[...] = mm(C_WI, C_WI + LANES).T[:N_IDX_HEADS, :]
    gates_ref[:, 0:512] = mm(C_GATES, C_GATES + 512).astype(BF16)
    gates_ref[:, 512:1024] = mm(C_GATES + 512, C_GATES + 1024).astype(BF16)
    conv_ref[...] = mm(C_CONV, C_CONV + 512)
    rw_ref[:, 0:512] = mm(C_RW, C_RW + 512)
    rw_ref[:, 512:1024] = mm(C_RW + 512, C_RW + 1024)
    rw_ref[:, 1024:RW_COLS] = mm(C_RW + 1024, C_RW + RW_COLS)


def _inproj(x2, sc1p, shift, norm_g, w, avg, qg, kg, seq, tm):
    n, d = x2.shape
    per_b = seq // tm
    row = lambda i: (i, 0)
    full = lambda i: (0, 0)
    bat = lambda i: (i // per_b, 0, 0)
    rows = lambda wd, dt: (pl.BlockSpec((tm, wd), row), jax.ShapeDtypeStruct((n, wd), dt))
    outs = [rows(D_A, BF16), rows(D_A, BF16),
            (pl.BlockSpec((1, D_A, tm), lambda i: (i, 0, 0)),
             jax.ShapeDtypeStruct((n // tm, D_A, tm), BF16)),
            rows(512, BF16), rows(LANES, BF16),
            (pl.BlockSpec((N_IDX_HEADS, tm), lambda i: (0, i)),
             jax.ShapeDtypeStruct((N_IDX_HEADS, n), F32)),
            rows(D_MIX, BF16), rows(512, F32), rows(RW_COLS, F32)]
    return pl.pallas_call(
        _inproj_kernel,
        grid=(n // tm,),
        in_specs=[pl.BlockSpec((tm, d), row),
                  pl.BlockSpec((1, 1, d), bat),
                  pl.BlockSpec((1, 1, d), bat),
                  pl.BlockSpec((1, d), full),
                  pl.BlockSpec((d, N_COLS), full),
                  pl.BlockSpec((D_A, D_A), full),
                  pl.BlockSpec((1, D_A), full),
                  pl.BlockSpec((1, D_A), full)],
        out_specs=[o[0] for o in outs],
        out_shape=[o[1] for o in outs],
        compiler_params=pltpu.CompilerParams(
            dimension_semantics=("arbitrary",), vmem_limit_bytes=VMEM_LIMIT),
    )(x2, sc1p, shift, norm_g, w, avg, qg, kg)


def _fold(t, op):
    parts = [t[8 * i:8 * i + 8, :] for i in range(TILE // 8)]
    while len(parts) > 1:
        parts = [op(parts[i], parts[i + 1]) for i in range(0, len(parts), 2)]
    return parts[0]


def _attn_kernel(q_ref, k_ref, vt_ref, qi_ref, ki_ref, wit_ref, bias_ref, o_ref,
                 key_scr, qih_scr, qh_scr, mb_scr, s_scr, m_scr, acc_scr,
                 *, topk, pos_bits):
    qb = pl.program_id(1)
    nst = qb // GROUP + 1
    kf = float(topk)
    zgap = 1 << pos_bits
    lane = lax.broadcasted_iota(jnp.int32, (TILE, LANES), 1)
    krow = lax.broadcasted_iota(jnp.int32, (TILE, LANES), 0)
    lo_half = lane < HEAD_DIM

    wit = wit_ref[...]
    qi = qi_ref[0]
    q = q_ref[0]
    for src, dst, n_pairs in ((qi, qih_scr, N_IDX_HEADS // 2), (q, qh_scr, H_A // 2)):
        for g in range(n_pairs):
            pair = src[:, g * LANES:(g + 1) * LANES]
            dst[g, 0:TILE, :] = jnp.where(lo_half, pair, jnp.zeros_like(pair))
            dst[g, TILE:2 * TILE, :] = jnp.where(lo_half, jnp.zeros_like(pair), pair)

    adm = (krow // CHUNK) <= (lane // CHUNK)
    inadm = NEG_KEY - zgap - 1

    def score_key(s):
        b = lax.bitcast_convert_type(s, jnp.int32)
        return jnp.where(b < 0, (b ^ jnp.int32(0x7FFFFFFF)) - zgap, b)

    def p1(st, c):
        base = st * GROUP
        kis = ki_ref[0, pl.ds(pl.multiple_of(base * TILE, GROUP * TILE), GROUP * TILE), :]
        acc = [None] * GROUP
        for g in range(N_IDX_HEADS // 2):
            d = lax.dot_general(kis, qih_scr[g], NT_DIMS, preferred_element_type=F32)
            for par in range(2):
                w_row = wit[2 * g + par:2 * g + par + 1, :]
                for u in range(GROUP):
                    t = jnp.maximum(d[u * TILE:(u + 1) * TILE, par * TILE:(par + 1) * TILE], 0.0) * w_row
                    acc[u] = t if acc[u] is None else acc[u] + t
        for u in range(GROUP):
            kt = base + u
            key = jnp.where(acc[u] == 0.0, -1 - (krow + kt * TILE), score_key(acc[u]))
            ok = jnp.logical_or(kt < qb, jnp.logical_and(kt == qb, adm))
            key_scr[kt] = jnp.where(ok, key, inadm)
            c = jnp.maximum(c, _fold(jnp.where(ok, acc[u], -jnp.inf), jnp.maximum))
        k_s = k_ref[0, pl.ds(pl.multiple_of(base * TILE, GROUP * TILE), GROUP * TILE), :]
        for g in range(H_A // 2):
            s = lax.dot_general(k_s[:, g * LANES:(g + 1) * LANES], qh_scr[g], NT_DIMS,
                                preferred_element_type=F32)
            for par in range(2):
                h = 2 * g + par
                mx = m_scr[h]
                for u in range(GROUP):
                    dd = jnp.clip(base + u - qb, -N_FAR, 1) + N_FAR
                    su = (s[u * TILE:(u + 1) * TILE, par * TILE:(par + 1) * TILE]
                          + bias_ref[h * (N_FAR + 2) + dd])
                    s_scr[h, base + u] = su
                    mx = jnp.maximum(mx, _fold(su, jnp.maximum))
                m_scr[h] = mx
        return c

    for h in range(H_A):
        m_scr[h] = jnp.full((8, LANES), NEG_BIG, F32)
    smax8 = lax.fori_loop(0, nst, p1, jnp.full((8, LANES), -jnp.inf, F32))
    fmax = jnp.max(smax8, axis=0, keepdims=True)

    def count_ge(thr_row):
        def body(st, cs):
            return tuple(c + _fold(jnp.where(key_scr[st * GROUP + u] >= thr_row, 1.0, 0.0), jnp.add)
                         for u, c in enumerate(cs))
        cs = lax.fori_loop(0, nst, body, (jnp.zeros((8, LANES), F32),) * GROUP)
        return jnp.sum(functools.reduce(jnp.add, cs), axis=0, keepdims=True)

    def bisect(lo, hi):
        mid = (lo >> 1) + (hi >> 1) + (lo & hi & 1)
        cnt = count_ge(mid)
        ge = cnt >= kf
        hi = jnp.where(cnt == kf, mid + 1, jnp.where(ge, hi, mid))
        return jnp.where(ge, mid, lo), hi

    kmax = jnp.where(fmax == 0.0, 0, score_key(fmax))
    lo_try = score_key(jnp.where(fmax > 0.0, fmax * (1.0 / BRACKET), fmax * BRACKET))
    narrow = count_ge(lo_try) >= kf
    lo0 = jnp.where(narrow, lo_try, INT_MIN)
    hi0 = jnp.where(narrow, jnp.minimum(kmax, INT_MAX - 1) + 1, INT_MAX)
    all_narrow = jnp.min(jnp.where(narrow, 1.0, 0.0)) > 0.0
    n_steps = jnp.where(all_narrow, NARROW_STEPS, 32)
    lo, hi = lax.fori_loop(0, n_steps - LATE_STEPS, lambda _, lh: bisect(*lh), (lo0, hi0))

    def unsettled(lo, hi):
        return jnp.max(jnp.where(hi != lo + 1, 1.0, 0.0)) > 0.0

    def late(carry):
        lo, hi, it, _ = carry
        lo, hi = bisect(*bisect(lo, hi))
        return lo, hi, it + 2, unsettled(lo, hi)

    thr, _, _, _ = lax.while_loop(lambda carry: jnp.logical_and(carry[2] < n_steps, carry[3]),
                                  late, (lo, hi, n_steps - LATE_STEPS, unsettled(lo, hi)))
    cnt_ge = count_ge(thr)
    tie = jnp.logical_and(cnt_ge > kf, thr > inadm)

    @pl.when(jnp.max(jnp.where(tie, 1.0, 0.0)) > 0.0)
    def _():
        need = kf - count_ge(thr + 1)

        def count_eq_le(pmax):
            def body(kt, c):
                hit = jnp.where(key_scr[kt] == thr, krow + kt * TILE, INT_MAX) <= pmax
                return c + _fold(jnp.where(hit, 1.0, 0.0), jnp.add)
            c = lax.fori_loop(0, nst * GROUP, body, jnp.zeros((8, LANES), F32))
            return jnp.sum(c, axis=0, keepdims=True)

        def pbisect(_, lohi):
            plo, phi = lohi
            pmid = (plo + phi) >> 1
            ok = count_eq_le(pmid) >= need
            return jnp.where(ok, plo, pmid), jnp.where(ok, pmid, phi)

        plo0 = jnp.full((1, LANES), -1, jnp.int32)
        phi0 = jnp.full((1, LANES), (1 << pos_bits) - 1, jnp.int32)
        _, pthr = lax.fori_loop(0, pos_bits + 1, pbisect, (plo0, phi0))
        pthr = jnp.where(tie, pthr, INT_MAX)

        def drop(kt, c):
            kk = key_scr[kt]
            pos = jnp.where(kk == thr, krow + kt * TILE, -1)
            key_scr[kt] = jnp.where(pos > pthr, inadm, kk)
            return c

        lax.fori_loop(0, nst * GROUP, drop, 0)

    thr_eff = jnp.maximum(thr, inadm + 1)

    ones_rows = jnp.ones((ONES_ROWS, GROUP * TILE), BF16)

    def collapse_max():
        for h in range(H_A):
            m_scr[h] = jnp.broadcast_to(jnp.max(m_scr[h], axis=0, keepdims=True), (8, LANES))

    def select_bias(base):
        for u in range(GROUP):
            mb_scr[u] = jnp.where(key_scr[base + u] >= thr_eff, 0.0, NEG_BIG)

    def pv(st, c):
        select_bias(st * GROUP)
        for h in range(H_A):
            shift = m_scr[h][0:1, :]
            ps = [jnp.exp((s_scr[h, st * GROUP + u] + mb_scr[u] - shift).astype(BF16))
                  for u in range(GROUP)]
            lhs = jnp.concatenate([vt_ref[st, h * HEAD_DIM:(h + 1) * HEAD_DIM, :], ones_rows], axis=0)
            acc_scr[h] += jnp.dot(lhs, jnp.concatenate(ps, axis=0), preferred_element_type=F32)
        return c

    def run_pv():
        for h in range(H_A):
            acc_scr[h] = jnp.zeros((HEAD_DIM + ONES_ROWS, LANES), F32)
        lax.fori_loop(0, nst, pv, 0)

    collapse_max()
    run_pv()
    den_min = acc_scr[0][HEAD_DIM:HEAD_DIM + 1, :]
    for h in range(1, H_A):
        den_min = jnp.minimum(den_min, acc_scr[h][HEAD_DIM:HEAD_DIM + 1, :])

    @pl.when(jnp.logical_not(jnp.min(den_min) > DEN_FLOOR))
    def _():
        for h in range(H_A):
            m_scr[h] = jnp.full((8, LANES), NEG_BIG, F32)

        def selected_max(st, c):
            select_bias(st * GROUP)
            for h in range(H_A):
                mx = m_scr[h]
                for u in range(GROUP):
                    mx = jnp.maximum(mx, _fold(s_scr[h, st * GROUP + u] + mb_scr[u], jnp.maximum))
                m_scr[h] = mx
            return c

        lax.fori_loop(0, nst, selected_max, 0)
        collapse_max()
        run_pv()

    outs = []
    for h in range(H_A):
        acc = acc_scr[h]
        outs.append(acc[0:HEAD_DIM, :] / acc[HEAD_DIM:HEAD_DIM + 1, :])
    o_ref[0] = jnp.concatenate(outs, axis=0).T.astype(BF16)


def _attention(q, k, vt, qi, ki, wit, bias_tiles, topk):
    b, s, _ = q.shape
    nt = s // TILE
    ng = s // (GROUP * TILE)
    kern = functools.partial(_attn_kernel, topk=topk, pos_bits=max(1, int(math.ceil(math.log2(s)))))
    qtile = lambda bb, i: (bb, i, 0)
    whole = lambda bb, i: (bb, 0, 0)
    once = pl.Buffered(1)
    return pl.pallas_call(
        kern,
        grid=(b, nt),
        in_specs=[pl.BlockSpec((1, TILE, D_A), qtile),
                  pl.BlockSpec((1, s, D_A), whole, pipeline_mode=once),
                  pl.BlockSpec((ng, D_A, GROUP * TILE), whole, pipeline_mode=once),
                  pl.BlockSpec((1, TILE, 512), qtile),
                  pl.BlockSpec((1, s, LANES), whole, pipeline_mode=once),
                  pl.BlockSpec((N_IDX_HEADS, TILE), lambda bb, i: (0, bb * nt + i)),
                  pl.BlockSpec(bias_tiles.shape, lambda bb, i: (0, 0, 0), pipeline_mode=once)],
        out_specs=pl.BlockSpec((1, TILE, D_A), qtile),
        out_shape=jax.ShapeDtypeStruct((b, s, D_A), BF16),
        scratch_shapes=[pltpu.VMEM((nt, TILE, LANES), jnp.int32),
                        pltpu.VMEM((N_IDX_HEADS // 2, 2 * TILE, LANES), BF16),
                        pltpu.VMEM((H_A // 2, 2 * TILE, LANES), BF16),
                        pltpu.VMEM((GROUP, TILE, LANES), F32),
                        pltpu.VMEM((H_A, nt, TILE, LANES), F32),
                        pltpu.VMEM((H_A, 8, LANES), F32),
                        pltpu.VMEM((H_A, HEAD_DIM + ONES_ROWS, LANES), F32)],
        compiler_params=pltpu.CompilerParams(
            dimension_semantics=("arbitrary", "arbitrary"), vmem_limit_bytes=VMEM_LIMIT),
    )(q, k, vt, qi, ki, wit, bias_tiles)


def _t5_bucket(rel):
    nb = N_BUCKETS // 2
    max_exact = nb // 2
    ret = jnp.where(rel > 0, nb, 0)
    n = jnp.abs(rel)
    nf = jnp.maximum(n, 1).astype(F32)
    large = max_exact + (jnp.log(nf / max_exact) / math.log(MAX_DISTANCE / max_exact)
                         * (nb - max_exact)).astype(jnp.int32)
    large = jnp.minimum(large, nb - 1)
    return ret + jnp.where(n < max_exact, n, large)


def _bias_tiles(rel_bias):
    n = 2 * TILE - 1
    m = np.arange(n)
    q_minus_k = np.where(m < TILE, m, m - n)
    rel = (np.arange(N_FAR + 1)[:, None] - N_FAR) * TILE - q_minus_k[None, :]
    u = rel_bias[_t5_bucket(jnp.asarray(rel, jnp.int32))].astype(F32)
    u = u.transpose(2, 0, 1)
    flat = jnp.tile(u, (1, 1, TILE))[:, :, :TILE * (n - 1)]
    tiles = flat.reshape(H_A, N_FAR + 1, TILE, n - 1)[..., :TILE]
    j = np.arange(TILE)[:, None]
    t = np.arange(TILE)[None, :]
    diag = jnp.where(jnp.asarray(j // CHUNK <= t // CHUNK), tiles[:, N_FAR], NEG_BIG)
    tiles = jnp.concatenate([tiles[:, :N_FAR], diag[:, None],
                             jnp.full((H_A, 1, TILE, TILE), NEG_BIG, F32)], axis=1)
    return tiles.reshape(H_A * (N_FAR + 2), TILE, TILE)


CONV_HALO = 32


def _conv_kernel(cur_ref, halo_ref, w_ref, b_ref, g_ref, beta_ref, o_ref, u_scr, sh_scr, *, tt):
    i = pl.program_id(1)

    def glu(t):
        return t[:, :D_B] * _sigmoid(t[:, D_B:])

    u_scr[0:CONV_HALO, :] = jnp.where(i > 0, glu(halo_ref[0]), 0.0)
    u_scr[CONV_HALO:CONV_HALO + tt, :] = glu(cur_ref[0])
    rows = 64
    first = CONV_HALO - (CONV_WIDTH - 1)
    for r0 in range(0, tt, rows):
        acc = jnp.broadcast_to(b_ref[...], (rows, D_B))
        for res in range(8):
            offs = [first + j for j in range(CONV_WIDTH) if (first + j) % 8 == res]
            span = offs[-1] - offs[0] + rows
            sh_scr[0:span, :] = u_scr[r0 + offs[0]:r0 + offs[0] + span, :]
            for off in offs:
                acc = acc + w_ref[off - first:off - first + 1, :] * sh_scr[off - offs[0]:off - offs[0] + rows, :]
        mu = jnp.mean(acc, axis=-1, keepdims=True)
        cen = acc - mu
        var = jnp.mean(cen * cen, axis=-1, keepdims=True)
        y = cen * lax.rsqrt(var + EPS) * g_ref[...] + beta_ref[...]
        o_ref[0, r0:r0 + rows, :] = _silu(y).astype(BF16)


def _conv(conv_in, w, b, g, beta, tt):
    bsz, s, _ = conv_in.shape
    per = tt // CONV_HALO
    vec = lambda bb, i: (0, 0)
    return pl.pallas_call(
        functools.partial(_conv_kernel, tt=tt),
        grid=(bsz, s // tt),
        in_specs=[pl.BlockSpec((1, tt, 2 * D_B), lambda bb, i: (bb, i, 0)),
                  pl.BlockSpec((1, CONV_HALO, 2 * D_B),
                               lambda bb, i: (bb, jnp.maximum(i * per - 1, 0), 0)),
                  pl.BlockSpec((CONV_WIDTH, D_B), vec),
                  pl.BlockSpec((1, D_B), vec),
                  pl.BlockSpec((1, D_B), vec),
                  pl.BlockSpec((1, D_B), vec)],
        out_specs=pl.BlockSpec((1, tt, D_B), lambda bb, i: (bb, i, 0)),
        out_shape=jax.ShapeDtypeStruct((bsz, s, D_B), BF16),
        scratch_shapes=[pltpu.VMEM((CONV_HALO + tt, D_B), F32),
                        pltpu.VMEM((CONV_HALO + 64, D_B), F32)],
        compiler_params=pltpu.CompilerParams(
            dimension_semantics=("arbitrary", "arbitrary"), vmem_limit_bytes=VMEM_LIMIT),
    )(conv_in, conv_in, w, b, g, beta)


def _rwkv_kernel(rw_ref, mu_ref, lora_ref, w0_ref, a0_ref, kk_ref, ka_ref, rk_ref, lg_ref, lb_ref,
                 o_ref, s_scr, prev_scr):
    c = pl.program_id(1)
    n = CHUNK
    nb = RW_CHUNKS * CHUNK

    @pl.when(c == 0)
    def _():
        s_scr[...] = jnp.zeros_like(s_scr)
        prev_scr[...] = jnp.zeros_like(prev_scr)

    ps = rw_ref[0]
    row = lax.broadcasted_iota(jnp.int32, (nb, 1), 0)
    prev = jnp.where(row == 0, prev_scr[0:1, :], pltpu.roll(ps, 1, axis=0))
    prev_scr[0:1, :] = ps[nb - 1:nb, :]
    xs = ps + mu_ref[...] * (prev - ps)
    r = xs[:, 0:D_C]
    k = xs[:, D_C:2 * D_C]
    v = xs[:, 2 * D_C:3 * D_C]
    dn = xs[:, 3 * D_C:3 * D_C + 2 * D_LORA]
    lane64 = lax.broadcasted_iota(jnp.int32, (nb, 2 * D_LORA), 1)
    dn = jnp.where(lane64 < D_LORA, jnp.tanh(dn), dn)
    dn_hi, dn_lo = _split2(dn)
    lora_hi, lora_lo = _split2(lora_ref[...])
    pre = _bdot(dn_hi, lora_hi) + _bdot(dn_hi, lora_lo) + _bdot(dn_lo, lora_hi)
    z = -(w0_ref[...] + pre[:, :D_C])
    softplus = jnp.maximum(z, 0.0) + jnp.log(1.0 + jnp.exp(-jnp.abs(z)))
    logdec = -jnp.exp(-softplus - 0.5)
    a = _sigmoid(a0_ref[...] + pre[:, D_C:])
    kkraw = k * kk_ref[...]
    k2 = k * (1.0 + (a - 1.0) * ka_ref[...])
    bonus_pre = r * k2 * rk_ref[...]

    rb = lax.broadcasted_iota(jnp.int32, (nb, nb), 0)
    cb = lax.broadcasted_iota(jnp.int32, (nb, nb), 1)
    tri = jnp.where(jnp.logical_and(rb // n == cb // n, cb <= rb), 1.0, 0.0)
    ld_hi, ld_rest = _split2(logdec)
    ld_mid, ld_lo = _split2(ld_rest)
    cum_all = _bdot(tri, ld_hi) + _bdot(tri, ld_mid) + _bdot(tri, ld_lo)
    p_in_all = jnp.exp(cum_all)
    p_inv_all = jnp.exp(-cum_all)
    p_ex_all = jnp.exp(cum_all - logdec)

    row_n = lax.broadcasted_iota(jnp.int32, (n, LANES), 0)
    lane_n = lax.broadcasted_iota(jnp.int32, (n, LANES), 1)
    lo = lane_n < HEAD_DIM
    col = lane_n & (HEAD_DIM - 1)
    incl = col <= row_n
    strict = col < row_n
    mask2 = jnp.concatenate([strict, incl], axis=0)
    r2 = lax.broadcasted_iota(jnp.int32, (2 * n, LANES), 0)
    l2 = lax.broadcasted_iota(jnp.int32, (2 * n, LANES), 1)
    same_head = (r2 // HEAD_DIM) == (l2 // HEAD_DIM)
    ones_bd = jnp.where(same_head, 1.0, 0.0).astype(BF16)

    def bd(t):
        tb = t.astype(BF16)
        z = jnp.zeros_like(tb)
        return jnp.concatenate([jnp.where(lo, tb, z), jnp.where(lo, z, tb)], axis=0)

    def hsum(t):
        return jnp.dot(t.astype(BF16), ones_bd, preferred_element_type=F32)

    pairs = range(H_C // 2)
    chains = [(j, g) for j in range(RW_CHUNKS) for g in pairs]
    ids = range(len(chains))
    sl = lambda t, i: t[chains[i][0] * n:(chains[i][0] + 1) * n,
                        chains[i][1] * LANES:(chains[i][1] + 1) * LANES]
    v_p = [sl(v, i) for i in ids]
    p_in = [sl(p_in_all, i) for i in ids]
    kkn = []
    for i in ids:
        kr = sl(kkraw, i)
        kkn.append(kr * lax.rsqrt(jnp.maximum(hsum(kr * kr), 1e-12)))
    rt = [sl(r, i) * p_in[i] for i in ids]
    at = [kkn[i] * sl(p_ex_all, i) for i in ids]
    kt = [sl(k2, i) * sl(p_inv_all, i) for i in ids]
    bt = [kkn[i] * sl(a, i) * sl(p_inv_all, i) for i in ids]
    lhs = [jnp.concatenate([at[i], rt[i]], axis=0).astype(BF16) for i in ids]
    ak = [lax.dot_general(lhs[i], bd(kt[i]), NT_DIMS, preferred_element_type=F32) for i in ids]
    ab = [lax.dot_general(lhs[i], bd(bt[i]), NT_DIMS, preferred_element_type=F32) for i in ids]
    av = [_bdot(jnp.where(mask2, ak[i], 0.0), bd(v_p[i])) for i in ids]
    a_rb = [jnp.where(incl, ab[i][n:], 0.0) for i in ids]
    npow = [jnp.where(strict, -ab[i][:n], 0.0) for i in ids]
    tm = list(npow)
    for _ in range(5):
        both = [_bdot(jnp.concatenate([npow[i], tm[i]], axis=0), bd(npow[i])) for i in ids]
        npow = [both[i][:n] for i in ids]
        tm = [tm[i] + both[i][n:] for i in ids]
    tm = [tm[i] + _bdot(tm[i], bd(npow[i])) for i in ids]
    wt = [at[i] + _bdot(tm[i], bd(at[i])) for i in ids]
    uloc = [av[i][:n] + _bdot(tm[i], bd(av[i][:n])) for i in ids]
    mcat = [jnp.concatenate([kt[i], bt[i]], axis=0).astype(BF16) for i in ids]
    bonus = [hsum(sl(bonus_pre, i)) * v_p[i] for i in ids]

    state = [s_scr[g] for g in pairs]
    for j in range(RW_CHUNKS):
        cid = [j * (H_C // 2) + g for g in pairs]
        ws = [lax.dot_general(jnp.concatenate([wt[i], rt[i]], axis=0).astype(BF16),
                              state[g].astype(BF16), NT_DIMS, preferred_element_type=F32)
              for g, i in zip(pairs, cid)]
        u = [ws[g][:n] + uloc[i] for g, i in zip(pairs, cid)]
        o = [ws[g][n:] + av[i][n:] - _bdot(a_rb[i], bd(u[g])) for g, i in zip(pairs, cid)]
        for g, i in zip(pairs, cid):
            ncat = jnp.concatenate([v_p[i], -u[g]], axis=0).astype(BF16)
            upd = lax.dot_general(ncat, mcat[i], TN_DIMS, preferred_element_type=F32)
            state[g] = (state[g] + jnp.where(same_head, upd, 0.0)) * p_in[i][n - 1:n, :]
        for g, i in zip(pairs, cid):
            lanes = slice(g * LANES, (g + 1) * LANES)
            cen = o[g] - hsum(o[g]) * (1.0 / HEAD_DIM)
            var = hsum(cen * cen) * (1.0 / HEAD_DIM)
            y = cen * lax.rsqrt(var + LNX_EPS) * lg_ref[:, lanes] + lb_ref[:, lanes]
            o_ref[0, j * n:(j + 1) * n, lanes] = (y + bonus[i]).astype(BF16)
    for g in pairs:
        s_scr[g] = state[g]


def _rwkv(rw, mu, lora, w0, a0, kk, ka, rk, lg, lb):
    bsz, s, _ = rw.shape
    vec = lambda bb, i: (0, 0)
    return pl.pallas_call(
        _rwkv_kernel,
        grid=(bsz, s // (RW_CHUNKS * CHUNK)),
        in_specs=[pl.BlockSpec((1, RW_CHUNKS * CHUNK, RW_COLS), lambda bb, i: (bb, i, 0)),
                  pl.BlockSpec((1, RW_COLS), vec),
                  pl.BlockSpec((2 * D_LORA, 2 * D_C), vec)] +
                 [pl.BlockSpec((1, D_C), vec)] * 7,
        out_specs=pl.BlockSpec((1, RW_CHUNKS * CHUNK, D_C), lambda bb, i: (bb, i, 0)),
        out_shape=jax.ShapeDtypeStruct((bsz, s, D_C), BF16),
        scratch_shapes=[pltpu.VMEM((H_C // 2, 2 * HEAD_DIM, LANES), F32),
                        pltpu.VMEM((8, RW_COLS), F32)],
        compiler_params=pltpu.CompilerParams(
            dimension_semantics=("arbitrary", "arbitrary"), vmem_limit_bytes=VMEM_LIMIT),
    )(rw, mu, lora, w0, a0, kk, ka, rk, lg, lb)


def _outproj_kernel(x_ref, attn_ref, conv_ref, rw_ref, gates_ref, gm_ref, w_ref, o_ref):
    sg = _silu(gates_ref[...].astype(F32))
    ya = (attn_ref[...].astype(F32) * sg[:, 0:D_A]).astype(BF16)
    yb = (conv_ref[...].astype(F32) * sg[:, D_A:D_A + D_B]).astype(BF16)
    yc = (rw_ref[...].astype(F32) * sg[:, D_A + D_B:D_MIX]).astype(BF16)
    y = (jnp.dot(ya, w_ref[0:D_A, :], preferred_element_type=F32)
         + jnp.dot(yb, w_ref[D_A:D_A + D_B, :], preferred_element_type=F32)
         + jnp.dot(yc, w_ref[D_A + D_B:D_MIX, :], preferred_element_type=F32))
    o_ref[...] = x_ref[...] + gm_ref[0] * y


def _outproj(x2, attn, conv, rw, gates, gate_mod, w, seq, tm):
    n, d = x2.shape
    per_b = seq // tm
    row = lambda i: (i, 0)
    return pl.pallas_call(
        _outproj_kernel,
        grid=(n // tm,),
        in_specs=[pl.BlockSpec((tm, d), row),
                  pl.BlockSpec((tm, D_A), row),
                  pl.BlockSpec((tm, D_B), row),
                  pl.BlockSpec((tm, D_C), row),
                  pl.BlockSpec((tm, D_MIX), row),
                  pl.BlockSpec((1, 1, d), lambda i: (i // per_b, 0, 0)),
                  pl.BlockSpec((D_MIX, d), lambda i: (0, 0))],
        out_specs=pl.BlockSpec((tm, d), row),
        out_shape=jax.ShapeDtypeStruct((n, d), F32),
        compiler_params=pltpu.CompilerParams(
            dimension_semantics=("arbitrary",), vmem_limit_bytes=VMEM_LIMIT),
    )(x2, attn, conv, rw, gates, gate_mod, w)


def _reorder_w_in(w_in):
    depth, d, _ = w_in.shape
    a_cols = 4 * D_A + N_IDX_HEADS * D_IDX + D_IDX + N_IDX_HEADS
    b0 = a_cols
    c0 = a_cols + 3 * D_B
    c_shift = 3 * D_C + 2 * D_LORA
    zeros = lambda m: jnp.zeros((depth, d, m), w_in.dtype)
    qi0 = 4 * D_A
    ki0 = qi0 + N_IDX_HEADS * D_IDX
    wi0 = ki0 + D_IDX
    parts = [w_in[..., 0:3 * D_A],
             w_in[..., qi0:ki0],
             w_in[..., ki0:wi0], w_in[..., ki0:wi0],
             w_in[..., wi0:a_cols], zeros(LANES - N_IDX_HEADS),
             w_in[..., 3 * D_A:4 * D_A], w_in[..., b0 + 2 * D_B:b0 + 3 * D_B],
             w_in[..., c0 + c_shift:c0 + c_shift + D_C],
             w_in[..., b0:b0 + 2 * D_B],
             w_in[..., c0:c0 + c_shift], zeros(RW_COLS - c_shift)]
    w = jnp.concatenate(parts, axis=-1)
    assert w.shape[-1] == N_COLS
    return w.astype(BF16)


def kernel(x, c, norm_g, w_ada, b_ada, w_in, w_out, q_norm_g, k_norm_g, rel_bias, conv_w, conv_b,
           conv_ln_g, conv_ln_b, shift_mu, decay_w0, decay_up, iclr_a0, iclr_up, key_k, key_a,
           bonus_r_k, lnx_g, lnx_b):
    bsz, seq, d = x.shape
    depth = w_in.shape[0]
    assert seq % (GROUP * TILE) == 0 and d == w_out.shape[-1]
    topk = min(TOPK_MAX, seq // 4)
    tm = GROUP * TILE
    conv_tt = 256

    mod = _modulation(c, w_ada, b_ada)
    shift = mod[:, :, None, 0:d]
    sc1p = 1.0 + mod[:, :, None, d:2 * d]
    gate_mod = mod[:, :, None, 2 * d:3 * d]

    w_all = _reorder_w_in(w_in)
    head = jnp.arange(D_A) // HEAD_DIM
    avg = (head[:, None] == head[None, :]).astype(BF16) * (1.0 / HEAD_DIM)
    qg = jnp.tile(q_norm_g, (1, H_A))[:, None, :] * (HEAD_DIM ** -0.5)
    kg = jnp.tile(k_norm_g, (1, H_A))[:, None, :]
    bias_tiles = _bias_tiles(rel_bias)
    c_shift = 3 * D_C + 2 * D_LORA
    mu_pad = jnp.pad(shift_mu, ((0, 0), (0, RW_COLS - c_shift)))[:, None, :]
    zl = jnp.zeros((depth, D_LORA, D_C), F32)
    lora = jnp.concatenate([jnp.concatenate([decay_up, zl], axis=2),
                            jnp.concatenate([zl, iclr_up], axis=2)], axis=1)
    row = lambda t: t[:, None, :]
    layers = dict(
        shift=shift, sc1p=sc1p, gate_mod=gate_mod, norm_g=row(norm_g), w=w_all, qg=qg, kg=kg,
        w_out=w_out.astype(BF16), conv_w=conv_w, conv_b=row(conv_b), conv_g=row(conv_ln_g),
        conv_beta=row(conv_ln_b), mu=mu_pad, lora=lora, w0=row(decay_w0), a0=row(iclr_a0),
        kk=row(key_k), ka=row(key_a), rk=row(bonus_r_k.reshape(depth, D_C)), lg=row(lnx_g),
        lb=row(lnx_b))

    def layer(x2, p):
        q, k, vt, qi, ki, wit, gates, conv_in, rw_in = _inproj(
            x2, p["sc1p"], p["shift"], p["norm_g"], p["w"], avg, p["qg"], p["kg"], seq, tm)
        r3 = lambda t: t.reshape(bsz, seq, t.shape[-1])
        attn = _attention(r3(q), r3(k), vt, r3(qi), r3(ki), wit, bias_tiles, topk)
        conv = _conv(r3(conv_in), p["conv_w"], p["conv_b"], p["conv_g"], p["conv_beta"], conv_tt)
        rwo = _rwkv(r3(rw_in), p["mu"], p["lora"], p["w0"], p["a0"], p["kk"], p["ka"], p["rk"],
                    p["lg"], p["lb"])
        x2 = _outproj(x2, attn.reshape(-1, D_A), conv.reshape(-1, D_B), rwo.reshape(-1, D_C),
                      gates, p["gate_mod"], p["w_out"], seq, tm)
        return x2

    x2 = x.reshape(bsz * seq, d)
    for l in range(depth):
        x2 = layer(x2, {name: t[l] for name, t in layers.items()})
    return x2.reshape(bsz, seq, d)
```

```python
import functools
import math

import jax
import jax.numpy as jnp
import numpy as np
from jax import lax
from jax.experimental import pallas as pl
from jax.experimental.pallas import tpu as pltpu

F32 = jnp.float32
BF16 = jnp.bfloat16

HEAD_DIM = 64
CHUNK = 64
EPS = 1e-6
H_A = 6
D_A = H_A * HEAD_DIM
N_IDX_HEADS = 8
D_IDX = 64
TOPK_MAX = 256
N_BUCKETS = 32
MAX_DISTANCE = 1024
D_B = 256
CONV_WIDTH = 31
H_C = 6
D_C = H_C * HEAD_DIM
D_LORA = 32
LNX_EPS = 64e-5
D_MIX = D_A + D_B + D_C

LANES = 128
TILE = 128
GROUP = 4
BRACKET = 16.0
NARROW_STEPS = 26
DEN_FLOOR = 1e-30
LATE_STEPS = 14
ONES_ROWS = 16
RW_CHUNKS = 4
QW = 256
N_FAR = 9
N_BIAS = N_FAR + QW // TILE + 1
ATTN_VMEM_LIMIT = 56 * 1024 * 1024
VMEM_LIMIT = 48 * 1024 * 1024

C_Q, C_K, C_V = 0, 384, 768
C_QI = 1152
C_KI = 1664
C_WI = 1792
C_GATES = 1920
C_CONV = 2944
C_RW = 3456
N_COLS = 4736
RW_COLS = 1280

INT_MIN = -(2 ** 31)
INT_MAX = 2 ** 31 - 1
NEG_KEY = int(np.array(-np.inf, np.float32).view(np.int32)) ^ 0x7FFFFFFF
NEG_BIG = -1e30

NT_DIMS = (((1,), (1,)), ((), ()))
TN_DIMS = (((0,), (0,)), ((), ()))


def _bdot(a, b):
    return jnp.dot(a.astype(BF16), b.astype(BF16), preferred_element_type=F32)


def _bdot_nt(a, b):
    return lax.dot_general(a.astype(BF16), b.astype(BF16), NT_DIMS, preferred_element_type=F32)


def _split2(x):
    hi = x.astype(BF16).astype(F32)
    return hi, x - hi


def _sigmoid(x):
    return 1.0 / (1.0 + jnp.exp(-x))


def _silu(x):
    return x * _sigmoid(x)


def _mod_kernel(c_ref, w_ref, b_ref, o_ref):
    ca = _silu(c_ref[...])
    o_ref[0] = jnp.dot(ca, w_ref[0], precision=lax.Precision.HIGHEST,
                       preferred_element_type=F32) + b_ref[0]


def _modulation(c, w_ada, b_ada):
    depth, d, d3 = w_ada.shape
    b = c.shape[0]
    return pl.pallas_call(
        _mod_kernel,
        grid=(depth, d3 // d),
        in_specs=[pl.BlockSpec((b, d), lambda l, j: (0, 0)),
                  pl.BlockSpec((1, d, d), lambda l, j: (l, 0, j)),
                  pl.BlockSpec((1, 1, d), lambda l, j: (l, 0, j))],
        out_specs=pl.BlockSpec((1, b, d), lambda l, j: (l, 0, j)),
        out_shape=jax.ShapeDtypeStruct((depth, b, d3), F32),
        compiler_params=pltpu.CompilerParams(
            dimension_semantics=("arbitrary", "arbitrary"), vmem_limit_bytes=VMEM_LIMIT),
    )(c, w_ada, b_ada.reshape(depth, 1, d3))


def _inproj_kernel(x_ref, sc_ref, sh_ref, g_ref, w_ref, avg_ref, qg_ref, kg_ref,
                   q_ref, k_ref, vt_ref, qi_ref, ki_ref, wit_ref, gates_ref, conv_ref, rw_ref):
    x = x_ref[...]
    ms = jnp.mean(x * x, axis=-1, keepdims=True)
    h = x * lax.rsqrt(ms + EPS) * g_ref[...]
    hb = (h * sc_ref[0] + sh_ref[0]).astype(BF16)

    def mm(c0, c1):
        return jnp.dot(hb, w_ref[:, c0:c1], preferred_element_type=F32)

    def head_rms(t, g):
        m2 = jnp.dot((t * t).astype(BF16), avg_ref[...], preferred_element_type=F32)
        return t * lax.rsqrt(m2 + EPS) * g

    q_ref[...] = head_rms(mm(C_Q, C_Q + D_A), qg_ref[...]).astype(BF16)
    k_ref[...] = head_rms(mm(C_K, C_K + D_A), kg_ref[...]).astype(BF16)
    vt_ref[0] = mm(C_V, C_V + D_A).T.astype(BF16)
    qi_ref[...] = mm(C_QI, C_QI + 512).astype(BF16)
    ki_ref[...] = mm(C_KI, C_KI + LANES).astype(BF16)
    wit_ref[...] = mm(C_WI, C_WI + LANES).T[:N_IDX_HEADS, :]
    gates_ref[:, 0:512] = mm(C_GATES, C_GATES + 512).astype(BF16)
    gates_ref[:, 512:1024] = mm(C_GATES + 512, C_GATES + 1024).astype(BF16)
    conv_ref[...] = mm(C_CONV, C_CONV + 512)
    rw_ref[:, 0:512] = mm(C_RW, C_RW + 512)
    rw_ref[:, 512:1024] = mm(C_RW + 512, C_RW + 1024)
    rw_ref[:, 1024:RW_COLS] = mm(C_RW + 1024, C_RW + RW_COLS)


def _inproj(x2, sc1p, shift, norm_g, w, avg, qg, kg, seq, tm):
    n, d = x2.shape
    per_b = seq // tm
    row = lambda i: (i, 0)
    full = lambda i: (0, 0)
    bat = lambda i: (i // per_b, 0, 0)
    rows = lambda wd, dt: (pl.BlockSpec((tm, wd), row), jax.ShapeDtypeStruct((n, wd), dt))
    outs = [rows(D_A, BF16), rows(D_A, BF16),
            (pl.BlockSpec((1, D_A, tm), lambda i: (i, 0, 0)),
             jax.ShapeDtypeStruct((n // tm, D_A, tm), BF16)),
            rows(512, BF16), rows(LANES, BF16),
            (pl.BlockSpec((N_IDX_HEADS, tm), lambda i: (0, i)),
             jax.ShapeDtypeStruct((N_IDX_HEADS, n), F32)),
            rows(D_MIX, BF16), rows(512, F32), rows(RW_COLS, F32)]
    return pl.pallas_call(
        _inproj_kernel,
        grid=(n // tm,),
        in_specs=[pl.BlockSpec((tm, d), row),
                  pl.BlockSpec((1, 1, d), bat),
                  pl.BlockSpec((1, 1, d), bat),
                  pl.BlockSpec((1, d), full),
                  pl.BlockSpec((d, N_COLS), full),
                  pl.BlockSpec((D_A, D_A), full),
                  pl.BlockSpec((1, D_A), full),
                  pl.BlockSpec((1, D_A), full)],
        out_specs=[o[0] for o in outs],
        out_shape=[o[1] for o in outs],
        compiler_params=pltpu.CompilerParams(
            dimension_semantics=("arbitrary",), vmem_limit_bytes=VMEM_LIMIT),
    )(x2, sc1p, shift, norm_g, w, avg, qg, kg)


def _fold(t, op):
    parts = [t[8 * i:8 * i + 8, :] for i in range(TILE // 8)]
    while len(parts) > 1:
        parts = [op(parts[i], parts[i + 1]) for i in range(0, len(parts), 2)]
    return parts[0]


def _attn_kernel(q_ref, k_ref, vt_ref, qi_ref, ki_ref, wit_ref, bias_ref, o_ref,
                 key_scr, qih_scr, qh_scr, mb_scr, s_scr, m_scr, acc_scr,
                 *, topk, pos_bits):
    qb = pl.program_id(1)
    nst = ((qb + 1) * (QW // TILE) - 1) // GROUP + 1
    kf = float(topk)
    zgap = 1 << pos_bits
    lane = lax.broadcasted_iota(jnp.int32, (TILE, QW), 1)
    krow = lax.broadcasted_iota(jnp.int32, (TILE, QW), 0)
    lo_half = lax.broadcasted_iota(jnp.int32, (QW, LANES), 1) < HEAD_DIM

    wit = wit_ref[...]
    qi = qi_ref[0]
    q = q_ref[0]
    for src, dst, n_pairs in ((qi, qih_scr, N_IDX_HEADS // 2), (q, qh_scr, H_A // 2)):
        for g in range(n_pairs):
            pair = src[:, g * LANES:(g + 1) * LANES]
            dst[g, 0:QW, :] = jnp.where(lo_half, pair, jnp.zeros_like(pair))
            dst[g, QW:2 * QW, :] = jnp.where(lo_half, jnp.zeros_like(pair), pair)

    key_chunk = krow // CHUNK
    query_chunk = qb * (QW // CHUNK) + lane // CHUNK
    inadm = NEG_KEY - zgap - 1

    def score_key(s):
        b = lax.bitcast_convert_type(s, jnp.int32)
        return jnp.where(b < 0, (b ^ jnp.int32(0x7FFFFFFF)) - zgap, b)

    def p1(st, c):
        base = st * GROUP
        kis = ki_ref[0, pl.ds(pl.multiple_of(base * TILE, GROUP * TILE), GROUP * TILE), :]
        acc = [None] * GROUP
        for g in range(N_IDX_HEADS // 2):
            d = lax.dot_general(kis, qih_scr[g], NT_DIMS, preferred_element_type=F32)
            for par in range(2):
                w_row = wit[2 * g + par:2 * g + par + 1, :]
                for u in range(GROUP):
                    t = jnp.maximum(d[u * TILE:(u + 1) * TILE, par * QW:(par + 1) * QW], 0.0) * w_row
                    acc[u] = t if acc[u] is None else acc[u] + t
        for u in range(GROUP):
            kt = base + u
            key = jnp.where(acc[u] == 0.0, -1 - (krow + kt * TILE), score_key(acc[u]))
            ok = key_chunk + kt * (TILE // CHUNK) <= query_chunk
            key_scr[kt] = jnp.where(ok, key, inadm)
            c = jnp.maximum(c, _fold(jnp.where(ok, acc[u], -jnp.inf), jnp.maximum))
        k_s = k_ref[0, pl.ds(pl.multiple_of(base * TILE, GROUP * TILE), GROUP * TILE), :]
        for g in range(H_A // 2):
            s = lax.dot_general(k_s[:, g * LANES:(g + 1) * LANES], qh_scr[g], NT_DIMS,
                                preferred_element_type=F32)
            for par in range(2):
                h = 2 * g + par
                mx = m_scr[h]
                for u in range(GROUP):
                    dd = jnp.clip(base + u - qb * (QW // TILE), -N_FAR, QW // TILE) + N_FAR
                    su = (s[u * TILE:(u + 1) * TILE, par * QW:(par + 1) * QW]
                          + bias_ref[h * N_BIAS + dd])
                    s_scr[h, base + u] = su
                    mx = jnp.maximum(mx, _fold(su, jnp.maximum))
                m_scr[h] = mx
        return c

    for h in range(H_A):
        m_scr[h] = jnp.full((8, QW), NEG_BIG, F32)
    smax8 = lax.fori_loop(0, nst, p1, jnp.full((8, QW), -jnp.inf, F32))
    fmax = jnp.max(smax8, axis=0, keepdims=True)

    def count_ge(thr_row):
        def body(st, cs):
            return tuple(c + _fold(jnp.where(key_scr[st * GROUP + u] >= thr_row, 1.0, 0.0), jnp.add)
                         for u, c in enumerate(cs))
        cs = lax.fori_loop(0, nst, body, (jnp.zeros((8, QW), F32),) * GROUP)
        return jnp.sum(functools.reduce(jnp.add, cs), axis=0, keepdims=True)

    def bisect(lo, hi):
        mid = (lo >> 1) + (hi >> 1) + (lo & hi & 1)
        cnt = count_ge(mid)
        ge = cnt >= kf
        hi = jnp.where(cnt == kf, mid + 1, jnp.where(ge, hi, mid))
        return jnp.where(ge, mid, lo), hi

    kmax = jnp.where(fmax == 0.0, 0, score_key(fmax))
    lo_try = score_key(jnp.where(fmax > 0.0, fmax * (1.0 / BRACKET), fmax * BRACKET))
    narrow = count_ge(lo_try) >= kf
    lo0 = jnp.where(narrow, lo_try, INT_MIN)
    hi0 = jnp.where(narrow, jnp.minimum(kmax, INT_MAX - 1) + 1, INT_MAX)
    all_narrow = jnp.min(jnp.where(narrow, 1.0, 0.0)) > 0.0
    n_steps = jnp.where(all_narrow, NARROW_STEPS, 32)
    lo, hi = lax.fori_loop(0, n_steps - LATE_STEPS, lambda _, lh: bisect(*lh), (lo0, hi0))

    def unsettled(lo, hi):
        return jnp.max(jnp.where(hi != lo + 1, 1.0, 0.0)) > 0.0

    def late(carry):
        lo, hi, it, _ = carry
        lo, hi = bisect(*bisect(lo, hi))
        return lo, hi, it + 2, unsettled(lo, hi)

    thr, _, _, _ = lax.while_loop(lambda carry: jnp.logical_and(carry[2] < n_steps, carry[3]),
                                  late, (lo, hi, n_steps - LATE_STEPS, unsettled(lo, hi)))
    cnt_ge = count_ge(thr)
    tie = jnp.logical_and(cnt_ge > kf, thr > inadm)

    @pl.when(jnp.max(jnp.where(tie, 1.0, 0.0)) > 0.0)
    def _():
        need = kf - count_ge(thr + 1)

        def count_eq_le(pmax):
            def body(kt, c):
                hit = jnp.where(key_scr[kt] == thr, krow + kt * TILE, INT_MAX) <= pmax
                return c + _fold(jnp.where(hit, 1.0, 0.0), jnp.add)
            c = lax.fori_loop(0, nst * GROUP, body, jnp.zeros((8, QW), F32))
            return jnp.sum(c, axis=0, keepdims=True)

        def pbisect(_, lohi):
            plo, phi = lohi
            pmid = (plo + phi) >> 1
            ok = count_eq_le(pmid) >= need
            return jnp.where(ok, plo, pmid), jnp.where(ok, pmid, phi)

        plo0 = jnp.full((1, QW), -1, jnp.int32)
        phi0 = jnp.full((1, QW), (1 << pos_bits) - 1, jnp.int32)
        _, pthr = lax.fori_loop(0, pos_bits + 1, pbisect, (plo0, phi0))
        pthr = jnp.where(tie, pthr, INT_MAX)

        def drop(kt, c):
            kk = key_scr[kt]
            pos = jnp.where(kk == thr, krow + kt * TILE, -1)
            key_scr[kt] = jnp.where(pos > pthr, inadm, kk)
            return c

        lax.fori_loop(0, nst * GROUP, drop, 0)

    thr_eff = jnp.maximum(thr, inadm + 1)

    ones_rows = jnp.ones((ONES_ROWS, GROUP * TILE), BF16)

    def collapse_max():
        for h in range(H_A):
            m_scr[h] = jnp.broadcast_to(jnp.max(m_scr[h], axis=0, keepdims=True), (8, QW))

    def select_bias(base):
        for u in range(GROUP):
            mb_scr[u] = jnp.where(key_scr[base + u] >= thr_eff, 0.0, NEG_BIG)

    def pv(st, c):
        select_bias(st * GROUP)
        for h in range(H_A):
            shift = m_scr[h][0:1, :]
            ps = [jnp.exp((s_scr[h, st * GROUP + u] + mb_scr[u] - shift).astype(BF16))
                  for u in range(GROUP)]
            lhs = jnp.concatenate([vt_ref[st, h * HEAD_DIM:(h + 1) * HEAD_DIM, :], ones_rows], axis=0)
            acc_scr[h] += jnp.dot(lhs, jnp.concatenate(ps, axis=0), preferred_element_type=F32)
        return c

    def run_pv():
        for h in range(H_A):
            acc_scr[h] = jnp.zeros((HEAD_DIM + ONES_ROWS, QW), F32)
        lax.fori_loop(0, nst, pv, 0)

    collapse_max()
    run_pv()
    den_min = acc_scr[0][HEAD_DIM:HEAD_DIM + 1, :]
    for h in range(1, H_A):
        den_min = jnp.minimum(den_min, acc_scr[h][HEAD_DIM:HEAD_DIM + 1, :])

    @pl.when(jnp.logical_not(jnp.min(den_min) > DEN_FLOOR))
    def _():
        for h in range(H_A):
            m_scr[h] = jnp.full((8, QW), NEG_BIG, F32)

        def selected_max(st, c):
            select_bias(st * GROUP)
            for h in range(H_A):
                mx = m_scr[h]
                for u in range(GROUP):
                    mx = jnp.maximum(mx, _fold(s_scr[h, st * GROUP + u] + mb_scr[u], jnp.maximum))
                m_scr[h] = mx
            return c

        lax.fori_loop(0, nst, selected_max, 0)
        collapse_max()
        run_pv()

    outs = []
    for h in range(H_A):
        acc = acc_scr[h]
        outs.append(acc[0:HEAD_DIM, :] / acc[HEAD_DIM:HEAD_DIM + 1, :])
    o_ref[0] = jnp.concatenate(outs, axis=0).T.astype(BF16)


def _attention(q, k, vt, qi, ki, wit, bias_tiles, topk):
    b, s, _ = q.shape
    nt = s // TILE
    nq = s // QW
    ng = s // (GROUP * TILE)
    kern = functools.partial(_attn_kernel, topk=topk, pos_bits=max(1, int(math.ceil(math.log2(s)))))
    qtile = lambda bb, i: (bb, i, 0)
    whole = lambda bb, i: (bb, 0, 0)
    once = pl.Buffered(1)
    return pl.pallas_call(
        kern,
        grid=(b, nq),
        in_specs=[pl.BlockSpec((1, QW, D_A), qtile),
                  pl.BlockSpec((1, s, D_A), whole, pipeline_mode=once),
                  pl.BlockSpec((ng, D_A, GROUP * TILE), whole, pipeline_mode=once),
                  pl.BlockSpec((1, QW, 512), qtile),
                  pl.BlockSpec((1, s, LANES), whole, pipeline_mode=once),
                  pl.BlockSpec((N_IDX_HEADS, QW), lambda bb, i: (0, bb * nq + i)),
                  pl.BlockSpec(bias_tiles.shape, lambda bb, i: (0, 0, 0), pipeline_mode=once)],
        out_specs=pl.BlockSpec((1, QW, D_A), qtile),
        out_shape=jax.ShapeDtypeStruct((b, s, D_A), BF16),
        scratch_shapes=[pltpu.VMEM((nt, TILE, QW), jnp.int32),
                        pltpu.VMEM((N_IDX_HEADS // 2, 2 * QW, LANES), BF16),
                        pltpu.VMEM((H_A // 2, 2 * QW, LANES), BF16),
                        pltpu.VMEM((GROUP, TILE, QW), F32),
                        pltpu.VMEM((H_A, nt, TILE, QW), F32),
                        pltpu.VMEM((H_A, 8, QW), F32),
                        pltpu.VMEM((H_A, HEAD_DIM + ONES_ROWS, QW), F32)],
        compiler_params=pltpu.CompilerParams(
            dimension_semantics=("arbitrary", "arbitrary"), vmem_limit_bytes=ATTN_VMEM_LIMIT),
    )(q, k, vt, qi, ki, wit, bias_tiles)


def _t5_bucket(rel):
    nb = N_BUCKETS // 2
    max_exact = nb // 2
    ret = jnp.where(rel > 0, nb, 0)
    n = jnp.abs(rel)
    nf = jnp.maximum(n, 1).astype(F32)
    large = max_exact + (jnp.log(nf / max_exact) / math.log(MAX_DISTANCE / max_exact)
                         * (nb - max_exact)).astype(jnp.int32)
    large = jnp.minimum(large, nb - 1)
    return ret + jnp.where(n < max_exact, n, large)


def _bias_tiles(rel_bias):
    n = TILE + QW - 1
    n_off = N_BIAS - 1
    m = np.arange(n)
    q_minus_k = np.where(m < QW, m, m - n)
    first_key = (np.arange(n_off)[:, None] - N_FAR) * TILE
    rel = first_key - q_minus_k[None, :]
    u = rel_bias[_t5_bucket(jnp.asarray(rel, jnp.int32))].astype(F32)
    u = u.transpose(2, 0, 1)
    flat = jnp.tile(u, (1, 1, TILE))[:, :, :TILE * (n - 1)]
    tiles = flat.reshape(H_A, n_off, TILE, n - 1)[..., :QW]
    j = np.arange(TILE)[None, :, None]
    t = np.arange(QW)[None, None, :]
    admissible = (first_key[:, :, None] + j) // CHUNK <= t // CHUNK
    tiles = jnp.where(jnp.asarray(admissible)[None], tiles, NEG_BIG)
    tiles = jnp.concatenate([tiles, jnp.full((H_A, 1, TILE, QW), NEG_BIG, F32)], axis=1)
    return tiles.reshape(H_A * N_BIAS, TILE, QW)


CONV_HALO = 32


def _conv_kernel(cur_ref, halo_ref, w_ref, b_ref, g_ref, beta_ref, o_ref, u_scr, sh_scr, *, tt):
    i = pl.program_id(1)

    def glu(t):
        return t[:, :D_B] * _sigmoid(t[:, D_B:])

    u_scr[0:CONV_HALO, :] = jnp.where(i > 0, glu(halo_ref[0]), 0.0)
    u_scr[CONV_HALO:CONV_HALO + tt, :] = glu(cur_ref[0])
    rows = 64
    first = CONV_HALO - (CONV_WIDTH - 1)
    for r0 in range(0, tt, rows):
        acc = jnp.broadcast_to(b_ref[...], (rows, D_B))
        for res in range(8):
            offs = [first + j for j in range(CONV_WIDTH) if (first + j) % 8 == res]
            span = offs[-1] - offs[0] + rows
            sh_scr[0:span, :] = u_scr[r0 + offs[0]:r0 + offs[0] + span, :]
            for off in offs:
                acc = acc + w_ref[off - first:off - first + 1, :] * sh_scr[off - offs[0]:off - offs[0] + rows, :]
        mu = jnp.mean(acc, axis=-1, keepdims=True)
        cen = acc - mu
        var = jnp.mean(cen * cen, axis=-1, keepdims=True)
        y = cen * lax.rsqrt(var + EPS) * g_ref[...] + beta_ref[...]
        o_ref[0, r0:r0 + rows, :] = _silu(y).astype(BF16)


def _conv(conv_in, w, b, g, beta, tt):
    bsz, s, _ = conv_in.shape
    per = tt // CONV_HALO
    vec = lambda bb, i: (0, 0)
    return pl.pallas_call(
        functools.partial(_conv_kernel, tt=tt),
        grid=(bsz, s // tt),
        in_specs=[pl.BlockSpec((1, tt, 2 * D_B), lambda bb, i: (bb, i, 0)),
                  pl.BlockSpec((1, CONV_HALO, 2 * D_B),
                               lambda bb, i: (bb, jnp.maximum(i * per - 1, 0), 0)),
                  pl.BlockSpec((CONV_WIDTH, D_B), vec),
                  pl.BlockSpec((1, D_B), vec),
                  pl.BlockSpec((1, D_B), vec),
                  pl.BlockSpec((1, D_B), vec)],
        out_specs=pl.BlockSpec((1, tt, D_B), lambda bb, i: (bb, i, 0)),
        out_shape=jax.ShapeDtypeStruct((bsz, s, D_B), BF16),
        scratch_shapes=[pltpu.VMEM((CONV_HALO + tt, D_B), F32),
                        pltpu.VMEM((CONV_HALO + 64, D_B), F32)],
        compiler_params=pltpu.CompilerParams(
            dimension_semantics=("arbitrary", "arbitrary"), vmem_limit_bytes=VMEM_LIMIT),
    )(conv_in, conv_in, w, b, g, beta)


def _rwkv_kernel(rw_ref, mu_ref, lora_ref, w0_ref, a0_ref, kk_ref, ka_ref, rk_ref, lg_ref, lb_ref,
                 o_ref, s_scr, prev_scr):
    c = pl.program_id(1)
    n = CHUNK
    nb = RW_CHUNKS * CHUNK

    @pl.when(c == 0)
    def _():
        s_scr[...] = jnp.zeros_like(s_scr)
        prev_scr[...] = jnp.zeros_like(prev_scr)

    ps = rw_ref[0]
    row = lax.broadcasted_iota(jnp.int32, (nb, 1), 0)
    prev = jnp.where(row == 0, prev_scr[0:1, :], pltpu.roll(ps, 1, axis=0))
    prev_scr[0:1, :] = ps[nb - 1:nb, :]
    xs = ps + mu_ref[...] * (prev - ps)
    r = xs[:, 0:D_C]
    k = xs[:, D_C:2 * D_C]
    v = xs[:, 2 * D_C:3 * D_C]
    dn = xs[:, 3 * D_C:3 * D_C + 2 * D_LORA]
    lane64 = lax.broadcasted_iota(jnp.int32, (nb, 2 * D_LORA), 1)
    dn = jnp.where(lane64 < D_LORA, jnp.tanh(dn), dn)
    dn_hi, dn_lo = _split2(dn)
    lora_hi, lora_lo = _split2(lora_ref[...])
    pre = _bdot(dn_hi, lora_hi) + _bdot(dn_hi, lora_lo) + _bdot(dn_lo, lora_hi)
    z = -(w0_ref[...] + pre[:, :D_C])
    softplus = jnp.maximum(z, 0.0) + jnp.log(1.0 + jnp.exp(-jnp.abs(z)))
    logdec = -jnp.exp(-softplus - 0.5)
    a = _sigmoid(a0_ref[...] + pre[:, D_C:])
    kkraw = k * kk_ref[...]
    k2 = k * (1.0 + (a - 1.0) * ka_ref[...])
    bonus_pre = r * k2 * rk_ref[...]

    rb = lax.broadcasted_iota(jnp.int32, (nb, nb), 0)
    cb = lax.broadcasted_iota(jnp.int32, (nb, nb), 1)
    tri = jnp.where(jnp.logical_and(rb // n == cb // n, cb <= rb), 1.0, 0.0)
    ld_hi, ld_rest = _split2(logdec)
    ld_mid, ld_lo = _split2(ld_rest)
    cum_all = _bdot(tri, ld_hi) + _bdot(tri, ld_mid) + _bdot(tri, ld_lo)
    p_in_all = jnp.exp(cum_all)
    p_inv_all = jnp.exp(-cum_all)
    p_ex_all = jnp.exp(cum_all - logdec)

    row_n = lax.broadcasted_iota(jnp.int32, (n, LANES), 0)
    lane_n = lax.broadcasted_iota(jnp.int32, (n, LANES), 1)
    lo = lane_n < HEAD_DIM
    col = lane_n & (HEAD_DIM - 1)
    incl = col <= row_n
    strict = col < row_n
    mask2 = jnp.concatenate([strict, incl], axis=0)
    r2 = lax.broadcasted_iota(jnp.int32, (2 * n, LANES), 0)
    l2 = lax.broadcasted_iota(jnp.int32, (2 * n, LANES), 1)
    same_head = (r2 // HEAD_DIM) == (l2 // HEAD_DIM)
    ones_bd = jnp.where(same_head, 1.0, 0.0).astype(BF16)

    def bd(t):
        tb = t.astype(BF16)
        z = jnp.zeros_like(tb)
        return jnp.concatenate([jnp.where(lo, tb, z), jnp.where(lo, z, tb)], axis=0)

    def hsum(t):
        return jnp.dot(t.astype(BF16), ones_bd, preferred_element_type=F32)

    pairs = range(H_C // 2)
    chains = [(j, g) for j in range(RW_CHUNKS) for g in pairs]
    ids = range(len(chains))
    sl = lambda t, i: t[chains[i][0] * n:(chains[i][0] + 1) * n,
                        chains[i][1] * LANES:(chains[i][1] + 1) * LANES]
    v_p = [sl(v, i) for i in ids]
    p_in = [sl(p_in_all, i) for i in ids]
    kkn = []
    for i in ids:
        kr = sl(kkraw, i)
        kkn.append(kr * lax.rsqrt(jnp.maximum(hsum(kr * kr), 1e-12)))
    rt = [sl(r, i) * p_in[i] for i in ids]
    at = [kkn[i] * sl(p_ex_all, i) for i in ids]
    kt = [sl(k2, i) * sl(p_inv_all, i) for i in ids]
    bt = [kkn[i] * sl(a, i) * sl(p_inv_all, i) for i in ids]
    lhs = [jnp.concatenate([at[i], rt[i]], axis=0).astype(BF16) for i in ids]
    ak = [lax.dot_general(lhs[i], bd(kt[i]), NT_DIMS, preferred_element_type=F32) for i in ids]
    ab = [lax.dot_general(lhs[i], bd(bt[i]), NT_DIMS, preferred_element_type=F32) for i in ids]
    av = [_bdot(jnp.where(mask2, ak[i], 0.0), bd(v_p[i])) for i in ids]
    a_rb = [jnp.where(incl, ab[i][n:], 0.0) for i in ids]
    npow = [jnp.where(strict, -ab[i][:n], 0.0) for i in ids]
    tm = list(npow)
    for _ in range(5):
        both = [_bdot(jnp.concatenate([npow[i], tm[i]], axis=0), bd(npow[i])) for i in ids]
        npow = [both[i][:n] for i in ids]
        tm = [tm[i] + both[i][n:] for i in ids]
    tm = [tm[i] + _bdot(tm[i], bd(npow[i])) for i in ids]
    wt = [at[i] + _bdot(tm[i], bd(at[i])) for i in ids]
    uloc = [av[i][:n] + _bdot(tm[i], bd(av[i][:n])) for i in ids]
    mcat = [jnp.concatenate([kt[i], bt[i]], axis=0).astype(BF16) for i in ids]
    bonus = [hsum(sl(bonus_pre, i)) * v_p[i] for i in ids]

    state = [s_scr[g] for g in pairs]
    for j in range(RW_CHUNKS):
        cid = [j * (H_C // 2) + g for g in pairs]
        ws = [lax.dot_general(jnp.concatenate([wt[i], rt[i]], axis=0).astype(BF16),
                              state[g].astype(BF16), NT_DIMS, preferred_element_type=F32)
              for g, i in zip(pairs, cid)]
        u = [ws[g][:n] + uloc[i] for g, i in zip(pairs, cid)]
        o = [ws[g][n:] + av[i][n:] - _bdot(a_rb[i], bd(u[g])) for g, i in zip(pairs, cid)]
        for g, i in zip(pairs, cid):
            ncat = jnp.concatenate([v_p[i], -u[g]], axis=0).astype(BF16)
            upd = lax.dot_general(ncat, mcat[i], TN_DIMS, preferred_element_type=F32)
            state[g] = (state[g] + jnp.where(same_head, upd, 0.0)) * p_in[i][n - 1:n, :]
        for g, i in zip(pairs, cid):
            lanes = slice(g * LANES, (g + 1) * LANES)
            cen = o[g] - hsum(o[g]) * (1.0 / HEAD_DIM)
            var = hsum(cen * cen) * (1.0 / HEAD_DIM)
            y = cen * lax.rsqrt(var + LNX_EPS) * lg_ref[:, lanes] + lb_ref[:, lanes]
            o_ref[0, j * n:(j + 1) * n, lanes] = (y + bonus[i]).astype(BF16)
    for g in pairs:
        s_scr[g] = state[g]


def _rwkv(rw, mu, lora, w0, a0, kk, ka, rk, lg, lb):
    bsz, s, _ = rw.shape
    vec = lambda bb, i: (0, 0)
    return pl.pallas_call(
        _rwkv_kernel,
        grid=(bsz, s // (RW_CHUNKS * CHUNK)),
        in_specs=[pl.BlockSpec((1, RW_CHUNKS * CHUNK, RW_COLS), lambda bb, i: (bb, i, 0)),
                  pl.BlockSpec((1, RW_COLS), vec),
                  pl.BlockSpec((2 * D_LORA, 2 * D_C), vec)] +
                 [pl.BlockSpec((1, D_C), vec)] * 7,
        out_specs=pl.BlockSpec((1, RW_CHUNKS * CHUNK, D_C), lambda bb, i: (bb, i, 0)),
        out_shape=jax.ShapeDtypeStruct((bsz, s, D_C), BF16),
        scratch_shapes=[pltpu.VMEM((H_C // 2, 2 * HEAD_DIM, LANES), F32),
                        pltpu.VMEM((8, RW_COLS), F32)],
        compiler_params=pltpu.CompilerParams(
            dimension_semantics=("arbitrary", "arbitrary"), vmem_limit_bytes=VMEM_LIMIT),
    )(rw, mu, lora, w0, a0, kk, ka, rk, lg, lb)


def _outproj_kernel(x_ref, attn_ref, conv_ref, rw_ref, gates_ref, gm_ref, w_ref, o_ref):
    sg = _silu(gates_ref[...].astype(F32))
    ya = (attn_ref[...].astype(F32) * sg[:, 0:D_A]).astype(BF16)
    yb = (conv_ref[...].astype(F32) * sg[:, D_A:D_A + D_B]).astype(BF16)
    yc = (rw_ref[...].astype(F32) * sg[:, D_A + D_B:D_MIX]).astype(BF16)
    y = (jnp.dot(ya, w_ref[0:D_A, :], preferred_element_type=F32)
         + jnp.dot(yb, w_ref[D_A:D_A + D_B, :], preferred_element_type=F32)
         + jnp.dot(yc, w_ref[D_A + D_B:D_MIX, :], preferred_element_type=F32))
    o_ref[...] = x_ref[...] + gm_ref[0] * y


def _outproj(x2, attn, conv, rw, gates, gate_mod, w, seq, tm):
    n, d = x2.shape
    per_b = seq // tm
    row = lambda i: (i, 0)
    return pl.pallas_call(
        _outproj_kernel,
        grid=(n // tm,),
        in_specs=[pl.BlockSpec((tm, d), row),
                  pl.BlockSpec((tm, D_A), row),
                  pl.BlockSpec((tm, D_B), row),
                  pl.BlockSpec((tm, D_C), row),
                  pl.BlockSpec((tm, D_MIX), row),
                  pl.BlockSpec((1, 1, d), lambda i: (i // per_b, 0, 0)),
                  pl.BlockSpec((D_MIX, d), lambda i: (0, 0))],
        out_specs=pl.BlockSpec((tm, d), row),
        out_shape=jax.ShapeDtypeStruct((n, d), F32),
        compiler_params=pltpu.CompilerParams(
            dimension_semantics=("arbitrary",), vmem_limit_bytes=VMEM_LIMIT),
    )(x2, attn, conv, rw, gates, gate_mod, w)


def _reorder_w_in(w_in):
    depth, d, _ = w_in.shape
    a_cols = 4 * D_A + N_IDX_HEADS * D_IDX + D_IDX + N_IDX_HEADS
    b0 = a_cols
    c0 = a_cols + 3 * D_B
    c_shift = 3 * D_C + 2 * D_LORA
    zeros = lambda m: jnp.zeros((depth, d, m), w_in.dtype)
    qi0 = 4 * D_A
    ki0 = qi0 + N_IDX_HEADS * D_IDX
    wi0 = ki0 + D_IDX
    parts = [w_in[..., 0:3 * D_A],
             w_in[..., qi0:ki0],
             w_in[..., ki0:wi0], w_in[..., ki0:wi0],
             w_in[..., wi0:a_cols], zeros(LANES - N_IDX_HEADS),
             w_in[..., 3 * D_A:4 * D_A], w_in[..., b0 + 2 * D_B:b0 + 3 * D_B],
             w_in[..., c0 + c_shift:c0 + c_shift + D_C],
             w_in[..., b0:b0 + 2 * D_B],
             w_in[..., c0:c0 + c_shift], zeros(RW_COLS - c_shift)]
    w = jnp.concatenate(parts, axis=-1)
    assert w.shape[-1] == N_COLS
    return w.astype(BF16)


def kernel(x, c, norm_g, w_ada, b_ada, w_in, w_out, q_norm_g, k_norm_g, rel_bias, conv_w, conv_b,
           conv_ln_g, conv_ln_b, shift_mu, decay_w0, decay_up, iclr_a0, iclr_up, key_k, key_a,
           bonus_r_k, lnx_g, lnx_b):
    bsz, seq, d = x.shape
    depth = w_in.shape[0]
    assert seq % (GROUP * TILE) == 0 and d == w_out.shape[-1]
    topk = min(TOPK_MAX, seq // 4)
    tm = GROUP * TILE
    conv_tt = 256

    mod = _modulation(c, w_ada, b_ada)
    shift = mod[:, :, None, 0:d]
    sc1p = 1.0 + mod[:, :, None, d:2 * d]
    gate_mod = mod[:, :, None, 2 * d:3 * d]

    w_all = _reorder_w_in(w_in)
    head = jnp.arange(D_A) // HEAD_DIM
    avg = (head[:, None] == head[None, :]).astype(BF16) * (1.0 / HEAD_DIM)
    qg = jnp.tile(q_norm_g, (1, H_A))[:, None, :] * (HEAD_DIM ** -0.5)
    kg = jnp.tile(k_norm_g, (1, H_A))[:, None, :]
    bias_tiles = _bias_tiles(rel_bias)
    c_shift = 3 * D_C + 2 * D_LORA
    mu_pad = jnp.pad(shift_mu, ((0, 0), (0, RW_COLS - c_shift)))[:, None, :]
    zl = jnp.zeros((depth, D_LORA, D_C), F32)
    lora = jnp.concatenate([jnp.concatenate([decay_up, zl], axis=2),
                            jnp.concatenate([zl, iclr_up], axis=2)], axis=1)
    row = lambda t: t[:, None, :]
    layers = dict(
        shift=shift, sc1p=sc1p, gate_mod=gate_mod, norm_g=row(norm_g), w=w_all, qg=qg, kg=kg,
        w_out=w_out.astype(BF16), conv_w=conv_w, conv_b=row(conv_b), conv_g=row(conv_ln_g),
        conv_beta=row(conv_ln_b), mu=mu_pad, lora=lora, w0=row(decay_w0), a0=row(iclr_a0),
        kk=row(key_k), ka=row(key_a), rk=row(bonus_r_k.reshape(depth, D_C)), lg=row(lnx_g),
        lb=row(lnx_b))

    def layer(x2, p):
        q, k, vt, qi, ki, wit, gates, conv_in, rw_in = _inproj(
            x2, p["sc1p"], p["shift"], p["norm_g"], p["w"], avg, p["qg"], p["kg"], seq, tm)
        r3 = lambda t: t.reshape(bsz, seq, t.shape[-1])
        attn = _attention(r3(q), r3(k), vt, r3(qi), r3(ki), wit, bias_tiles, topk)
        conv = _conv(r3(conv_in), p["conv_w"], p["conv_b"], p["conv_g"], p["conv_beta"], conv_tt)
        rwo = _rwkv(r3(rw_in), p["mu"], p["lora"], p["w0"], p["a0"], p["kk"], p["ka"], p["rk"],
                    p["lg"], p["lb"])
        x2 = _outproj(x2, attn.reshape(-1, D_A), conv.reshape(-1, D_B), rwo.reshape(-1, D_C),
                      gates, p["gate_mod"], p["w_out"], seq, tm)
        return x2

    x2 = x.reshape(bsz * seq, d)
    for l in range(depth):
        x2 = layer(x2, {name: t[l] for name, t in layers.items()})
    return x2.reshape(bsz, seq, d)
```

```python
import functools
import math

import jax
import jax.numpy as jnp
import numpy as np
from jax import lax
from jax.experimental import pallas as pl
from jax.experimental.pallas import tpu as pltpu

F32 = jnp.float32
BF16 = jnp.bfloat16

HEAD_DIM = 64
CHUNK = 64
EPS = 1e-6
H_A = 6
D_A = H_A * HEAD_DIM
N_IDX_HEADS = 8
D_IDX = 64
TOPK_MAX = 256
N_BUCKETS = 32
MAX_DISTANCE = 1024
D_B = 256
CONV_WIDTH = 31
H_C = 6
D_C = H_C * HEAD_DIM
D_LORA = 32
LNX_EPS = 64e-5
D_MIX = D_A + D_B + D_C

LANES = 128
TILE = 128
GROUP = 4
HALF = 1 << 15
DEN_FLOOR = 1e-30
LATE_STEPS = 6
ONES_ROWS = 16
RW_CHUNKS = 4
QW = 256
N_FAR = 9
N_BIAS = N_FAR + QW // TILE + 1
ATTN_VMEM_LIMIT = 60 * 1024 * 1024
VMEM_LIMIT = 48 * 1024 * 1024

C_Q, C_K, C_V = 0, 384, 768
C_QI = 1152
C_KI = 1664
C_WI = 1792
C_GATES = 1920
C_CONV = 2944
C_RW = 3456
N_COLS = 4736
RW_COLS = 1280

INT_MIN = -(2 ** 31)
INT_MAX = 2 ** 31 - 1
NEG_KEY = int(np.array(-np.inf, np.float32).view(np.int32)) ^ 0x7FFFFFFF
NEG_BIG = -1e30

NT_DIMS = (((1,), (1,)), ((), ()))
TN_DIMS = (((0,), (0,)), ((), ()))


def _bdot(a, b):
    return jnp.dot(a.astype(BF16), b.astype(BF16), preferred_element_type=F32)


def _bdot_nt(a, b):
    return lax.dot_general(a.astype(BF16), b.astype(BF16), NT_DIMS, preferred_element_type=F32)


def _split2(x):
    hi = x.astype(BF16).astype(F32)
    return hi, x - hi


def _sigmoid(x):
    return 1.0 / (1.0 + jnp.exp(-x))


def _silu(x):
    return x * _sigmoid(x)


def _mod_kernel(c_ref, w_ref, b_ref, o_ref):
    ca = _silu(c_ref[...])
    o_ref[0] = jnp.dot(ca, w_ref[0], precision=lax.Precision.HIGHEST,
                       preferred_element_type=F32) + b_ref[0]


def _modulation(c, w_ada, b_ada):
    depth, d, d3 = w_ada.shape
    b = c.shape[0]
    return pl.pallas_call(
        _mod_kernel,
        grid=(depth, d3 // d),
        in_specs=[pl.BlockSpec((b, d), lambda l, j: (0, 0)),
                  pl.BlockSpec((1, d, d), lambda l, j: (l, 0, j)),
                  pl.BlockSpec((1, 1, d), lambda l, j: (l, 0, j))],
        out_specs=pl.BlockSpec((1, b, d), lambda l, j: (l, 0, j)),
        out_shape=jax.ShapeDtypeStruct((depth, b, d3), F32),
        compiler_params=pltpu.CompilerParams(
            dimension_semantics=("arbitrary", "arbitrary"), vmem_limit_bytes=VMEM_LIMIT),
    )(c, w_ada, b_ada.reshape(depth, 1, d3))


def _inproj_kernel(x_ref, sc_ref, sh_ref, g_ref, w_ref, avg_ref, qg_ref, kg_ref,
                   q_ref, k_ref, vt_ref, qi_ref, ki_ref, wit_ref, gates_ref, conv_ref, rw_ref):
    x = x_ref[...]
    ms = jnp.mean(x * x, axis=-1, keepdims=True)
    h = x * lax.rsqrt(ms + EPS) * g_ref[...]
    hb = (h * sc_ref[0] + sh_ref[0]).astype(BF16)

    def mm(c0, c1):
        return jnp.dot(hb, w_ref[:, c0:c1], preferred_element_type=F32)

    def head_rms(t, g):
        m2 = jnp.dot((t * t).astype(BF16), avg_ref[...], preferred_element_type=F32)
        return t * lax.rsqrt(m2 + EPS) * g

    q_ref[...] = head_rms(mm(C_Q, C_Q + D_A), qg_ref[...]).astype(BF16)
    k_ref[...] = head_rms(mm(C_K, C_K + D_A), kg_ref[...]).astype(BF16)
    vt_ref[0] = mm(C_V, C_V + D_A).T.astype(BF16)
    qi_ref[...] = mm(C_QI, C_QI + 512).astype(BF16)
    ki_ref[...] = mm(C_KI, C_KI + LANES).astype(BF16)
    wit_ref[...] = mm(C_WI, C_WI + LANES).T[:N_IDX_HEADS, :]
    gates_ref[:, 0:512] = mm(C_GATES, C_GATES + 512).astype(BF16)
    gates_ref[:, 512:1024] = mm(C_GATES + 512, C_GATES + 1024).astype(BF16)
    conv_ref[...] = mm(C_CONV, C_CONV + 512)
    rw_ref[:, 0:512] = mm(C_RW, C_RW + 512)
    rw_ref[:, 512:1024] = mm(C_RW + 512, C_RW + 1024)
    rw_ref[:, 1024:RW_COLS] = mm(C_RW + 1024, C_RW + RW_COLS)


def _inproj(x2, sc1p, shift, norm_g, w, avg, qg, kg, seq, tm):
    n, d = x2.shape
    per_b = seq // tm
    row = lambda i: (i, 0)
    full = lambda i: (0, 0)
    bat = lambda i: (i // per_b, 0, 0)
    rows = lambda wd, dt: (pl.BlockSpec((tm, wd), row), jax.ShapeDtypeStruct((n, wd), dt))
    outs = [rows(D_A, BF16), rows(D_A, BF16),
            (pl.BlockSpec((1, D_A, tm), lambda i: (i, 0, 0)),
             jax.ShapeDtypeStruct((n // tm, D_A, tm), BF16)),
            rows(512, BF16), rows(LANES, BF16),
            (pl.BlockSpec((N_IDX_HEADS, tm), lambda i: (0, i)),
             jax.ShapeDtypeStruct((N_IDX_HEADS, n), F32)),
            rows(D_MIX, BF16), rows(512, F32), rows(RW_COLS, F32)]
    return pl.pallas_call(
        _inproj_kernel,
        grid=(n // tm,),
        in_specs=[pl.BlockSpec((tm, d), row),
                  pl.BlockSpec((1, 1, d), bat),
                  pl.BlockSpec((1, 1, d), bat),
                  pl.BlockSpec((1, d), full),
                  pl.BlockSpec((d, N_COLS), full),
                  pl.BlockSpec((D_A, D_A), full),
                  pl.BlockSpec((1, D_A), full),
                  pl.BlockSpec((1, D_A), full)],
        out_specs=[o[0] for o in outs],
        out_shape=[o[1] for o in outs],
        compiler_params=pltpu.CompilerParams(
            dimension_semantics=("arbitrary",), vmem_limit_bytes=VMEM_LIMIT),
    )(x2, sc1p, shift, norm_g, w, avg, qg, kg)


def _fold(t, op):
    parts = [t[8 * i:8 * i + 8, :] for i in range(TILE // 8)]
    while len(parts) > 1:
        parts = [op(parts[i], parts[i + 1]) for i in range(0, len(parts), 2)]
    return parts[0]


def _attn_kernel(q_ref, k_ref, vt_ref, qi_ref, ki_ref, wit_ref, bias_ref, o_ref,
                 key_scr, khi_scr, klo_scr, klo2_scr, qih_scr, qh_scr, mb_scr, s_scr, m_scr, acc_scr,
                 *, topk, pos_bits):
    qb = pl.program_id(1)
    nst = ((qb + 1) * (QW // TILE) - 1) // GROUP + 1
    kf = float(topk)
    zgap = 1 << pos_bits
    lane = lax.broadcasted_iota(jnp.int32, (TILE, QW), 1)
    krow = lax.broadcasted_iota(jnp.int32, (TILE, QW), 0)
    lo_half = lax.broadcasted_iota(jnp.int32, (QW, LANES), 1) < HEAD_DIM

    wit = wit_ref[...]
    qi = qi_ref[0]
    q = q_ref[0]
    for src, dst, n_pairs in ((qi, qih_scr, N_IDX_HEADS // 2), (q, qh_scr, H_A // 2)):
        for g in range(n_pairs):
            pair = src[:, g * LANES:(g + 1) * LANES]
            dst[g, 0:QW, :] = jnp.where(lo_half, pair, jnp.zeros_like(pair))
            dst[g, QW:2 * QW, :] = jnp.where(lo_half, jnp.zeros_like(pair), pair)

    key_chunk = krow // CHUNK
    query_chunk = qb * (QW // CHUNK) + lane // CHUNK
    inadm = NEG_KEY - zgap - 1

    def score_key(s):
        b = lax.bitcast_convert_type(s, jnp.int32)
        return jnp.where(b < 0, (b ^ jnp.int32(0x7FFFFFFF)) - zgap, b)

    def p1(st, c):
        base = st * GROUP
        kis = ki_ref[0, pl.ds(pl.multiple_of(base * TILE, GROUP * TILE), GROUP * TILE), :]
        acc = [None] * GROUP
        for g in range(N_IDX_HEADS // 2):
            d = lax.dot_general(kis, qih_scr[g], NT_DIMS, preferred_element_type=F32)
            for par in range(2):
                w_row = wit[2 * g + par:2 * g + par + 1, :]
                for u in range(GROUP):
                    t = jnp.maximum(d[u * TILE:(u + 1) * TILE, par * QW:(par + 1) * QW], 0.0) * w_row
                    acc[u] = t if acc[u] is None else acc[u] + t
        for u in range(GROUP):
            kt = base + u
            key = jnp.where(acc[u] == 0.0, -1 - (krow + kt * TILE), score_key(acc[u]))
            ok = key_chunk + kt * (TILE // CHUNK) <= query_chunk
            key = jnp.where(ok, key, inadm)
            key_scr[kt] = key
            khi_scr[kt] = (key >> 16).astype(jnp.int16)
            klo_scr[kt] = ((key & (2 * HALF - 1)) - HALF).astype(jnp.int16)
        k_s = k_ref[0, pl.ds(pl.multiple_of(base * TILE, GROUP * TILE), GROUP * TILE), :]
        for g in range(H_A // 2):
            s = lax.dot_general(k_s[:, g * LANES:(g + 1) * LANES], qh_scr[g], NT_DIMS,
                                preferred_element_type=F32)
            for par in range(2):
                h = 2 * g + par
                mx = m_scr[h]
                for u in range(GROUP):
                    dd = jnp.clip(base + u - qb * (QW // TILE), -N_FAR, QW // TILE) + N_FAR
                    su = (s[u * TILE:(u + 1) * TILE, par * QW:(par + 1) * QW]
                          + bias_ref[h * N_BIAS + dd])
                    s_scr[h, base + u] = su
                    mx = jnp.maximum(mx, _fold(su, jnp.maximum))
                m_scr[h] = mx
        return c

    for h in range(H_A):
        m_scr[h] = jnp.full((8, QW), NEG_BIG, F32)
    lax.fori_loop(0, nst, p1, 0)

    def count_ge(thr_row):
        def body(st, cs):
            return tuple(c + _fold(jnp.where(key_scr[st * GROUP + u] >= thr_row, 1.0, 0.0), jnp.add)
                         for u, c in enumerate(cs))
        cs = lax.fori_loop(0, nst, body, (jnp.zeros((8, QW), F32),) * GROUP)
        return jnp.sum(functools.reduce(jnp.add, cs), axis=0, keepdims=True)

    def fold16(t):
        parts = [t[16 * i:16 * i + 16, :] for i in range(TILE // 16)]
        while len(parts) > 1:
            parts = [parts[i] + parts[i + 1] for i in range(0, len(parts), 2)]
        return parts[0]

    one16 = jnp.ones((TILE, QW), jnp.int16)
    zero16 = jnp.zeros((TILE, QW), jnp.int16)

    def as16(row):
        return jnp.broadcast_to(row, (TILE, QW)).astype(jnp.int16)

    def count16(scr, thr_row):
        thr16 = as16(thr_row)

        def body(st, cs):
            return tuple(c + fold16(jnp.where(scr[st * GROUP + u] >= thr16, one16, zero16))
                         for u, c in enumerate(cs))
        cs = lax.fori_loop(0, nst, body, (jnp.zeros((16, QW), jnp.int16),) * GROUP)
        total = functools.reduce(jnp.add, [c.astype(F32) for c in cs])
        return jnp.sum(total, axis=0, keepdims=True)

    def hi_step(_, carry):
        lo, hi, c_lo, c_hi = carry
        mid = (lo + hi) >> 1
        cnt = count16(khi_scr, mid)
        ge = cnt >= kf
        return (jnp.where(ge, mid, lo), jnp.where(ge, hi, mid),
                jnp.where(ge, cnt, c_lo), jnp.where(ge, c_hi, cnt))

    n_keys = (nst * (GROUP * TILE)).astype(F32)
    top, _, c_top, c_above = lax.fori_loop(
        0, 16, hi_step, (jnp.full((1, QW), -HALF, jnp.int32), jnp.full((1, QW), HALF, jnp.int32),
                         jnp.broadcast_to(n_keys, (1, QW)), jnp.zeros((1, QW), F32)))

    top16 = as16(top)
    min16 = jnp.full((TILE, QW), -HALF, jnp.int16)

    def keep_bucket(kt, c):
        klo2_scr[kt] = jnp.where(khi_scr[kt] == top16, klo_scr[kt], min16)
        return c

    lax.fori_loop(0, nst * GROUP, keep_bucket, 0)
    need = kf - c_above

    def lo_step(lo, hi, c_lo):
        mid = (lo + hi) >> 1
        cnt = count16(klo2_scr, mid)
        active = hi != lo + 1
        ge = jnp.logical_and(active, cnt >= need)
        hi = jnp.where(jnp.logical_and(active, cnt == need), mid + 1,
                       jnp.where(jnp.logical_and(active, cnt < need), mid, hi))
        return jnp.where(ge, mid, lo), hi, jnp.where(ge, cnt, c_lo)

    def unsettled(lo, hi):
        return jnp.max(jnp.where(hi != lo + 1, 1.0, 0.0)) > 0.0

    def late(carry):
        lo, hi, c_lo, it, _ = carry
        lo, hi, c_lo = lo_step(*lo_step(lo, hi, c_lo))
        return lo, hi, c_lo, it + 2, unsettled(lo, hi)

    low0 = (jnp.full((1, QW), -HALF, jnp.int32), jnp.full((1, QW), HALF, jnp.int32), c_top - c_above)
    low = lax.fori_loop(0, 16 - LATE_STEPS, lambda _, c: lo_step(*c), low0)
    low, _, c_low, _, _ = lax.while_loop(lambda carry: jnp.logical_and(carry[3] < 16, carry[4]),
                                         late, (*low, 16 - LATE_STEPS, unsettled(low[0], low[1])))
    thr = top * (2 * HALF) + (low + HALF)
    cnt_ge = c_above + c_low
    tie = jnp.logical_and(cnt_ge > kf, thr > inadm)

    @pl.when(jnp.max(jnp.where(tie, 1.0, 0.0)) > 0.0)
    def _():
        need = kf - count_ge(thr + 1)

        def count_eq_le(pmax):
            def body(kt, c):
                hit = jnp.where(key_scr[kt] == thr, krow + kt * TILE, INT_MAX) <= pmax
                return c + _fold(jnp.where(hit, 1.0, 0.0), jnp.add)
            c = lax.fori_loop(0, nst * GROUP, body, jnp.zeros((8, QW), F32))
            return jnp.sum(c, axis=0, keepdims=True)

        def pbisect(_, lohi):
            plo, phi = lohi
            pmid = (plo + phi) >> 1
            ok = count_eq_le(pmid) >= need
            return jnp.where(ok, plo, pmid), jnp.where(ok, pmid, phi)

        plo0 = jnp.full((1, QW), -1, jnp.int32)
        phi0 = jnp.full((1, QW), (1 << pos_bits) - 1, jnp.int32)
        _, pthr = lax.fori_loop(0, pos_bits + 1, pbisect, (plo0, phi0))
        pthr = jnp.where(tie, pthr, INT_MAX)

        def drop(kt, c):
            kk = key_scr[kt]
            pos = jnp.where(kk == thr, krow + kt * TILE, -1)
            key_scr[kt] = jnp.where(pos > pthr, inadm, kk)
            return c

        lax.fori_loop(0, nst * GROUP, drop, 0)

    thr_eff = jnp.maximum(thr, inadm + 1)

    ones_rows = jnp.ones((ONES_ROWS, GROUP * TILE), BF16)

    def collapse_max():
        for h in range(H_A):
            m_scr[h] = jnp.broadcast_to(jnp.max(m_scr[h], axis=0, keepdims=True), (8, QW))

    def select_bias(base):
        for u in range(GROUP):
            mb_scr[u] = jnp.where(key_scr[base + u] >= thr_eff, 0.0, NEG_BIG)

    def pv(st, c):
        select_bias(st * GROUP)
        for h in range(H_A):
            shift = m_scr[h][0:1, :]
            ps = [jnp.exp((s_scr[h, st * GROUP + u] + mb_scr[u] - shift).astype(BF16))
                  for u in range(GROUP)]
            lhs = jnp.concatenate([vt_ref[st, h * HEAD_DIM:(h + 1) * HEAD_DIM, :], ones_rows], axis=0)
            acc_scr[h] += jnp.dot(lhs, jnp.concatenate(ps, axis=0), preferred_element_type=F32)
        return c

    def run_pv():
        for h in range(H_A):
            acc_scr[h] = jnp.zeros((HEAD_DIM + ONES_ROWS, QW), F32)
        lax.fori_loop(0, nst, pv, 0)

    collapse_max()
    run_pv()
    den_min = acc_scr[0][HEAD_DIM:HEAD_DIM + 1, :]
    for h in range(1, H_A):
        den_min = jnp.minimum(den_min, acc_scr[h][HEAD_DIM:HEAD_DIM + 1, :])

    @pl.when(jnp.logical_not(jnp.min(den_min) > DEN_FLOOR))
    def _():
        for h in range(H_A):
            m_scr[h] = jnp.full((8, QW), NEG_BIG, F32)

        def selected_max(st, c):
            select_bias(st * GROUP)
            for h in range(H_A):
                mx = m_scr[h]
                for u in range(GROUP):
                    mx = jnp.maximum(mx, _fold(s_scr[h, st * GROUP + u] + mb_scr[u], jnp.maximum))
                m_scr[h] = mx
            return c

        lax.fori_loop(0, nst, selected_max, 0)
        collapse_max()
        run_pv()

    outs = []
    for h in range(H_A):
        acc = acc_scr[h]
        outs.append(acc[0:HEAD_DIM, :] / acc[HEAD_DIM:HEAD_DIM + 1, :])
    o_ref[0] = jnp.concatenate(outs, axis=0).T.astype(BF16)


def _attention(q, k, vt, qi, ki, wit, bias_tiles, topk):
    b, s, _ = q.shape
    nt = s // TILE
    nq = s // QW
    ng = s // (GROUP * TILE)
    kern = functools.partial(_attn_kernel, topk=topk, pos_bits=max(1, int(math.ceil(math.log2(s)))))
    qtile = lambda bb, i: (bb, i, 0)
    whole = lambda bb, i: (bb, 0, 0)
    once = pl.Buffered(1)
    return pl.pallas_call(
        kern,
        grid=(b, nq),
        in_specs=[pl.BlockSpec((1, QW, D_A), qtile),
                  pl.BlockSpec((1, s, D_A), whole, pipeline_mode=once),
                  pl.BlockSpec((ng, D_A, GROUP * TILE), whole, pipeline_mode=once),
                  pl.BlockSpec((1, QW, 512), qtile),
                  pl.BlockSpec((1, s, LANES), whole, pipeline_mode=once),
                  pl.BlockSpec((N_IDX_HEADS, QW), lambda bb, i: (0, bb * nq + i)),
                  pl.BlockSpec(bias_tiles.shape, lambda bb, i: (0, 0, 0), pipeline_mode=once)],
        out_specs=pl.BlockSpec((1, QW, D_A), qtile),
        out_shape=jax.ShapeDtypeStruct((b, s, D_A), BF16),
        scratch_shapes=[pltpu.VMEM((nt, TILE, QW), jnp.int32),
                        pltpu.VMEM((nt, TILE, QW), jnp.int16),
                        pltpu.VMEM((nt, TILE, QW), jnp.int16),
                        pltpu.VMEM((nt, TILE, QW), jnp.int16),
                        pltpu.VMEM((N_IDX_HEADS // 2, 2 * QW, LANES), BF16),
                        pltpu.VMEM((H_A // 2, 2 * QW, LANES), BF16),
                        pltpu.VMEM((GROUP, TILE, QW), F32),
                        pltpu.VMEM((H_A, nt, TILE, QW), F32),
                        pltpu.VMEM((H_A, 8, QW), F32),
                        pltpu.VMEM((H_A, HEAD_DIM + ONES_ROWS, QW), F32)],
        compiler_params=pltpu.CompilerParams(
            dimension_semantics=("arbitrary", "arbitrary"), vmem_limit_bytes=ATTN_VMEM_LIMIT),
    )(q, k, vt, qi, ki, wit, bias_tiles)


def _t5_bucket(rel):
    nb = N_BUCKETS // 2
    max_exact = nb // 2
    ret = jnp.where(rel > 0, nb, 0)
    n = jnp.abs(rel)
    nf = jnp.maximum(n, 1).astype(F32)
    large = max_exact + (jnp.log(nf / max_exact) / math.log(MAX_DISTANCE / max_exact)
                         * (nb - max_exact)).astype(jnp.int32)
    large = jnp.minimum(large, nb - 1)
    return ret + jnp.where(n < max_exact, n, large)


def _bias_tiles(rel_bias):
    n = TILE + QW - 1
    n_off = N_BIAS - 1
    m = np.arange(n)
    q_minus_k = np.where(m < QW, m, m - n)
    first_key = (np.arange(n_off)[:, None] - N_FAR) * TILE
    rel = first_key - q_minus_k[None, :]
    u = rel_bias[_t5_bucket(jnp.asarray(rel, jnp.int32))].astype(F32)
    u = u.transpose(2, 0, 1)
    flat = jnp.tile(u, (1, 1, TILE))[:, :, :TILE * (n - 1)]
    tiles = flat.reshape(H_A, n_off, TILE, n - 1)[..., :QW]
    j = np.arange(TILE)[None, :, None]
    t = np.arange(QW)[None, None, :]
    admissible = (first_key[:, :, None] + j) // CHUNK <= t // CHUNK
    tiles = jnp.where(jnp.asarray(admissible)[None], tiles, NEG_BIG)
    tiles = jnp.concatenate([tiles, jnp.full((H_A, 1, TILE, QW), NEG_BIG, F32)], axis=1)
    return tiles.reshape(H_A * N_BIAS, TILE, QW)


CONV_HALO = 32


def _conv_kernel(cur_ref, halo_ref, w_ref, b_ref, g_ref, beta_ref, o_ref, u_scr, sh_scr, *, tt):
    i = pl.program_id(1)

    def glu(t):
        return t[:, :D_B] * _sigmoid(t[:, D_B:])

    u_scr[0:CONV_HALO, :] = jnp.where(i > 0, glu(halo_ref[0]), 0.0)
    u_scr[CONV_HALO:CONV_HALO + tt, :] = glu(cur_ref[0])
    rows = 64
    first = CONV_HALO - (CONV_WIDTH - 1)
    for r0 in range(0, tt, rows):
        acc = jnp.broadcast_to(b_ref[...], (rows, D_B))
        for res in range(8):
            offs = [first + j for j in range(CONV_WIDTH) if (first + j) % 8 == res]
            span = offs[-1] - offs[0] + rows
            sh_scr[0:span, :] = u_scr[r0 + offs[0]:r0 + offs[0] + span, :]
            for off in offs:
                acc = acc + w_ref[off - first:off - first + 1, :] * sh_scr[off - offs[0]:off - offs[0] + rows, :]
        mu = jnp.mean(acc, axis=-1, keepdims=True)
        cen = acc - mu
        var = jnp.mean(cen * cen, axis=-1, keepdims=True)
        y = cen * lax.rsqrt(var + EPS) * g_ref[...] + beta_ref[...]
        o_ref[0, r0:r0 + rows, :] = _silu(y).astype(BF16)


def _conv(conv_in, w, b, g, beta, tt):
    bsz, s, _ = conv_in.shape
    per = tt // CONV_HALO
    vec = lambda bb, i: (0, 0)
    return pl.pallas_call(
        functools.partial(_conv_kernel, tt=tt),
        grid=(bsz, s // tt),
        in_specs=[pl.BlockSpec((1, tt, 2 * D_B), lambda bb, i: (bb, i, 0)),
                  pl.BlockSpec((1, CONV_HALO, 2 * D_B),
                               lambda bb, i: (bb, jnp.maximum(i * per - 1, 0), 0)),
                  pl.BlockSpec((CONV_WIDTH, D_B), vec),
                  pl.BlockSpec((1, D_B), vec),
                  pl.BlockSpec((1, D_B), vec),
                  pl.BlockSpec((1, D_B), vec)],
        out_specs=pl.BlockSpec((1, tt, D_B), lambda bb, i: (bb, i, 0)),
        out_shape=jax.ShapeDtypeStruct((bsz, s, D_B), BF16),
        scratch_shapes=[pltpu.VMEM((CONV_HALO + tt, D_B), F32),
                        pltpu.VMEM((CONV_HALO + 64, D_B), F32)],
        compiler_params=pltpu.CompilerParams(
            dimension_semantics=("arbitrary", "arbitrary"), vmem_limit_bytes=VMEM_LIMIT),
    )(conv_in, conv_in, w, b, g, beta)


def _rwkv_kernel(rw_ref, mu_ref, lora_ref, w0_ref, a0_ref, kk_ref, ka_ref, rk_ref, lg_ref, lb_ref,
                 o_ref, s_scr, prev_scr):
    c = pl.program_id(1)
    n = CHUNK
    nb = RW_CHUNKS * CHUNK

    @pl.when(c == 0)
    def _():
        s_scr[...] = jnp.zeros_like(s_scr)
        prev_scr[...] = jnp.zeros_like(prev_scr)

    ps = rw_ref[0]
    row = lax.broadcasted_iota(jnp.int32, (nb, 1), 0)
    prev = jnp.where(row == 0, prev_scr[0:1, :], pltpu.roll(ps, 1, axis=0))
    prev_scr[0:1, :] = ps[nb - 1:nb, :]
    xs = ps + mu_ref[...] * (prev - ps)
    r = xs[:, 0:D_C]
    k = xs[:, D_C:2 * D_C]
    v = xs[:, 2 * D_C:3 * D_C]
    dn = xs[:, 3 * D_C:3 * D_C + 2 * D_LORA]
    lane64 = lax.broadcasted_iota(jnp.int32, (nb, 2 * D_LORA), 1)
    dn = jnp.where(lane64 < D_LORA, jnp.tanh(dn), dn)
    dn_hi, dn_lo = _split2(dn)
    lora_hi, lora_lo = _split2(lora_ref[...])
    pre = _bdot(dn_hi, lora_hi) + _bdot(dn_hi, lora_lo) + _bdot(dn_lo, lora_hi)
    z = -(w0_ref[...] + pre[:, :D_C])
    softplus = jnp.maximum(z, 0.0) + jnp.log(1.0 + jnp.exp(-jnp.abs(z)))
    logdec = -jnp.exp(-softplus - 0.5)
    a = _sigmoid(a0_ref[...] + pre[:, D_C:])
    kkraw = k * kk_ref[...]
    k2 = k * (1.0 + (a - 1.0) * ka_ref[...])
    bonus_pre = r * k2 * rk_ref[...]

    rb = lax.broadcasted_iota(jnp.int32, (nb, nb), 0)
    cb = lax.broadcasted_iota(jnp.int32, (nb, nb), 1)
    tri = jnp.where(jnp.logical_and(rb // n == cb // n, cb <= rb), 1.0, 0.0)
    ld_hi, ld_rest = _split2(logdec)
    ld_mid, ld_lo = _split2(ld_rest)
    cum_all = _bdot(tri, ld_hi) + _bdot(tri, ld_mid) + _bdot(tri, ld_lo)
    p_in_all = jnp.exp(cum_all)
    p_inv_all = jnp.exp(-cum_all)
    p_ex_all = jnp.exp(cum_all - logdec)

    row_n = lax.broadcasted_iota(jnp.int32, (n, LANES), 0)
    lane_n = lax.broadcasted_iota(jnp.int32, (n, LANES), 1)
    lo = lane_n < HEAD_DIM
    col = lane_n & (HEAD_DIM - 1)
    incl = col <= row_n
    strict = col < row_n
    mask2 = jnp.concatenate([strict, incl], axis=0)
    r2 = lax.broadcasted_iota(jnp.int32, (2 * n, LANES), 0)
    l2 = lax.broadcasted_iota(jnp.int32, (2 * n, LANES), 1)
    same_head = (r2 // HEAD_DIM) == (l2 // HEAD_DIM)
    ones_bd = jnp.where(same_head, 1.0, 0.0).astype(BF16)

    def bd(t):
        tb = t.astype(BF16)
        z = jnp.zeros_like(tb)
        return jnp.concatenate([jnp.where(lo, tb, z), jnp.where(lo, z, tb)], axis=0)

    def hsum(t):
        return jnp.dot(t.astype(BF16), ones_bd, preferred_element_type=F32)

    pairs = range(H_C // 2)
    chains = [(j, g) for j in range(RW_CHUNKS) for g in pairs]
    ids = range(len(chains))
    sl = lambda t, i: t[chains[i][0] * n:(chains[i][0] + 1) * n,
                        chains[i][1] * LANES:(chains[i][1] + 1) * LANES]
    v_p = [sl(v, i) for i in ids]
    p_in = [sl(p_in_all, i) for i in ids]
    kkn = []
    for i in ids:
        kr = sl(kkraw, i)
        kkn.append(kr * lax.rsqrt(jnp.maximum(hsum(kr * kr), 1e-12)))
    rt = [sl(r, i) * p_in[i] for i in ids]
    at = [kkn[i] * sl(p_ex_all, i) for i in ids]
    kt = [sl(k2, i) * sl(p_inv_all, i) for i in ids]
    bt = [kkn[i] * sl(a, i) * sl(p_inv_all, i) for i in ids]
    lhs = [jnp.concatenate([at[i], rt[i]], axis=0).astype(BF16) for i in ids]
    ak = [lax.dot_general(lhs[i], bd(kt[i]), NT_DIMS, preferred_element_type=F32) for i in ids]
    ab = [lax.dot_general(lhs[i], bd(bt[i]), NT_DIMS, preferred_element_type=F32) for i in ids]
    av = [_bdot(jnp.where(mask2, ak[i], 0.0), bd(v_p[i])) for i in ids]
    a_rb = [jnp.where(incl, ab[i][n:], 0.0) for i in ids]
    npow = [jnp.where(strict, -ab[i][:n], 0.0) for i in ids]
    tm = list(npow)
    for _ in range(5):
        both = [_bdot(jnp.concatenate([npow[i], tm[i]], axis=0), bd(npow[i])) for i in ids]
        npow = [both[i][:n] for i in ids]
        tm = [tm[i] + both[i][n:] for i in ids]
    tm = [tm[i] + _bdot(tm[i], bd(npow[i])) for i in ids]
    wt = [at[i] + _bdot(tm[i], bd(at[i])) for i in ids]
    uloc = [av[i][:n] + _bdot(tm[i], bd(av[i][:n])) for i in ids]
    mcat = [jnp.concatenate([kt[i], bt[i]], axis=0).astype(BF16) for i in ids]
    bonus = [hsum(sl(bonus_pre, i)) * v_p[i] for i in ids]

    state = [s_scr[g] for g in pairs]
    for j in range(RW_CHUNKS):
        cid = [j * (H_C // 2) + g for g in pairs]
        ws = [lax.dot_general(jnp.concatenate([wt[i], rt[i]], axis=0).astype(BF16),
                              state[g].astype(BF16), NT_DIMS, preferred_element_type=F32)
              for g, i in zip(pairs, cid)]
        u = [ws[g][:n] + uloc[i] for g, i in zip(pairs, cid)]
        o = [ws[g][n:] + av[i][n:] - _bdot(a_rb[i], bd(u[g])) for g, i in zip(pairs, cid)]
        for g, i in zip(pairs, cid):
            ncat = jnp.concatenate([v_p[i], -u[g]], axis=0).astype(BF16)
            upd = lax.dot_general(ncat, mcat[i], TN_DIMS, preferred_element_type=F32)
            state[g] = (state[g] + jnp.where(same_head, upd, 0.0)) * p_in[i][n - 1:n, :]
        for g, i in zip(pairs, cid):
            lanes = slice(g * LANES, (g + 1) * LANES)
            cen = o[g] - hsum(o[g]) * (1.0 / HEAD_DIM)
            var = hsum(cen * cen) * (1.0 / HEAD_DIM)
            y = cen * lax.rsqrt(var + LNX_EPS) * lg_ref[:, lanes] + lb_ref[:, lanes]
            o_ref[0, j * n:(j + 1) * n, lanes] = (y + bonus[i]).astype(BF16)
    for g in pairs:
        s_scr[g] = state[g]


def _rwkv(rw, mu, lora, w0, a0, kk, ka, rk, lg, lb):
    bsz, s, _ = rw.shape
    vec = lambda bb, i: (0, 0)
    return pl.pallas_call(
        _rwkv_kernel,
        grid=(bsz, s // (RW_CHUNKS * CHUNK)),
        in_specs=[pl.BlockSpec((1, RW_CHUNKS * CHUNK, RW_COLS), lambda bb, i: (bb, i, 0)),
                  pl.BlockSpec((1, RW_COLS), vec),
                  pl.BlockSpec((2 * D_LORA, 2 * D_C), vec)] +
                 [pl.BlockSpec((1, D_C), vec)] * 7,
        out_specs=pl.BlockSpec((1, RW_CHUNKS * CHUNK, D_C), lambda bb, i: (bb, i, 0)),
        out_shape=jax.ShapeDtypeStruct((bsz, s, D_C), BF16),
        scratch_shapes=[pltpu.VMEM((H_C // 2, 2 * HEAD_DIM, LANES), F32),
                        pltpu.VMEM((8, RW_COLS), F32)],
        compiler_params=pltpu.CompilerParams(
            dimension_semantics=("arbitrary", "arbitrary"), vmem_limit_bytes=VMEM_LIMIT),
    )(rw, mu, lora, w0, a0, kk, ka, rk, lg, lb)


def _outproj_kernel(x_ref, attn_ref, conv_ref, rw_ref, gates_ref, gm_ref, w_ref, o_ref):
    sg = _silu(gates_ref[...].astype(F32))
    ya = (attn_ref[...].astype(F32) * sg[:, 0:D_A]).astype(BF16)
    yb = (conv_ref[...].astype(F32) * sg[:, D_A:D_A + D_B]).astype(BF16)
    yc = (rw_ref[...].astype(F32) * sg[:, D_A + D_B:D_MIX]).astype(BF16)
    y = (jnp.dot(ya, w_ref[0:D_A, :], preferred_element_type=F32)
         + jnp.dot(yb, w_ref[D_A:D_A + D_B, :], preferred_element_type=F32)
         + jnp.dot(yc, w_ref[D_A + D_B:D_MIX, :], preferred_element_type=F32))
    o_ref[...] = x_ref[...] + gm_ref[0] * y


def _outproj(x2, attn, conv, rw, gates, gate_mod, w, seq, tm):
    n, d = x2.shape
    per_b = seq // tm
    row = lambda i: (i, 0)
    return pl.pallas_call(
        _outproj_kernel,
        grid=(n // tm,),
        in_specs=[pl.BlockSpec((tm, d), row),
                  pl.BlockSpec((tm, D_A), row),
                  pl.BlockSpec((tm, D_B), row),
                  pl.BlockSpec((tm, D_C), row),
                  pl.BlockSpec((tm, D_MIX), row),
                  pl.BlockSpec((1, 1, d), lambda i: (i // per_b, 0, 0)),
                  pl.BlockSpec((D_MIX, d), lambda i: (0, 0))],
        out_specs=pl.BlockSpec((tm, d), row),
        out_shape=jax.ShapeDtypeStruct((n, d), F32),
        compiler_params=pltpu.CompilerParams(
            dimension_semantics=("arbitrary",), vmem_limit_bytes=VMEM_LIMIT),
    )(x2, attn, conv, rw, gates, gate_mod, w)


def _reorder_w_in(w_in):
    depth, d, _ = w_in.shape
    a_cols = 4 * D_A + N_IDX_HEADS * D_IDX + D_IDX + N_IDX_HEADS
    b0 = a_cols
    c0 = a_cols + 3 * D_B
    c_shift = 3 * D_C + 2 * D_LORA
    zeros = lambda m: jnp.zeros((depth, d, m), w_in.dtype)
    qi0 = 4 * D_A
    ki0 = qi0 + N_IDX_HEADS * D_IDX
    wi0 = ki0 + D_IDX
    parts = [w_in[..., 0:3 * D_A],
             w_in[..., qi0:ki0],
             w_in[..., ki0:wi0], w_in[..., ki0:wi0],
             w_in[..., wi0:a_cols], zeros(LANES - N_IDX_HEADS),
             w_in[..., 3 * D_A:4 * D_A], w_in[..., b0 + 2 * D_B:b0 + 3 * D_B],
             w_in[..., c0 + c_shift:c0 + c_shift + D_C],
             w_in[..., b0:b0 + 2 * D_B],
             w_in[..., c0:c0 + c_shift], zeros(RW_COLS - c_shift)]
    w = jnp.concatenate(parts, axis=-1)
    assert w.shape[-1] == N_COLS
    return w.astype(BF16)


def kernel(x, c, norm_g, w_ada, b_ada, w_in, w_out, q_norm_g, k_norm_g, rel_bias, conv_w, conv_b,
           conv_ln_g, conv_ln_b, shift_mu, decay_w0, decay_up, iclr_a0, iclr_up, key_k, key_a,
           bonus_r_k, lnx_g, lnx_b):
    bsz, seq, d = x.shape
    depth = w_in.shape[0]
    assert seq % (GROUP * TILE) == 0 and d == w_out.shape[-1]
    topk = min(TOPK_MAX, seq // 4)
    tm = GROUP * TILE
    conv_tt = 256

    mod = _modulation(c, w_ada, b_ada)
    shift = mod[:, :, None, 0:d]
    sc1p = 1.0 + mod[:, :, None, d:2 * d]
    gate_mod = mod[:, :, None, 2 * d:3 * d]

    w_all = _reorder_w_in(w_in)
    head = jnp.arange(D_A) // HEAD_DIM
    avg = (head[:, None] == head[None, :]).astype(BF16) * (1.0 / HEAD_DIM)
    qg = jnp.tile(q_norm_g, (1, H_A))[:, None, :] * (HEAD_DIM ** -0.5)
    kg = jnp.tile(k_norm_g, (1, H_A))[:, None, :]
    bias_tiles = _bias_tiles(rel_bias)
    c_shift = 3 * D_C + 2 * D_LORA
    mu_pad = jnp.pad(shift_mu, ((0, 0), (0, RW_COLS - c_shift)))[:, None, :]
    zl = jnp.zeros((depth, D_LORA, D_C), F32)
    lora = jnp.concatenate([jnp.concatenate([decay_up, zl], axis=2),
                            jnp.concatenate([zl, iclr_up], axis=2)], axis=1)
    row = lambda t: t[:, None, :]
    layers = dict(
        shift=shift, sc1p=sc1p, gate_mod=gate_mod, norm_g=row(norm_g), w=w_all, qg=qg, kg=kg,
        w_out=w_out.astype(BF16), conv_w=conv_w, conv_b=row(conv_b), conv_g=row(conv_ln_g),
        conv_beta=row(conv_ln_b), mu=mu_pad, lora=lora, w0=row(decay_w0), a0=row(iclr_a0),
        kk=row(key_k), ka=row(key_a), rk=row(bonus_r_k.reshape(depth, D_C)), lg=row(lnx_g),
        lb=row(lnx_b))

    def layer(x2, p):
        q, k, vt, qi, ki, wit, gates, conv_in, rw_in = _inproj(
            x2, p["sc1p"], p["shift"], p["norm_g"], p["w"], avg, p["qg"], p["kg"], seq, tm)
        r3 = lambda t: t.reshape(bsz, seq, t.shape[-1])
        attn = _attention(r3(q), r3(k), vt, r3(qi), r3(ki), wit, bias_tiles, topk)
        conv = _conv(r3(conv_in), p["conv_w"], p["conv_b"], p["conv_g"], p["conv_beta"], conv_tt)
        rwo = _rwkv(r3(rw_in), p["mu"], p["lora"], p["w0"], p["a0"], p["kk"], p["ka"], p["rk"],
                    p["lg"], p["lb"])
        x2 = _outproj(x2, attn.reshape(-1, D_A), conv.reshape(-1, D_B), rwo.reshape(-1, D_C),
                      gates, p["gate_mod"], p["w_out"], seq, tm)
        return x2

    x2 = x.reshape(bsz * seq, d)
    for l in range(depth):
        x2 = layer(x2, {name: t[l] for name, t in layers.items()})
    return x2.reshape(bsz, seq, d)
```

```python
import functools
import math

import jax
import jax.numpy as jnp
import numpy as np
from jax import lax
from jax.experimental import pallas as pl
from jax.experimental.pallas import tpu as pltpu

F32 = jnp.float32
BF16 = jnp.bfloat16

HEAD_DIM = 64
CHUNK = 64
EPS = 1e-6
H_A = 6
D_A = H_A * HEAD_DIM
N_IDX_HEADS = 8
D_IDX = 64
TOPK_MAX = 256
N_BUCKETS = 32
MAX_DISTANCE = 1024
D_B = 256
CONV_WIDTH = 31
H_C = 6
D_C = H_C * HEAD_DIM
D_LORA = 32
LNX_EPS = 64e-5
D_MIX = D_A + D_B + D_C

LANES = 128
TILE = 128
GROUP = 4
HALF = 1 << 15
DEN_FLOOR = 1e-30
LATE_STEPS = 6
ONES_ROWS = 16
RW_CHUNKS = 4
QW = 256
N_FAR = 9
N_BIAS = N_FAR + QW // TILE + 1
ATTN_VMEM_LIMIT = 60 * 1024 * 1024
VMEM_LIMIT = 48 * 1024 * 1024

C_Q, C_K, C_V = 0, 384, 768
C_QI = 1152
C_KI = 1664
C_WI = 1792
C_GATES = 1920
C_CONV = 2944
C_RW = 3456
N_COLS = 4736
RW_COLS = 1280

INT_MIN = -(2 ** 31)
INT_MAX = 2 ** 31 - 1
NEG_KEY = int(np.array(-np.inf, np.float32).view(np.int32)) ^ 0x7FFFFFFF
NEG_BIG = -1e30

NT_DIMS = (((1,), (1,)), ((), ()))


def _bdot(a, b):
    return jnp.dot(a.astype(BF16), b.astype(BF16), preferred_element_type=F32)


def _split2(x):
    hi = x.astype(BF16).astype(F32)
    return hi, x - hi


def _sigmoid(x):
    return 1.0 / (1.0 + jnp.exp(-x))


def _silu(x):
    return x * _sigmoid(x)


def _mod_kernel(c_ref, w_ref, b_ref, o_ref):
    ca = _silu(c_ref[...])
    o_ref[0] = jnp.dot(ca, w_ref[0], precision=lax.Precision.HIGHEST,
                       preferred_element_type=F32) + b_ref[0]


def _modulation(c, w_ada, b_ada):
    depth, d, d3 = w_ada.shape
    b = c.shape[0]
    return pl.pallas_call(
        _mod_kernel,
        grid=(depth, d3 // d),
        in_specs=[pl.BlockSpec((b, d), lambda l, j: (0, 0)),
                  pl.BlockSpec((1, d, d), lambda l, j: (l, 0, j)),
                  pl.BlockSpec((1, 1, d), lambda l, j: (l, 0, j))],
        out_specs=pl.BlockSpec((1, b, d), lambda l, j: (l, 0, j)),
        out_shape=jax.ShapeDtypeStruct((depth, b, d3), F32),
        compiler_params=pltpu.CompilerParams(
            dimension_semantics=("arbitrary", "arbitrary"), vmem_limit_bytes=VMEM_LIMIT),
    )(c, w_ada, b_ada.reshape(depth, 1, d3))


def _inproj_kernel(x_ref, sc_ref, sh_ref, g_ref, w_ref, avg_ref, qg_ref, kg_ref,
                   q_ref, k_ref, vt_ref, qi_ref, ki_ref, wit_ref, gates_ref, conv_ref, rw_ref):
    x = x_ref[...]
    ms = jnp.mean(x * x, axis=-1, keepdims=True)
    h = x * lax.rsqrt(ms + EPS) * g_ref[...]
    hb = (h * sc_ref[0] + sh_ref[0]).astype(BF16)

    def mm(c0, c1):
        return jnp.dot(hb, w_ref[:, c0:c1], preferred_element_type=F32)

    def head_rms(t, g):
        m2 = jnp.dot((t * t).astype(BF16), avg_ref[...], preferred_element_type=F32)
        return t * lax.rsqrt(m2 + EPS) * g

    q_ref[...] = head_rms(mm(C_Q, C_Q + D_A), qg_ref[...]).astype(BF16)
    k_ref[...] = head_rms(mm(C_K, C_K + D_A), kg_ref[...]).astype(BF16)
    vt_ref[0] = mm(C_V, C_V + D_A).T.astype(BF16)
    qi_ref[...] = mm(C_QI, C_QI + 512).astype(BF16)
    ki_ref[...] = mm(C_KI, C_KI + LANES).astype(BF16)
    wit_ref[...] = mm(C_WI, C_WI + LANES).T[:N_IDX_HEADS, :]
    gates_ref[:, 0:512] = mm(C_GATES, C_GATES + 512).astype(BF16)
    gates_ref[:, 512:1024] = mm(C_GATES + 512, C_GATES + 1024).astype(BF16)
    conv_ref[...] = mm(C_CONV, C_CONV + 512)
    rw_ref[:, 0:512] = mm(C_RW, C_RW + 512)
    rw_ref[:, 512:1024] = mm(C_RW + 512, C_RW + 1024)
    rw_ref[:, 1024:RW_COLS] = mm(C_RW + 1024, C_RW + RW_COLS)


def _inproj(x2, sc1p, shift, norm_g, w, avg, qg, kg, seq, tm):
    n, d = x2.shape
    per_b = seq // tm
    row = lambda i: (i, 0)
    full = lambda i: (0, 0)
    bat = lambda i: (i // per_b, 0, 0)
    rows = lambda wd, dt: (pl.BlockSpec((tm, wd), row), jax.ShapeDtypeStruct((n, wd), dt))
    outs = [rows(D_A, BF16), rows(D_A, BF16),
            (pl.BlockSpec((1, D_A, tm), lambda i: (i, 0, 0)),
             jax.ShapeDtypeStruct((n // tm, D_A, tm), BF16)),
            rows(512, BF16), rows(LANES, BF16),
            (pl.BlockSpec((N_IDX_HEADS, tm), lambda i: (0, i)),
             jax.ShapeDtypeStruct((N_IDX_HEADS, n), F32)),
            rows(D_MIX, BF16), rows(512, F32), rows(RW_COLS, F32)]
    return pl.pallas_call(
        _inproj_kernel,
        grid=(n // tm,),
        in_specs=[pl.BlockSpec((tm, d), row),
                  pl.BlockSpec((1, 1, d), bat),
                  pl.BlockSpec((1, 1, d), bat),
                  pl.BlockSpec((1, d), full),
                  pl.BlockSpec((d, N_COLS), full),
                  pl.BlockSpec((D_A, D_A), full),
                  pl.BlockSpec((1, D_A), full),
                  pl.BlockSpec((1, D_A), full)],
        out_specs=[o[0] for o in outs],
        out_shape=[o[1] for o in outs],
        compiler_params=pltpu.CompilerParams(
            dimension_semantics=("arbitrary",), vmem_limit_bytes=VMEM_LIMIT),
    )(x2, sc1p, shift, norm_g, w, avg, qg, kg)


def _fold(t, op):
    parts = [t[8 * i:8 * i + 8, :] for i in range(TILE // 8)]
    while len(parts) > 1:
        parts = [op(parts[i], parts[i + 1]) for i in range(0, len(parts), 2)]
    return parts[0]


def _attn_kernel(q_ref, k_ref, vt_ref, qi_ref, ki_ref, wit_ref, bias_ref, o_ref,
                 key_scr, khi_scr, klo_scr, klo2_scr, qih_scr, qh_scr, mb_scr, s_scr, m_scr, acc_scr,
                 *, topk, pos_bits):
    qb = pl.program_id(1)
    nst = ((qb + 1) * (QW // TILE) - 1) // GROUP + 1
    kf = float(topk)
    zgap = 1 << pos_bits
    lane = lax.broadcasted_iota(jnp.int32, (TILE, QW), 1)
    krow = lax.broadcasted_iota(jnp.int32, (TILE, QW), 0)
    lo_half = lax.broadcasted_iota(jnp.int32, (QW, LANES), 1) < HEAD_DIM

    wit = wit_ref[...]
    qi = qi_ref[0]
    q = q_ref[0]
    for src, dst, n_pairs in ((qi, qih_scr, N_IDX_HEADS // 2), (q, qh_scr, H_A // 2)):
        for g in range(n_pairs):
            pair = src[:, g * LANES:(g + 1) * LANES]
            dst[g, 0:QW, :] = jnp.where(lo_half, pair, jnp.zeros_like(pair))
            dst[g, QW:2 * QW, :] = jnp.where(lo_half, jnp.zeros_like(pair), pair)

    key_chunk = krow // CHUNK
    query_chunk = qb * (QW // CHUNK) + lane // CHUNK
    inadm = NEG_KEY - zgap - 1

    def score_key(s):
        b = lax.bitcast_convert_type(s, jnp.int32)
        return jnp.where(b < 0, (b ^ jnp.int32(0x7FFFFFFF)) - zgap, b)

    def p1(st, c):
        base = st * GROUP
        kis = ki_ref[0, pl.ds(pl.multiple_of(base * TILE, GROUP * TILE), GROUP * TILE), :]
        acc = [None] * GROUP
        for g in range(N_IDX_HEADS // 2):
            d = lax.dot_general(kis, qih_scr[g], NT_DIMS, preferred_element_type=F32)
            for par in range(2):
                w_row = wit[2 * g + par:2 * g + par + 1, :]
                for u in range(GROUP):
                    t = jnp.maximum(d[u * TILE:(u + 1) * TILE, par * QW:(par + 1) * QW], 0.0) * w_row
                    acc[u] = t if acc[u] is None else acc[u] + t
        for u in range(GROUP):
            kt = base + u
            key = jnp.where(acc[u] == 0.0, -1 - (krow + kt * TILE), score_key(acc[u]))
            ok = key_chunk + kt * (TILE // CHUNK) <= query_chunk
            key = jnp.where(ok, key, inadm)
            key_scr[kt] = key
            khi_scr[kt] = (key >> 16).astype(jnp.int16)
            klo_scr[kt] = ((key & (2 * HALF - 1)) - HALF).astype(jnp.int16)
        k_s = k_ref[0, pl.ds(pl.multiple_of(base * TILE, GROUP * TILE), GROUP * TILE), :]
        for g in range(H_A // 2):
            s = lax.dot_general(k_s[:, g * LANES:(g + 1) * LANES], qh_scr[g], NT_DIMS,
                                preferred_element_type=F32)
            for par in range(2):
                h = 2 * g + par
                mx = m_scr[h]
                for u in range(GROUP):
                    dd = jnp.clip(base + u - qb * (QW // TILE), -N_FAR, QW // TILE) + N_FAR
                    su = (s[u * TILE:(u + 1) * TILE, par * QW:(par + 1) * QW]
                          + bias_ref[h * N_BIAS + dd])
                    s_scr[h, base + u] = su
                    mx = jnp.maximum(mx, _fold(su, jnp.maximum))
                m_scr[h] = mx
        return c

    for h in range(H_A):
        m_scr[h] = jnp.full((8, QW), NEG_BIG, F32)
    lax.fori_loop(0, nst, p1, 0)

    def count_ge(thr_row):
        def body(st, cs):
            return tuple(c + _fold(jnp.where(key_scr[st * GROUP + u] >= thr_row, 1.0, 0.0), jnp.add)
                         for u, c in enumerate(cs))
        cs = lax.fori_loop(0, nst, body, (jnp.zeros((8, QW), F32),) * GROUP)
        return jnp.sum(functools.reduce(jnp.add, cs), axis=0, keepdims=True)

    def fold16(t):
        parts = [t[16 * i:16 * i + 16, :] for i in range(TILE // 16)]
        while len(parts) > 1:
            parts = [parts[i] + parts[i + 1] for i in range(0, len(parts), 2)]
        return parts[0]

    one16 = jnp.ones((TILE, QW), jnp.int16)
    zero16 = jnp.zeros((TILE, QW), jnp.int16)

    def as16(row):
        return jnp.broadcast_to(row, (TILE, QW)).astype(jnp.int16)

    def count16(scr, thr_row):
        thr16 = as16(thr_row)

        def body(st, cs):
            return tuple(c + fold16(jnp.where(scr[st * GROUP + u] >= thr16, one16, zero16))
                         for u, c in enumerate(cs))
        cs = lax.fori_loop(0, nst, body, (jnp.zeros((16, QW), jnp.int16),) * GROUP)
        total = functools.reduce(jnp.add, [c.astype(F32) for c in cs])
        return jnp.sum(total, axis=0, keepdims=True)

    def hi_step(_, carry):
        lo, hi, c_lo, c_hi = carry
        mid = (lo + hi) >> 1
        cnt = count16(khi_scr, mid)
        ge = cnt >= kf
        return (jnp.where(ge, mid, lo), jnp.where(ge, hi, mid),
                jnp.where(ge, cnt, c_lo), jnp.where(ge, c_hi, cnt))

    n_keys = (nst * (GROUP * TILE)).astype(F32)
    top, _, c_top, c_above = lax.fori_loop(
        0, 16, hi_step, (jnp.full((1, QW), -HALF, jnp.int32), jnp.full((1, QW), HALF, jnp.int32),
                         jnp.broadcast_to(n_keys, (1, QW)), jnp.zeros((1, QW), F32)))

    top16 = as16(top)
    min16 = jnp.full((TILE, QW), -HALF, jnp.int16)

    def keep_bucket(kt, c):
        klo2_scr[kt] = jnp.where(khi_scr[kt] == top16, klo_scr[kt], min16)
        return c

    lax.fori_loop(0, nst * GROUP, keep_bucket, 0)
    need = kf - c_above

    def lo_step(lo, hi, c_lo):
        mid = (lo + hi) >> 1
        cnt = count16(klo2_scr, mid)
        active = hi != lo + 1
        ge = jnp.logical_and(active, cnt >= need)
        hi = jnp.where(jnp.logical_and(active, cnt == need), mid + 1,
                       jnp.where(jnp.logical_and(active, cnt < need), mid, hi))
        return jnp.where(ge, mid, lo), hi, jnp.where(ge, cnt, c_lo)

    def unsettled(lo, hi):
        return jnp.max(jnp.where(hi != lo + 1, 1.0, 0.0)) > 0.0

    def late(carry):
        lo, hi, c_lo, it, _ = carry
        lo, hi, c_lo = lo_step(*lo_step(lo, hi, c_lo))
        return lo, hi, c_lo, it + 2, unsettled(lo, hi)

    low0 = (jnp.full((1, QW), -HALF, jnp.int32), jnp.full((1, QW), HALF, jnp.int32), c_top - c_above)
    low = lax.fori_loop(0, 16 - LATE_STEPS, lambda _, c: lo_step(*c), low0)
    low, _, c_low, _, _ = lax.while_loop(lambda carry: jnp.logical_and(carry[3] < 16, carry[4]),
                                         late, (*low, 16 - LATE_STEPS, unsettled(low[0], low[1])))
    thr = top * (2 * HALF) + (low + HALF)
    cnt_ge = c_above + c_low
    tie = jnp.logical_and(cnt_ge > kf, thr > inadm)

    @pl.when(jnp.max(jnp.where(tie, 1.0, 0.0)) > 0.0)
    def _():
        need = kf - count_ge(thr + 1)

        def count_eq_le(pmax):
            def body(kt, c):
                hit = jnp.where(key_scr[kt] == thr, krow + kt * TILE, INT_MAX) <= pmax
                return c + _fold(jnp.where(hit, 1.0, 0.0), jnp.add)
            c = lax.fori_loop(0, nst * GROUP, body, jnp.zeros((8, QW), F32))
            return jnp.sum(c, axis=0, keepdims=True)

        def pbisect(_, lohi):
            plo, phi = lohi
            pmid = (plo + phi) >> 1
            ok = count_eq_le(pmid) >= need
            return jnp.where(ok, plo, pmid), jnp.where(ok, pmid, phi)

        plo0 = jnp.full((1, QW), -1, jnp.int32)
        phi0 = jnp.full((1, QW), (1 << pos_bits) - 1, jnp.int32)
        _, pthr = lax.fori_loop(0, pos_bits + 1, pbisect, (plo0, phi0))
        pthr = jnp.where(tie, pthr, INT_MAX)

        def drop(kt, c):
            kk = key_scr[kt]
            pos = jnp.where(kk == thr, krow + kt * TILE, -1)
            key_scr[kt] = jnp.where(pos > pthr, inadm, kk)
            return c

        lax.fori_loop(0, nst * GROUP, drop, 0)

    thr_eff = jnp.maximum(thr, inadm + 1)

    ones_rows = jnp.ones((ONES_ROWS, GROUP * TILE), BF16)

    def collapse_max():
        for h in range(H_A):
            m_scr[h] = jnp.broadcast_to(jnp.max(m_scr[h], axis=0, keepdims=True), (8, QW))

    def select_bias(base):
        for u in range(GROUP):
            mb_scr[u] = jnp.where(key_scr[base + u] >= thr_eff, 0.0, NEG_BIG)

    def pv(st, c):
        select_bias(st * GROUP)
        for h in range(H_A):
            shift = m_scr[h][0:1, :]
            ps = [jnp.exp((s_scr[h, st * GROUP + u] + mb_scr[u] - shift).astype(BF16))
                  for u in range(GROUP)]
            lhs = jnp.concatenate([vt_ref[st, h * HEAD_DIM:(h + 1) * HEAD_DIM, :], ones_rows], axis=0)
            acc_scr[h] += jnp.dot(lhs, jnp.concatenate(ps, axis=0), preferred_element_type=F32)
        return c

    def run_pv():
        for h in range(H_A):
            acc_scr[h] = jnp.zeros((HEAD_DIM + ONES_ROWS, QW), F32)
        lax.fori_loop(0, nst, pv, 0)

    collapse_max()
    run_pv()
    den_min = acc_scr[0][HEAD_DIM:HEAD_DIM + 1, :]
    for h in range(1, H_A):
        den_min = jnp.minimum(den_min, acc_scr[h][HEAD_DIM:HEAD_DIM + 1, :])

    @pl.when(jnp.logical_not(jnp.min(den_min) > DEN_FLOOR))
    def _():
        for h in range(H_A):
            m_scr[h] = jnp.full((8, QW), NEG_BIG, F32)

        def selected_max(st, c):
            select_bias(st * GROUP)
            for h in range(H_A):
                mx = m_scr[h]
                for u in range(GROUP):
                    mx = jnp.maximum(mx, _fold(s_scr[h, st * GROUP + u] + mb_scr[u], jnp.maximum))
                m_scr[h] = mx
            return c

        lax.fori_loop(0, nst, selected_max, 0)
        collapse_max()
        run_pv()

    outs = []
    for h in range(H_A):
        acc = acc_scr[h]
        outs.append(acc[0:HEAD_DIM, :] / acc[HEAD_DIM:HEAD_DIM + 1, :])
    o_ref[0] = jnp.concatenate(outs, axis=0).T.astype(BF16)


def _attention(q, k, vt, qi, ki, wit, bias_tiles, topk):
    b, s, _ = q.shape
    nt = s // TILE
    nq = s // QW
    ng = s // (GROUP * TILE)
    kern = functools.partial(_attn_kernel, topk=topk, pos_bits=max(1, int(math.ceil(math.log2(s)))))
    qtile = lambda bb, i: (bb, i, 0)
    whole = lambda bb, i: (bb, 0, 0)
    once = pl.Buffered(1)
    return pl.pallas_call(
        kern,
        grid=(b, nq),
        in_specs=[pl.BlockSpec((1, QW, D_A), qtile),
                  pl.BlockSpec((1, s, D_A), whole, pipeline_mode=once),
                  pl.BlockSpec((ng, D_A, GROUP * TILE), whole, pipeline_mode=once),
                  pl.BlockSpec((1, QW, 512), qtile),
                  pl.BlockSpec((1, s, LANES), whole, pipeline_mode=once),
                  pl.BlockSpec((N_IDX_HEADS, QW), lambda bb, i: (0, bb * nq + i)),
                  pl.BlockSpec(bias_tiles.shape, lambda bb, i: (0, 0, 0), pipeline_mode=once)],
        out_specs=pl.BlockSpec((1, QW, D_A), qtile),
        out_shape=jax.ShapeDtypeStruct((b, s, D_A), BF16),
        scratch_shapes=[pltpu.VMEM((nt, TILE, QW), jnp.int32),
                        pltpu.VMEM((nt, TILE, QW), jnp.int16),
                        pltpu.VMEM((nt, TILE, QW), jnp.int16),
                        pltpu.VMEM((nt, TILE, QW), jnp.int16),
                        pltpu.VMEM((N_IDX_HEADS // 2, 2 * QW, LANES), BF16),
                        pltpu.VMEM((H_A // 2, 2 * QW, LANES), BF16),
                        pltpu.VMEM((GROUP, TILE, QW), F32),
                        pltpu.VMEM((H_A, nt, TILE, QW), F32),
                        pltpu.VMEM((H_A, 8, QW), F32),
                        pltpu.VMEM((H_A, HEAD_DIM + ONES_ROWS, QW), F32)],
        compiler_params=pltpu.CompilerParams(
            dimension_semantics=("arbitrary", "arbitrary"), vmem_limit_bytes=ATTN_VMEM_LIMIT),
    )(q, k, vt, qi, ki, wit, bias_tiles)


def _t5_bucket(rel):
    nb = N_BUCKETS // 2
    max_exact = nb // 2
    ret = jnp.where(rel > 0, nb, 0)
    n = jnp.abs(rel)
    nf = jnp.maximum(n, 1).astype(F32)
    large = max_exact + (jnp.log(nf / max_exact) / math.log(MAX_DISTANCE / max_exact)
                         * (nb - max_exact)).astype(jnp.int32)
    large = jnp.minimum(large, nb - 1)
    return ret + jnp.where(n < max_exact, n, large)


def _bias_tiles(rel_bias):
    n = TILE + QW - 1
    n_off = N_BIAS - 1
    m = np.arange(n)
    q_minus_k = np.where(m < QW, m, m - n)
    first_key = (np.arange(n_off)[:, None] - N_FAR) * TILE
    rel = first_key - q_minus_k[None, :]
    u = rel_bias[_t5_bucket(jnp.asarray(rel, jnp.int32))].astype(F32)
    u = u.transpose(2, 0, 1)
    flat = jnp.tile(u, (1, 1, TILE))[:, :, :TILE * (n - 1)]
    tiles = flat.reshape(H_A, n_off, TILE, n - 1)[..., :QW]
    j = np.arange(TILE)[None, :, None]
    t = np.arange(QW)[None, None, :]
    admissible = (first_key[:, :, None] + j) // CHUNK <= t // CHUNK
    tiles = jnp.where(jnp.asarray(admissible)[None], tiles, NEG_BIG)
    tiles = jnp.concatenate([tiles, jnp.full((H_A, 1, TILE, QW), NEG_BIG, F32)], axis=1)
    return tiles.reshape(H_A * N_BIAS, TILE, QW)


CONV_HALO = 32


def _conv_kernel(cur_ref, halo_ref, w_ref, b_ref, g_ref, beta_ref, o_ref, u_scr, sh_scr, *, tt):
    i = pl.program_id(1)

    def glu(t):
        return t[:, :D_B] * _sigmoid(t[:, D_B:])

    u_scr[0:CONV_HALO, :] = jnp.where(i > 0, glu(halo_ref[0]), 0.0)
    u_scr[CONV_HALO:CONV_HALO + tt, :] = glu(cur_ref[0])
    rows = 64
    first = CONV_HALO - (CONV_WIDTH - 1)
    for r0 in range(0, tt, rows):
        acc = jnp.broadcast_to(b_ref[...], (rows, D_B))
        for res in range(8):
            offs = [first + j for j in range(CONV_WIDTH) if (first + j) % 8 == res]
            span = offs[-1] - offs[0] + rows
            sh_scr[0:span, :] = u_scr[r0 + offs[0]:r0 + offs[0] + span, :]
            for off in offs:
                acc = acc + w_ref[off - first:off - first + 1, :] * sh_scr[off - offs[0]:off - offs[0] + rows, :]
        mu = jnp.mean(acc, axis=-1, keepdims=True)
        cen = acc - mu
        var = jnp.mean(cen * cen, axis=-1, keepdims=True)
        y = cen * lax.rsqrt(var + EPS) * g_ref[...] + beta_ref[...]
        o_ref[0, r0:r0 + rows, :] = _silu(y).astype(BF16)


def _conv(conv_in, w, b, g, beta, tt):
    bsz, s, _ = conv_in.shape
    per = tt // CONV_HALO
    vec = lambda bb, i: (0, 0)
    return pl.pallas_call(
        functools.partial(_conv_kernel, tt=tt),
        grid=(bsz, s // tt),
        in_specs=[pl.BlockSpec((1, tt, 2 * D_B), lambda bb, i: (bb, i, 0)),
                  pl.BlockSpec((1, CONV_HALO, 2 * D_B),
                               lambda bb, i: (bb, jnp.maximum(i * per - 1, 0), 0)),
                  pl.BlockSpec((CONV_WIDTH, D_B), vec),
                  pl.BlockSpec((1, D_B), vec),
                  pl.BlockSpec((1, D_B), vec),
                  pl.BlockSpec((1, D_B), vec)],
        out_specs=pl.BlockSpec((1, tt, D_B), lambda bb, i: (bb, i, 0)),
        out_shape=jax.ShapeDtypeStruct((bsz, s, D_B), BF16),
        scratch_shapes=[pltpu.VMEM((CONV_HALO + tt, D_B), F32),
                        pltpu.VMEM((CONV_HALO + 64, D_B), F32)],
        compiler_params=pltpu.CompilerParams(
            dimension_semantics=("arbitrary", "arbitrary"), vmem_limit_bytes=VMEM_LIMIT),
    )(conv_in, conv_in, w, b, g, beta)


def _rwkv_kernel(rw_ref, mu_ref, lora_ref, w0_ref, a0_ref, kk_ref, ka_ref, rk_ref, lg_ref, lb_ref,
                 o_ref, s_scr, prev_scr):
    c = pl.program_id(1)
    n = CHUNK
    nb = RW_CHUNKS * CHUNK

    @pl.when(c == 0)
    def _():
        s_scr[...] = jnp.zeros_like(s_scr)
        prev_scr[...] = jnp.zeros_like(prev_scr)

    ps = rw_ref[0]
    row = lax.broadcasted_iota(jnp.int32, (nb, 1), 0)
    prev = jnp.where(row == 0, prev_scr[0:1, :], pltpu.roll(ps, 1, axis=0))
    prev_scr[0:1, :] = ps[nb - 1:nb, :]
    xs = ps + mu_ref[...] * (prev - ps)
    r = xs[:, 0:D_C]
    k = xs[:, D_C:2 * D_C]
    v = xs[:, 2 * D_C:3 * D_C]
    dn = xs[:, 3 * D_C:3 * D_C + 2 * D_LORA]
    lane64 = lax.broadcasted_iota(jnp.int32, (nb, 2 * D_LORA), 1)
    dn = jnp.where(lane64 < D_LORA, jnp.tanh(dn), dn)
    dn_hi, dn_lo = _split2(dn)
    lora_hi, lora_lo = _split2(lora_ref[...])
    pre = _bdot(dn_hi, lora_hi) + _bdot(dn_hi, lora_lo) + _bdot(dn_lo, lora_hi)
    z = -(w0_ref[...] + pre[:, :D_C])
    softplus = jnp.maximum(z, 0.0) + jnp.log(1.0 + jnp.exp(-jnp.abs(z)))
    logdec = -jnp.exp(-softplus - 0.5)
    a = _sigmoid(a0_ref[...] + pre[:, D_C:])
    kkraw = k * kk_ref[...]
    k2 = k * (1.0 + (a - 1.0) * ka_ref[...])
    bonus_pre = r * k2 * rk_ref[...]

    step_in_chunk = row & (n - 1)
    cum_all = logdec
    shift = 1
    while shift < n:
        cum_all = cum_all + jnp.where(step_in_chunk >= shift, pltpu.roll(cum_all, shift, axis=0), 0.0)
        shift *= 2
    p_in_all = jnp.exp(cum_all)
    p_inv_all = jnp.exp(-cum_all)
    p_ex_all = jnp.exp(cum_all - logdec)

    row_n = lax.broadcasted_iota(jnp.int32, (n, LANES), 0)
    lane_n = lax.broadcasted_iota(jnp.int32, (n, LANES), 1)
    lo = lane_n < HEAD_DIM
    col = lane_n & (HEAD_DIM - 1)
    incl = col <= row_n
    strict = col < row_n
    mask2 = jnp.concatenate([strict, incl], axis=0)
    r2 = lax.broadcasted_iota(jnp.int32, (2 * n, LANES), 0)
    l2 = lax.broadcasted_iota(jnp.int32, (2 * n, LANES), 1)
    same_head = (r2 // HEAD_DIM) == (l2 // HEAD_DIM)

    def bd(t):
        tb = t.astype(BF16)
        z = jnp.zeros_like(tb)
        return jnp.concatenate([jnp.where(lo, tb, z), jnp.where(lo, z, tb)], axis=0)

    def hsum(t):
        s_lo = jnp.sum(jnp.where(lo, t, 0.0), axis=-1, keepdims=True)
        s_hi = jnp.sum(jnp.where(lo, 0.0, t), axis=-1, keepdims=True)
        return jnp.where(lo, s_lo, s_hi)

    pairs = range(H_C // 2)
    chains = [(j, g) for j in range(RW_CHUNKS) for g in pairs]
    ids = range(len(chains))
    sl = lambda t, i: t[chains[i][0] * n:(chains[i][0] + 1) * n,
                        chains[i][1] * LANES:(chains[i][1] + 1) * LANES]
    v_p = [sl(v, i) for i in ids]
    p_in = [sl(p_in_all, i) for i in ids]
    kkn = []
    for i in ids:
        kr = sl(kkraw, i)
        kkn.append(kr * lax.rsqrt(jnp.maximum(hsum(kr * kr), 1e-12)))
    rt = [sl(r, i) * p_in[i] for i in ids]
    at = [kkn[i] * sl(p_ex_all, i) for i in ids]
    kt = [sl(k2, i) * sl(p_inv_all, i) for i in ids]
    bt = [kkn[i] * sl(a, i) * sl(p_inv_all, i) for i in ids]
    lhs = [jnp.concatenate([at[i], rt[i]], axis=0).astype(BF16) for i in ids]
    ak = [lax.dot_general(lhs[i], bd(kt[i]), NT_DIMS, preferred_element_type=F32) for i in ids]
    ab = [lax.dot_general(lhs[i], bd(bt[i]), NT_DIMS, preferred_element_type=F32) for i in ids]
    av = [_bdot(jnp.where(mask2, ak[i], 0.0), bd(v_p[i])) for i in ids]
    a_rb = [jnp.where(incl, ab[i][n:], 0.0) for i in ids]
    npow = [jnp.where(strict, -ab[i][:n], 0.0) for i in ids]
    tm = list(npow)
    for _ in range(5):
        both = [_bdot(jnp.concatenate([npow[i], tm[i]], axis=0), bd(npow[i])) for i in ids]
        npow = [both[i][:n] for i in ids]
        tm = [tm[i] + both[i][n:] for i in ids]
    tm = [tm[i] + _bdot(tm[i], bd(npow[i])) for i in ids]
    wt = [at[i] + _bdot(tm[i], bd(at[i])) for i in ids]
    uloc = [av[i][:n] + _bdot(tm[i], bd(av[i][:n])) for i in ids]
    m_t = [jnp.concatenate([kt[i], bt[i]], axis=0).T.astype(BF16) for i in ids]
    eye = r2 == l2
    p_col = [jnp.sum(jnp.where(eye, p_in[i][n - 1:n, :], 0.0), axis=-1, keepdims=True) for i in ids]
    bonus = [hsum(sl(bonus_pre, i)) * v_p[i] for i in ids]

    state = [s_scr[g] for g in pairs]
    for j in range(RW_CHUNKS):
        cid = [j * (H_C // 2) + g for g in pairs]
        ws = [_bdot(jnp.concatenate([wt[i], rt[i]], axis=0), state[g])
              for g, i in zip(pairs, cid)]
        u = [ws[g][:n] + uloc[i] for g, i in zip(pairs, cid)]
        o = [ws[g][n:] + av[i][n:] - _bdot(a_rb[i], bd(u[g])) for g, i in zip(pairs, cid)]
        for g, i in zip(pairs, cid):
            ncat = jnp.concatenate([v_p[i], -u[g]], axis=0).astype(BF16)
            upd = jnp.dot(m_t[i], ncat, preferred_element_type=F32)
            state[g] = (state[g] + jnp.where(same_head, upd, 0.0)) * p_col[i]
        for g, i in zip(pairs, cid):
            lanes = slice(g * LANES, (g + 1) * LANES)
            cen = o[g] - hsum(o[g]) * (1.0 / HEAD_DIM)
            var = hsum(cen * cen) * (1.0 / HEAD_DIM)
            y = cen * lax.rsqrt(var + LNX_EPS) * lg_ref[:, lanes] + lb_ref[:, lanes]
            o_ref[0, j * n:(j + 1) * n, lanes] = (y + bonus[i]).astype(BF16)
    for g in pairs:
        s_scr[g] = state[g]


def _rwkv(rw, mu, lora, w0, a0, kk, ka, rk, lg, lb):
    bsz, s, _ = rw.shape
    vec = lambda bb, i: (0, 0)
    return pl.pallas_call(
        _rwkv_kernel,
        grid=(bsz, s // (RW_CHUNKS * CHUNK)),
        in_specs=[pl.BlockSpec((1, RW_CHUNKS * CHUNK, RW_COLS), lambda bb, i: (bb, i, 0)),
                  pl.BlockSpec((1, RW_COLS), vec),
                  pl.BlockSpec((2 * D_LORA, 2 * D_C), vec)] +
                 [pl.BlockSpec((1, D_C), vec)] * 7,
        out_specs=pl.BlockSpec((1, RW_CHUNKS * CHUNK, D_C), lambda bb, i: (bb, i, 0)),
        out_shape=jax.ShapeDtypeStruct((bsz, s, D_C), BF16),
        scratch_shapes=[pltpu.VMEM((H_C // 2, 2 * HEAD_DIM, LANES), F32),
                        pltpu.VMEM((8, RW_COLS), F32)],
        compiler_params=pltpu.CompilerParams(
            dimension_semantics=("arbitrary", "arbitrary"), vmem_limit_bytes=VMEM_LIMIT),
    )(rw, mu, lora, w0, a0, kk, ka, rk, lg, lb)


def _outproj_kernel(x_ref, attn_ref, conv_ref, rw_ref, gates_ref, gm_ref, w_ref, o_ref):
    sg = _silu(gates_ref[...].astype(F32))
    ya = (attn_ref[...].astype(F32) * sg[:, 0:D_A]).astype(BF16)
    yb = (conv_ref[...].astype(F32) * sg[:, D_A:D_A + D_B]).astype(BF16)
    yc = (rw_ref[...].astype(F32) * sg[:, D_A + D_B:D_MIX]).astype(BF16)
    y = (jnp.dot(ya, w_ref[0:D_A, :], preferred_element_type=F32)
         + jnp.dot(yb, w_ref[D_A:D_A + D_B, :], preferred_element_type=F32)
         + jnp.dot(yc, w_ref[D_A + D_B:D_MIX, :], preferred_element_type=F32))
    o_ref[...] = x_ref[...] + gm_ref[0] * y


def _outproj(x2, attn, conv, rw, gates, gate_mod, w, seq, tm):
    n, d = x2.shape
    per_b = seq // tm
    row = lambda i: (i, 0)
    return pl.pallas_call(
        _outproj_kernel,
        grid=(n // tm,),
        in_specs=[pl.BlockSpec((tm, d), row),
                  pl.BlockSpec((tm, D_A), row),
                  pl.BlockSpec((tm, D_B), row),
                  pl.BlockSpec((tm, D_C), row),
                  pl.BlockSpec((tm, D_MIX), row),
                  pl.BlockSpec((1, 1, d), lambda i: (i // per_b, 0, 0)),
                  pl.BlockSpec((D_MIX, d), lambda i: (0, 0))],
        out_specs=pl.BlockSpec((tm, d), row),
        out_shape=jax.ShapeDtypeStruct((n, d), F32),
        compiler_params=pltpu.CompilerParams(
            dimension_semantics=("arbitrary",), vmem_limit_bytes=VMEM_LIMIT),
    )(x2, attn, conv, rw, gates, gate_mod, w)


def _reorder_w_in(w_in):
    depth, d, _ = w_in.shape
    a_cols = 4 * D_A + N_IDX_HEADS * D_IDX + D_IDX + N_IDX_HEADS
    b0 = a_cols
    c0 = a_cols + 3 * D_B
    c_shift = 3 * D_C + 2 * D_LORA
    zeros = lambda m: jnp.zeros((depth, d, m), w_in.dtype)
    qi0 = 4 * D_A
    ki0 = qi0 + N_IDX_HEADS * D_IDX
    wi0 = ki0 + D_IDX
    parts = [w_in[..., 0:3 * D_A],
             w_in[..., qi0:ki0],
             w_in[..., ki0:wi0], w_in[..., ki0:wi0],
             w_in[..., wi0:a_cols], zeros(LANES - N_IDX_HEADS),
             w_in[..., 3 * D_A:4 * D_A], w_in[..., b0 + 2 * D_B:b0 + 3 * D_B],
             w_in[..., c0 + c_shift:c0 + c_shift + D_C],
             w_in[..., b0:b0 + 2 * D_B],
             w_in[..., c0:c0 + c_shift], zeros(RW_COLS - c_shift)]
    w = jnp.concatenate(parts, axis=-1)
    assert w.shape[-1] == N_COLS
    return w.astype(BF16)


def kernel(x, c, norm_g, w_ada, b_ada, w_in, w_out, q_norm_g, k_norm_g, rel_bias, conv_w, conv_b,
           conv_ln_g, conv_ln_b, shift_mu, decay_w0, decay_up, iclr_a0, iclr_up, key_k, key_a,
           bonus_r_k, lnx_g, lnx_b):
    bsz, seq, d = x.shape
    depth = w_in.shape[0]
    assert seq % (GROUP * TILE) == 0 and d == w_out.shape[-1]
    topk = min(TOPK_MAX, seq // 4)
    tm = GROUP * TILE
    conv_tt = 256

    mod = _modulation(c, w_ada, b_ada)
    shift = mod[:, :, None, 0:d]
    sc1p = 1.0 + mod[:, :, None, d:2 * d]
    gate_mod = mod[:, :, None, 2 * d:3 * d]

    w_all = _reorder_w_in(w_in)
    head = jnp.arange(D_A) // HEAD_DIM
    avg = (head[:, None] == head[None, :]).astype(BF16) * (1.0 / HEAD_DIM)
    qg = jnp.tile(q_norm_g, (1, H_A))[:, None, :] * (HEAD_DIM ** -0.5)
    kg = jnp.tile(k_norm_g, (1, H_A))[:, None, :]
    bias_tiles = _bias_tiles(rel_bias)
    c_shift = 3 * D_C + 2 * D_LORA
    mu_pad = jnp.pad(shift_mu, ((0, 0), (0, RW_COLS - c_shift)))[:, None, :]
    zl = jnp.zeros((depth, D_LORA, D_C), F32)
    lora = jnp.concatenate([jnp.concatenate([decay_up, zl], axis=2),
                            jnp.concatenate([zl, iclr_up], axis=2)], axis=1)
    row = lambda t: t[:, None, :]
    layers = dict(
        shift=shift, sc1p=sc1p, gate_mod=gate_mod, norm_g=row(norm_g), w=w_all, qg=qg, kg=kg,
        w_out=w_out.astype(BF16), conv_w=conv_w, conv_b=row(conv_b), conv_g=row(conv_ln_g),
        conv_beta=row(conv_ln_b), mu=mu_pad, lora=lora, w0=row(decay_w0), a0=row(iclr_a0),
        kk=row(key_k), ka=row(key_a), rk=row(bonus_r_k.reshape(depth, D_C)), lg=row(lnx_g),
        lb=row(lnx_b))

    def layer(x2, p):
        q, k, vt, qi, ki, wit, gates, conv_in, rw_in = _inproj(
            x2, p["sc1p"], p["shift"], p["norm_g"], p["w"], avg, p["qg"], p["kg"], seq, tm)
        r3 = lambda t: t.reshape(bsz, seq, t.shape[-1])
        attn = _attention(r3(q), r3(k), vt, r3(qi), r3(ki), wit, bias_tiles, topk)
        conv = _conv(r3(conv_in), p["conv_w"], p["conv_b"], p["conv_g"], p["conv_beta"], conv_tt)
        rwo = _rwkv(r3(rw_in), p["mu"], p["lora"], p["w0"], p["a0"], p["kk"], p["ka"], p["rk"],
                    p["lg"], p["lb"])
        x2 = _outproj(x2, attn.reshape(-1, D_A), conv.reshape(-1, D_B), rwo.reshape(-1, D_C),
                      gates, p["gate_mod"], p["w_out"], seq, tm)
        return x2

    x2 = x.reshape(bsz * seq, d)
    for l in range(depth):
        x2 = layer(x2, {name: t[l] for name, t in layers.items()})
    return x2.reshape(bsz, seq, d)
```

```python
import functools
import math

import jax
import jax.numpy as jnp
import numpy as np
from jax import lax
from jax.experimental import pallas as pl
from jax.experimental.pallas import tpu as pltpu

F32 = jnp.float32
BF16 = jnp.bfloat16

HEAD_DIM = 64
CHUNK = 64
EPS = 1e-6
H_A = 6
D_A = H_A * HEAD_DIM
N_IDX_HEADS = 8
D_IDX = 64
TOPK_MAX = 256
N_BUCKETS = 32
MAX_DISTANCE = 1024
D_B = 256
CONV_WIDTH = 31
H_C = 6
D_C = H_C * HEAD_DIM
D_LORA = 32
LNX_EPS = 64e-5
D_MIX = D_A + D_B + D_C

LANES = 128
TILE = 128
GROUP = 4
HALF = 1 << 15
DEN_FLOOR = 1e-30
LATE_STEPS = 6
ONES_ROWS = 16
RW_CHUNKS = 4
QW = 256
N_FAR = 9
N_BIAS = N_FAR + QW // TILE + 1
ATTN_VMEM_LIMIT = 60 * 1024 * 1024
VMEM_LIMIT = 48 * 1024 * 1024

C_Q, C_K, C_V = 0, 384, 768
C_QI = 1152
C_KI = 1664
C_WI = 1792
C_GATES = 1920
C_CONV = 2944
C_RW = 3456
N_COLS = 4736
RW_COLS = 1280

INT_MIN = -(2 ** 31)
INT_MAX = 2 ** 31 - 1
NEG_KEY = int(np.array(-np.inf, np.float32).view(np.int32)) ^ 0x7FFFFFFF
NEG_BIG = -1e30

NT_DIMS = (((1,), (1,)), ((), ()))


def _bdot(a, b):
    return jnp.dot(a.astype(BF16), b.astype(BF16), preferred_element_type=F32)


def _split2(x):
    hi = x.astype(BF16).astype(F32)
    return hi, x - hi


def _sigmoid(x):
    return 1.0 / (1.0 + jnp.exp(-x))


def _silu(x):
    return x * _sigmoid(x)


def _mod_kernel(c_ref, w_ref, b_ref, o_ref):
    ca = _silu(c_ref[...])
    o_ref[0] = jnp.dot(ca, w_ref[0], precision=lax.Precision.HIGHEST,
                       preferred_element_type=F32) + b_ref[0]


def _modulation(c, w_ada, b_ada):
    depth, d, d3 = w_ada.shape
    b = c.shape[0]
    return pl.pallas_call(
        _mod_kernel,
        grid=(depth, d3 // d),
        in_specs=[pl.BlockSpec((b, d), lambda l, j: (0, 0)),
                  pl.BlockSpec((1, d, d), lambda l, j: (l, 0, j)),
                  pl.BlockSpec((1, 1, d), lambda l, j: (l, 0, j))],
        out_specs=pl.BlockSpec((1, b, d), lambda l, j: (l, 0, j)),
        out_shape=jax.ShapeDtypeStruct((depth, b, d3), F32),
        compiler_params=pltpu.CompilerParams(
            dimension_semantics=("arbitrary", "arbitrary"), vmem_limit_bytes=VMEM_LIMIT),
    )(c, w_ada, b_ada.reshape(depth, 1, d3))


def _inproj_kernel(x_ref, sc_ref, sh_ref, g_ref, w_ref, qg_ref, kg_ref,
                   q_ref, k_ref, vt_ref, qi_ref, ki_ref, wit_ref, gates_ref, conv_ref, rw_ref):
    x = x_ref[...]
    ms = jnp.mean(x * x, axis=-1, keepdims=True)
    h = x * lax.rsqrt(ms + EPS) * g_ref[...]
    hb = (h * sc_ref[0] + sh_ref[0]).astype(BF16)

    def mm(c0, c1):
        return jnp.dot(hb, w_ref[:, c0:c1], preferred_element_type=F32)

    lo = lax.broadcasted_iota(jnp.int32, (x.shape[0], LANES), 1) < HEAD_DIM

    def head_rms(t, g):
        sq = t * t
        means = []
        for grp in range(D_A // LANES):
            blk = sq[:, grp * LANES:(grp + 1) * LANES]
            s_lo = jnp.sum(jnp.where(lo, blk, 0.0), axis=-1, keepdims=True)
            s_hi = jnp.sum(jnp.where(lo, 0.0, blk), axis=-1, keepdims=True)
            means.append(jnp.where(lo, s_lo, s_hi) * (1.0 / HEAD_DIM))
        return t * lax.rsqrt(jnp.concatenate(means, axis=1) + EPS) * g

    q_ref[...] = head_rms(mm(C_Q, C_Q + D_A), qg_ref[...]).astype(BF16)
    k_ref[...] = head_rms(mm(C_K, C_K + D_A), kg_ref[...]).astype(BF16)
    vt_ref[0] = mm(C_V, C_V + D_A).T.astype(BF16)
    qi_ref[...] = mm(C_QI, C_QI + 512).astype(BF16)
    ki_ref[...] = mm(C_KI, C_KI + LANES).astype(BF16)
    wit_ref[...] = mm(C_WI, C_WI + LANES).T[:N_IDX_HEADS, :]
    gates_ref[:, 0:512] = mm(C_GATES, C_GATES + 512).astype(BF16)
    gates_ref[:, 512:1024] = mm(C_GATES + 512, C_GATES + 1024).astype(BF16)
    conv_ref[...] = mm(C_CONV, C_CONV + 512)
    rw_ref[:, 0:512] = mm(C_RW, C_RW + 512)
    rw_ref[:, 512:1024] = mm(C_RW + 512, C_RW + 1024)
    rw_ref[:, 1024:RW_COLS] = mm(C_RW + 1024, C_RW + RW_COLS)


def _inproj(x2, sc1p, shift, norm_g, w, qg, kg, seq, tm):
    n, d = x2.shape
    per_b = seq // tm
    row = lambda i: (i, 0)
    full = lambda i: (0, 0)
    bat = lambda i: (i // per_b, 0, 0)
    rows = lambda wd, dt: (pl.BlockSpec((tm, wd), row), jax.ShapeDtypeStruct((n, wd), dt))
    outs = [rows(D_A, BF16), rows(D_A, BF16),
            (pl.BlockSpec((1, D_A, tm), lambda i: (i, 0, 0)),
             jax.ShapeDtypeStruct((n // tm, D_A, tm), BF16)),
            rows(512, BF16), rows(LANES, BF16),
            (pl.BlockSpec((N_IDX_HEADS, tm), lambda i: (0, i)),
             jax.ShapeDtypeStruct((N_IDX_HEADS, n), F32)),
            rows(D_MIX, BF16), rows(512, F32), rows(RW_COLS, F32)]
    return pl.pallas_call(
        _inproj_kernel,
        grid=(n // tm,),
        in_specs=[pl.BlockSpec((tm, d), row),
                  pl.BlockSpec((1, 1, d), bat),
                  pl.BlockSpec((1, 1, d), bat),
                  pl.BlockSpec((1, d), full),
                  pl.BlockSpec((d, N_COLS), full),
                  pl.BlockSpec((1, D_A), full),
                  pl.BlockSpec((1, D_A), full)],
        out_specs=[o[0] for o in outs],
        out_shape=[o[1] for o in outs],
        compiler_params=pltpu.CompilerParams(
            dimension_semantics=("arbitrary",), vmem_limit_bytes=VMEM_LIMIT),
    )(x2, sc1p, shift, norm_g, w, qg, kg)


def _fold(t, op):
    parts = [t[8 * i:8 * i + 8, :] for i in range(TILE // 8)]
    while len(parts) > 1:
        parts = [op(parts[i], parts[i + 1]) for i in range(0, len(parts), 2)]
    return parts[0]


def _attn_kernel(q_ref, k_ref, vt_ref, qi_ref, ki_ref, wit_ref, bias_ref, o_ref,
                 key_scr, khi_scr, klo_scr, klo2_scr, qih_scr, qh_scr, mb_scr, s_scr, m_scr, acc_scr,
                 *, topk, pos_bits):
    qb = pl.program_id(1)
    nst = ((qb + 1) * (QW // TILE) - 1) // GROUP + 1
    kf = float(topk)
    zgap = 1 << pos_bits
    lane = lax.broadcasted_iota(jnp.int32, (TILE, QW), 1)
    krow = lax.broadcasted_iota(jnp.int32, (TILE, QW), 0)
    lo_half = lax.broadcasted_iota(jnp.int32, (QW, LANES), 1) < HEAD_DIM

    wit = wit_ref[...]
    qi = qi_ref[0]
    q = q_ref[0]
    for src, dst, n_pairs in ((qi, qih_scr, N_IDX_HEADS // 2), (q, qh_scr, H_A // 2)):
        for g in range(n_pairs):
            pair = src[:, g * LANES:(g + 1) * LANES]
            dst[g, 0:QW, :] = jnp.where(lo_half, pair, jnp.zeros_like(pair))
            dst[g, QW:2 * QW, :] = jnp.where(lo_half, jnp.zeros_like(pair), pair)

    key_chunk = krow // CHUNK
    query_chunk = qb * (QW // CHUNK) + lane // CHUNK
    inadm = NEG_KEY - zgap - 1

    def score_key(s):
        b = lax.bitcast_convert_type(s, jnp.int32)
        return jnp.where(b < 0, (b ^ jnp.int32(0x7FFFFFFF)) - zgap, b)

    def p1(st, c):
        base = st * GROUP
        kis = ki_ref[0, pl.ds(pl.multiple_of(base * TILE, GROUP * TILE), GROUP * TILE), :]
        acc = [None] * GROUP
        for g in range(N_IDX_HEADS // 2):
            d = lax.dot_general(kis, qih_scr[g], NT_DIMS, preferred_element_type=F32)
            for par in range(2):
                w_row = wit[2 * g + par:2 * g + par + 1, :]
                for u in range(GROUP):
                    t = jnp.maximum(d[u * TILE:(u + 1) * TILE, par * QW:(par + 1) * QW], 0.0) * w_row
                    acc[u] = t if acc[u] is None else acc[u] + t
        for u in range(GROUP):
            kt = base + u
            key = jnp.where(acc[u] == 0.0, -1 - (krow + kt * TILE), score_key(acc[u]))
            ok = key_chunk + kt * (TILE // CHUNK) <= query_chunk
            key = jnp.where(ok, key, inadm)
            key_scr[kt] = key
            khi_scr[kt] = (key >> 16).astype(jnp.int16)
            klo_scr[kt] = ((key & (2 * HALF - 1)) - HALF).astype(jnp.int16)
        k_s = k_ref[0, pl.ds(pl.multiple_of(base * TILE, GROUP * TILE), GROUP * TILE), :]
        for g in range(H_A // 2):
            s = lax.dot_general(k_s[:, g * LANES:(g + 1) * LANES], qh_scr[g], NT_DIMS,
                                preferred_element_type=F32)
            for par in range(2):
                h = 2 * g + par
                mx = m_scr[h]
                for u in range(GROUP):
                    dd = jnp.clip(base + u - qb * (QW // TILE), -N_FAR, QW // TILE) + N_FAR
                    su = (s[u * TILE:(u + 1) * TILE, par * QW:(par + 1) * QW]
                          + bias_ref[h * N_BIAS + dd])
                    s_scr[h, base + u] = su
                    mx = jnp.maximum(mx, _fold(su, jnp.maximum))
                m_scr[h] = mx
        return c

    for h in range(H_A):
        m_scr[h] = jnp.full((8, QW), NEG_BIG, F32)
    lax.fori_loop(0, nst, p1, 0)

    def count_ge(thr_row):
        def body(st, cs):
            return tuple(c + _fold(jnp.where(key_scr[st * GROUP + u] >= thr_row, 1.0, 0.0), jnp.add)
                         for u, c in enumerate(cs))
        cs = lax.fori_loop(0, nst, body, (jnp.zeros((8, QW), F32),) * GROUP)
        return jnp.sum(functools.reduce(jnp.add, cs), axis=0, keepdims=True)

    def fold16(t):
        parts = [t[16 * i:16 * i + 16, :] for i in range(TILE // 16)]
        while len(parts) > 1:
            parts = [parts[i] + parts[i + 1] for i in range(0, len(parts), 2)]
        return parts[0]

    one16 = jnp.ones((TILE, QW), jnp.int16)
    zero16 = jnp.zeros((TILE, QW), jnp.int16)

    def as16(row):
        return jnp.broadcast_to(row, (TILE, QW)).astype(jnp.int16)

    def count16(scr, thr_row):
        thr16 = as16(thr_row)

        def body(st, cs):
            return tuple(c + fold16(jnp.where(scr[st * GROUP + u] >= thr16, one16, zero16))
                         for u, c in enumerate(cs))
        cs = lax.fori_loop(0, nst, body, (jnp.zeros((16, QW), jnp.int16),) * GROUP)
        total = functools.reduce(jnp.add, [c.astype(F32) for c in cs])
        return jnp.sum(total, axis=0, keepdims=True)

    def hi_step(_, carry):
        lo, hi, c_lo, c_hi = carry
        mid = (lo + hi) >> 1
        cnt = count16(khi_scr, mid)
        ge = cnt >= kf
        return (jnp.where(ge, mid, lo), jnp.where(ge, hi, mid),
                jnp.where(ge, cnt, c_lo), jnp.where(ge, c_hi, cnt))

    n_keys = (nst * (GROUP * TILE)).astype(F32)
    top, _, c_top, c_above = lax.fori_loop(
        0, 16, hi_step, (jnp.full((1, QW), -HALF, jnp.int32), jnp.full((1, QW), HALF, jnp.int32),
                         jnp.broadcast_to(n_keys, (1, QW)), jnp.zeros((1, QW), F32)))

    top16 = as16(top)
    min16 = jnp.full((TILE, QW), -HALF, jnp.int16)

    def keep_bucket(kt, c):
        klo2_scr[kt] = jnp.where(khi_scr[kt] == top16, klo_scr[kt], min16)
        return c

    lax.fori_loop(0, nst * GROUP, keep_bucket, 0)
    need = kf - c_above

    def lo_step(lo, hi, c_lo):
        mid = (lo + hi) >> 1
        cnt = count16(klo2_scr, mid)
        active = hi != lo + 1
        ge = jnp.logical_and(active, cnt >= need)
        hi = jnp.where(jnp.logical_and(active, cnt == need), mid + 1,
                       jnp.where(jnp.logical_and(active, cnt < need), mid, hi))
        return jnp.where(ge, mid, lo), hi, jnp.where(ge, cnt, c_lo)

    def unsettled(lo, hi):
        return jnp.max(jnp.where(hi != lo + 1, 1.0, 0.0)) > 0.0

    def late(carry):
        lo, hi, c_lo, it, _ = carry
        lo, hi, c_lo = lo_step(*lo_step(lo, hi, c_lo))
        return lo, hi, c_lo, it + 2, unsettled(lo, hi)

    low0 = (jnp.full((1, QW), -HALF, jnp.int32), jnp.full((1, QW), HALF, jnp.int32), c_top - c_above)
    low = lax.fori_loop(0, 16 - LATE_STEPS, lambda _, c: lo_step(*c), low0)
    low, _, c_low, _, _ = lax.while_loop(lambda carry: jnp.logical_and(carry[3] < 16, carry[4]),
                                         late, (*low, 16 - LATE_STEPS, unsettled(low[0], low[1])))
    thr = top * (2 * HALF) + (low + HALF)
    cnt_ge = c_above + c_low
    tie = jnp.logical_and(cnt_ge > kf, thr > inadm)

    @pl.when(jnp.max(jnp.where(tie, 1.0, 0.0)) > 0.0)
    def _():
        need = kf - count_ge(thr + 1)

        def count_eq_le(pmax):
            def body(kt, c):
                hit = jnp.where(key_scr[kt] == thr, krow + kt * TILE, INT_MAX) <= pmax
                return c + _fold(jnp.where(hit, 1.0, 0.0), jnp.add)
            c = lax.fori_loop(0, nst * GROUP, body, jnp.zeros((8, QW), F32))
            return jnp.sum(c, axis=0, keepdims=True)

        def pbisect(_, lohi):
            plo, phi = lohi
            pmid = (plo + phi) >> 1
            ok = count_eq_le(pmid) >= need
            return jnp.where(ok, plo, pmid), jnp.where(ok, pmid, phi)

        plo0 = jnp.full((1, QW), -1, jnp.int32)
        phi0 = jnp.full((1, QW), (1 << pos_bits) - 1, jnp.int32)
        _, pthr = lax.fori_loop(0, pos_bits + 1, pbisect, (plo0, phi0))
        pthr = jnp.where(tie, pthr, INT_MAX)

        def drop(kt, c):
            kk = key_scr[kt]
            pos = jnp.where(kk == thr, krow + kt * TILE, -1)
            key_scr[kt] = jnp.where(pos > pthr, inadm, kk)
            return c

        lax.fori_loop(0, nst * GROUP, drop, 0)

    thr_eff = jnp.maximum(thr, inadm + 1)

    ones_rows = jnp.ones((ONES_ROWS, GROUP * TILE), BF16)

    def collapse_max():
        for h in range(H_A):
            m_scr[h] = jnp.broadcast_to(jnp.max(m_scr[h], axis=0, keepdims=True), (8, QW))

    def select_bias(base):
        for u in range(GROUP):
            mb_scr[u] = jnp.where(key_scr[base + u] >= thr_eff, 0.0, NEG_BIG)

    def pv(st, c):
        select_bias(st * GROUP)
        for h in range(H_A):
            shift = m_scr[h][0:1, :]
            ps = [jnp.exp((s_scr[h, st * GROUP + u] + mb_scr[u] - shift).astype(BF16))
                  for u in range(GROUP)]
            lhs = jnp.concatenate([vt_ref[st, h * HEAD_DIM:(h + 1) * HEAD_DIM, :], ones_rows], axis=0)
            acc_scr[h] += jnp.dot(lhs, jnp.concatenate(ps, axis=0), preferred_element_type=F32)
        return c

    def run_pv():
        for h in range(H_A):
            acc_scr[h] = jnp.zeros((HEAD_DIM + ONES_ROWS, QW), F32)
        lax.fori_loop(0, nst, pv, 0)

    collapse_max()
    run_pv()
    den_min = acc_scr[0][HEAD_DIM:HEAD_DIM + 1, :]
    for h in range(1, H_A):
        den_min = jnp.minimum(den_min, acc_scr[h][HEAD_DIM:HEAD_DIM + 1, :])

    @pl.when(jnp.logical_not(jnp.min(den_min) > DEN_FLOOR))
    def _():
        for h in range(H_A):
            m_scr[h] = jnp.full((8, QW), NEG_BIG, F32)

        def selected_max(st, c):
            select_bias(st * GROUP)
            for h in range(H_A):
                mx = m_scr[h]
                for u in range(GROUP):
                    mx = jnp.maximum(mx, _fold(s_scr[h, st * GROUP + u] + mb_scr[u], jnp.maximum))
                m_scr[h] = mx
            return c

        lax.fori_loop(0, nst, selected_max, 0)
        collapse_max()
        run_pv()

    outs = []
    for h in range(H_A):
        acc = acc_scr[h]
        outs.append(acc[0:HEAD_DIM, :] / acc[HEAD_DIM:HEAD_DIM + 1, :])
    o_ref[0] = jnp.concatenate(outs, axis=0).T.astype(BF16)


def _attention(q, k, vt, qi, ki, wit, bias_tiles, topk):
    b, s, _ = q.shape
    nt = s // TILE
    nq = s // QW
    ng = s // (GROUP * TILE)
    kern = functools.partial(_attn_kernel, topk=topk, pos_bits=max(1, int(math.ceil(math.log2(s)))))
    qtile = lambda bb, i: (bb, i, 0)
    whole = lambda bb, i: (bb, 0, 0)
    once = pl.Buffered(1)
    return pl.pallas_call(
        kern,
        grid=(b, nq),
        in_specs=[pl.BlockSpec((1, QW, D_A), qtile),
                  pl.BlockSpec((1, s, D_A), whole, pipeline_mode=once),
                  pl.BlockSpec((ng, D_A, GROUP * TILE), whole, pipeline_mode=once),
                  pl.BlockSpec((1, QW, 512), qtile),
                  pl.BlockSpec((1, s, LANES), whole, pipeline_mode=once),
                  pl.BlockSpec((N_IDX_HEADS, QW), lambda bb, i: (0, bb * nq + i)),
                  pl.BlockSpec(bias_tiles.shape, lambda bb, i: (0, 0, 0), pipeline_mode=once)],
        out_specs=pl.BlockSpec((1, QW, D_A), qtile),
        out_shape=jax.ShapeDtypeStruct((b, s, D_A), BF16),
        scratch_shapes=[pltpu.VMEM((nt, TILE, QW), jnp.int32),
                        pltpu.VMEM((nt, TILE, QW), jnp.int16),
                        pltpu.VMEM((nt, TILE, QW), jnp.int16),
                        pltpu.VMEM((nt, TILE, QW), jnp.int16),
                        pltpu.VMEM((N_IDX_HEADS // 2, 2 * QW, LANES), BF16),
                        pltpu.VMEM((H_A // 2, 2 * QW, LANES), BF16),
                        pltpu.VMEM((GROUP, TILE, QW), F32),
                        pltpu.VMEM((H_A, nt, TILE, QW), F32),
                        pltpu.VMEM((H_A, 8, QW), F32),
                        pltpu.VMEM((H_A, HEAD_DIM + ONES_ROWS, QW), F32)],
        compiler_params=pltpu.CompilerParams(
            dimension_semantics=("arbitrary", "arbitrary"), vmem_limit_bytes=ATTN_VMEM_LIMIT),
    )(q, k, vt, qi, ki, wit, bias_tiles)


def _t5_bucket(rel):
    nb = N_BUCKETS // 2
    max_exact = nb // 2
    ret = jnp.where(rel > 0, nb, 0)
    n = jnp.abs(rel)
    nf = jnp.maximum(n, 1).astype(F32)
    large = max_exact + (jnp.log(nf / max_exact) / math.log(MAX_DISTANCE / max_exact)
                         * (nb - max_exact)).astype(jnp.int32)
    large = jnp.minimum(large, nb - 1)
    return ret + jnp.where(n < max_exact, n, large)


def _bias_tiles(rel_bias):
    n = TILE + QW - 1
    n_off = N_BIAS - 1
    m = np.arange(n)
    q_minus_k = np.where(m < QW, m, m - n)
    first_key = (np.arange(n_off)[:, None] - N_FAR) * TILE
    rel = first_key - q_minus_k[None, :]
    u = rel_bias[_t5_bucket(jnp.asarray(rel, jnp.int32))].astype(F32)
    u = u.transpose(2, 0, 1)
    flat = jnp.tile(u, (1, 1, TILE))[:, :, :TILE * (n - 1)]
    tiles = flat.reshape(H_A, n_off, TILE, n - 1)[..., :QW]
    j = np.arange(TILE)[None, :, None]
    t = np.arange(QW)[None, None, :]
    admissible = (first_key[:, :, None] + j) // CHUNK <= t // CHUNK
    tiles = jnp.where(jnp.asarray(admissible)[None], tiles, NEG_BIG)
    tiles = jnp.concatenate([tiles, jnp.full((H_A, 1, TILE, QW), NEG_BIG, F32)], axis=1)
    return tiles.reshape(H_A * N_BIAS, TILE, QW)


CONV_HALO = 32


def _conv_kernel(cur_ref, halo_ref, w_ref, b_ref, g_ref, beta_ref, o_ref, u_scr, sh_scr, *, tt):
    i = pl.program_id(1)

    def glu(t):
        return t[:, :D_B] * _sigmoid(t[:, D_B:])

    u_scr[0:CONV_HALO, :] = jnp.where(i > 0, glu(halo_ref[0]), 0.0)
    u_scr[CONV_HALO:CONV_HALO + tt, :] = glu(cur_ref[0])
    rows = 64
    first = CONV_HALO - (CONV_WIDTH - 1)
    for r0 in range(0, tt, rows):
        acc = jnp.broadcast_to(b_ref[...], (rows, D_B))
        for res in range(8):
            offs = [first + j for j in range(CONV_WIDTH) if (first + j) % 8 == res]
            span = offs[-1] - offs[0] + rows
            sh_scr[0:span, :] = u_scr[r0 + offs[0]:r0 + offs[0] + span, :]
            for off in offs:
                acc = acc + w_ref[off - first:off - first + 1, :] * sh_scr[off - offs[0]:off - offs[0] + rows, :]
        mu = jnp.mean(acc, axis=-1, keepdims=True)
        cen = acc - mu
        var = jnp.mean(cen * cen, axis=-1, keepdims=True)
        y = cen * lax.rsqrt(var + EPS) * g_ref[...] + beta_ref[...]
        o_ref[0, r0:r0 + rows, :] = _silu(y).astype(BF16)


def _conv(conv_in, w, b, g, beta, tt):
    bsz, s, _ = conv_in.shape
    per = tt // CONV_HALO
    vec = lambda bb, i: (0, 0)
    return pl.pallas_call(
        functools.partial(_conv_kernel, tt=tt),
        grid=(bsz, s // tt),
        in_specs=[pl.BlockSpec((1, tt, 2 * D_B), lambda bb, i: (bb, i, 0)),
                  pl.BlockSpec((1, CONV_HALO, 2 * D_B),
                               lambda bb, i: (bb, jnp.maximum(i * per - 1, 0), 0)),
                  pl.BlockSpec((CONV_WIDTH, D_B), vec),
                  pl.BlockSpec((1, D_B), vec),
                  pl.BlockSpec((1, D_B), vec),
                  pl.BlockSpec((1, D_B), vec)],
        out_specs=pl.BlockSpec((1, tt, D_B), lambda bb, i: (bb, i, 0)),
        out_shape=jax.ShapeDtypeStruct((bsz, s, D_B), BF16),
        scratch_shapes=[pltpu.VMEM((CONV_HALO + tt, D_B), F32),
                        pltpu.VMEM((CONV_HALO + 64, D_B), F32)],
        compiler_params=pltpu.CompilerParams(
            dimension_semantics=("arbitrary", "arbitrary"), vmem_limit_bytes=VMEM_LIMIT),
    )(conv_in, conv_in, w, b, g, beta)


def _rwkv_kernel(rw_ref, mu_ref, lora_ref, w0_ref, a0_ref, kk_ref, ka_ref, rk_ref, lg_ref, lb_ref,
                 o_ref, s_scr, prev_scr):
    c = pl.program_id(1)
    n = CHUNK
    nb = RW_CHUNKS * CHUNK

    @pl.when(c == 0)
    def _():
        s_scr[...] = jnp.zeros_like(s_scr)
        prev_scr[...] = jnp.zeros_like(prev_scr)

    ps = rw_ref[0]
    row = lax.broadcasted_iota(jnp.int32, (nb, 1), 0)
    prev = jnp.where(row == 0, prev_scr[0:1, :], pltpu.roll(ps, 1, axis=0))
    prev_scr[0:1, :] = ps[nb - 1:nb, :]
    xs = ps + mu_ref[...] * (prev - ps)
    r = xs[:, 0:D_C]
    k = xs[:, D_C:2 * D_C]
    v = xs[:, 2 * D_C:3 * D_C]
    dn = xs[:, 3 * D_C:3 * D_C + 2 * D_LORA]
    lane64 = lax.broadcasted_iota(jnp.int32, (nb, 2 * D_LORA), 1)
    dn = jnp.where(lane64 < D_LORA, jnp.tanh(dn), dn)
    dn_hi, dn_lo = _split2(dn)
    lora_hi, lora_lo = _split2(lora_ref[...])
    pre = _bdot(dn_hi, lora_hi) + _bdot(dn_hi, lora_lo) + _bdot(dn_lo, lora_hi)
    z = -(w0_ref[...] + pre[:, :D_C])
    softplus = jnp.maximum(z, 0.0) + jnp.log(1.0 + jnp.exp(-jnp.abs(z)))
    logdec = -jnp.exp(-softplus - 0.5)
    a = _sigmoid(a0_ref[...] + pre[:, D_C:])
    kkraw = k * kk_ref[...]
    k2 = k * (1.0 + (a - 1.0) * ka_ref[...])
    bonus_pre = r * k2 * rk_ref[...]

    step_in_chunk = row & (n - 1)
    cum_all = logdec
    shift = 1
    while shift < n:
        cum_all = cum_all + jnp.where(step_in_chunk >= shift, pltpu.roll(cum_all, shift, axis=0), 0.0)
        shift *= 2
    p_in_all = jnp.exp(cum_all)
    p_inv_all = jnp.exp(-cum_all)
    p_ex_all = jnp.exp(cum_all - logdec)

    row_n = lax.broadcasted_iota(jnp.int32, (n, LANES), 0)
    lane_n = lax.broadcasted_iota(jnp.int32, (n, LANES), 1)
    lo = lane_n < HEAD_DIM
    col = lane_n & (HEAD_DIM - 1)
    incl = col <= row_n
    strict = col < row_n
    mask2 = jnp.concatenate([strict, incl], axis=0)
    r2 = lax.broadcasted_iota(jnp.int32, (2 * n, LANES), 0)
    l2 = lax.broadcasted_iota(jnp.int32, (2 * n, LANES), 1)
    same_head = (r2 // HEAD_DIM) == (l2 // HEAD_DIM)

    def bd(t):
        tb = t.astype(BF16)
        z = jnp.zeros_like(tb)
        return jnp.concatenate([jnp.where(lo, tb, z), jnp.where(lo, z, tb)], axis=0)

    def hsum(t):
        s_lo = jnp.sum(jnp.where(lo, t, 0.0), axis=-1, keepdims=True)
        s_hi = jnp.sum(jnp.where(lo, 0.0, t), axis=-1, keepdims=True)
        return jnp.where(lo, s_lo, s_hi)

    pairs = range(H_C // 2)
    chains = [(j, g) for j in range(RW_CHUNKS) for g in pairs]
    ids = range(len(chains))
    sl = lambda t, i: t[chains[i][0] * n:(chains[i][0] + 1) * n,
                        chains[i][1] * LANES:(chains[i][1] + 1) * LANES]
    v_p = [sl(v, i) for i in ids]
    p_in = [sl(p_in_all, i) for i in ids]
    kkn = []
    for i in ids:
        kr = sl(kkraw, i)
        kkn.append(kr * lax.rsqrt(jnp.maximum(hsum(kr * kr), 1e-12)))
    rt = [sl(r, i) * p_in[i] for i in ids]
    at = [kkn[i] * sl(p_ex_all, i) for i in ids]
    kt = [sl(k2, i) * sl(p_inv_all, i) for i in ids]
    bt = [kkn[i] * sl(a, i) * sl(p_inv_all, i) for i in ids]
    lhs = [jnp.concatenate([at[i], rt[i]], axis=0).astype(BF16) for i in ids]
    ak = [lax.dot_general(lhs[i], bd(kt[i]), NT_DIMS, preferred_element_type=F32) for i in ids]
    ab = [lax.dot_general(lhs[i], bd(bt[i]), NT_DIMS, preferred_element_type=F32) for i in ids]
    av = [_bdot(jnp.where(mask2, ak[i], 0.0), bd(v_p[i])) for i in ids]
    a_rb = [jnp.where(incl, ab[i][n:], 0.0) for i in ids]
    npow = [jnp.where(strict, -ab[i][:n], 0.0) for i in ids]
    tm = list(npow)
    for _ in range(5):
        both = [_bdot(jnp.concatenate([npow[i], tm[i]], axis=0), bd(npow[i])) for i in ids]
        npow = [both[i][:n] for i in ids]
        tm = [tm[i] + both[i][n:] for i in ids]
    tm = [tm[i] + _bdot(tm[i], bd(npow[i])) for i in ids]
    wt = [at[i] + _bdot(tm[i], bd(at[i])) for i in ids]
    uloc = [av[i][:n] + _bdot(tm[i], bd(av[i][:n])) for i in ids]
    m_t = [jnp.concatenate([kt[i], bt[i]], axis=0).T.astype(BF16) for i in ids]
    eye = r2 == l2
    p_col = [jnp.sum(jnp.where(eye, p_in[i][n - 1:n, :], 0.0), axis=-1, keepdims=True) for i in ids]
    bonus = [hsum(sl(bonus_pre, i)) * v_p[i] for i in ids]

    state = [s_scr[g] for g in pairs]
    for j in range(RW_CHUNKS):
        cid = [j * (H_C // 2) + g for g in pairs]
        ws = [_bdot(jnp.concatenate([wt[i], rt[i]], axis=0), state[g])
              for g, i in zip(pairs, cid)]
        u = [ws[g][:n] + uloc[i] for g, i in zip(pairs, cid)]
        o = [ws[g][n:] + av[i][n:] - _bdot(a_rb[i], bd(u[g])) for g, i in zip(pairs, cid)]
        for g, i in zip(pairs, cid):
            ncat = jnp.concatenate([v_p[i], -u[g]], axis=0).astype(BF16)
            upd = jnp.dot(m_t[i], ncat, preferred_element_type=F32)
            state[g] = (state[g] + jnp.where(same_head, upd, 0.0)) * p_col[i]
        for g, i in zip(pairs, cid):
            lanes = slice(g * LANES, (g + 1) * LANES)
            cen = o[g] - hsum(o[g]) * (1.0 / HEAD_DIM)
            var = hsum(cen * cen) * (1.0 / HEAD_DIM)
            y = cen * lax.rsqrt(var + LNX_EPS) * lg_ref[:, lanes] + lb_ref[:, lanes]
            o_ref[0, j * n:(j + 1) * n, lanes] = (y + bonus[i]).astype(BF16)
    for g in pairs:
        s_scr[g] = state[g]


def _rwkv(rw, mu, lora, w0, a0, kk, ka, rk, lg, lb):
    bsz, s, _ = rw.shape
    vec = lambda bb, i: (0, 0)
    return pl.pallas_call(
        _rwkv_kernel,
        grid=(bsz, s // (RW_CHUNKS * CHUNK)),
        in_specs=[pl.BlockSpec((1, RW_CHUNKS * CHUNK, RW_COLS), lambda bb, i: (bb, i, 0)),
                  pl.BlockSpec((1, RW_COLS), vec),
                  pl.BlockSpec((2 * D_LORA, 2 * D_C), vec)] +
                 [pl.BlockSpec((1, D_C), vec)] * 7,
        out_specs=pl.BlockSpec((1, RW_CHUNKS * CHUNK, D_C), lambda bb, i: (bb, i, 0)),
        out_shape=jax.ShapeDtypeStruct((bsz, s, D_C), BF16),
        scratch_shapes=[pltpu.VMEM((H_C // 2, 2 * HEAD_DIM, LANES), F32),
                        pltpu.VMEM((8, RW_COLS), F32)],
        compiler_params=pltpu.CompilerParams(
            dimension_semantics=("arbitrary", "arbitrary"), vmem_limit_bytes=VMEM_LIMIT),
    )(rw, mu, lora, w0, a0, kk, ka, rk, lg, lb)


def _outproj_kernel(x_ref, attn_ref, conv_ref, rw_ref, gates_ref, gm_ref, w_ref, o_ref):
    sg = _silu(gates_ref[...].astype(F32))
    ya = (attn_ref[...].astype(F32) * sg[:, 0:D_A]).astype(BF16)
    yb = (conv_ref[...].astype(F32) * sg[:, D_A:D_A + D_B]).astype(BF16)
    yc = (rw_ref[...].astype(F32) * sg[:, D_A + D_B:D_MIX]).astype(BF16)
    y = (jnp.dot(ya, w_ref[0:D_A, :], preferred_element_type=F32)
         + jnp.dot(yb, w_ref[D_A:D_A + D_B, :], preferred_element_type=F32)
         + jnp.dot(yc, w_ref[D_A + D_B:D_MIX, :], preferred_element_type=F32))
    o_ref[...] = x_ref[...] + gm_ref[0] * y


def _outproj(x2, attn, conv, rw, gates, gate_mod, w, seq, tm):
    n, d = x2.shape
    per_b = seq // tm
    row = lambda i: (i, 0)
    return pl.pallas_call(
        _outproj_kernel,
        grid=(n // tm,),
        in_specs=[pl.BlockSpec((tm, d), row),
                  pl.BlockSpec((tm, D_A), row),
                  pl.BlockSpec((tm, D_B), row),
                  pl.BlockSpec((tm, D_C), row),
                  pl.BlockSpec((tm, D_MIX), row),
                  pl.BlockSpec((1, 1, d), lambda i: (i // per_b, 0, 0)),
                  pl.BlockSpec((D_MIX, d), lambda i: (0, 0))],
        out_specs=pl.BlockSpec((tm, d), row),
        out_shape=jax.ShapeDtypeStruct((n, d), F32),
        compiler_params=pltpu.CompilerParams(
            dimension_semantics=("arbitrary",), vmem_limit_bytes=VMEM_LIMIT),
    )(x2, attn, conv, rw, gates, gate_mod, w)


def _reorder_w_in(w_in):
    depth, d, _ = w_in.shape
    a_cols = 4 * D_A + N_IDX_HEADS * D_IDX + D_IDX + N_IDX_HEADS
    b0 = a_cols
    c0 = a_cols + 3 * D_B
    c_shift = 3 * D_C + 2 * D_LORA
    zeros = lambda m: jnp.zeros((depth, d, m), w_in.dtype)
    qi0 = 4 * D_A
    ki0 = qi0 + N_IDX_HEADS * D_IDX
    wi0 = ki0 + D_IDX
    parts = [w_in[..., 0:3 * D_A],
             w_in[..., qi0:ki0],
             w_in[..., ki0:wi0], w_in[..., ki0:wi0],
             w_in[..., wi0:a_cols], zeros(LANES - N_IDX_HEADS),
             w_in[..., 3 * D_A:4 * D_A], w_in[..., b0 + 2 * D_B:b0 + 3 * D_B],
             w_in[..., c0 + c_shift:c0 + c_shift + D_C],
             w_in[..., b0:b0 + 2 * D_B],
             w_in[..., c0:c0 + c_shift], zeros(RW_COLS - c_shift)]
    w = jnp.concatenate(parts, axis=-1)
    assert w.shape[-1] == N_COLS
    return w.astype(BF16)


def kernel(x, c, norm_g, w_ada, b_ada, w_in, w_out, q_norm_g, k_norm_g, rel_bias, conv_w, conv_b,
           conv_ln_g, conv_ln_b, shift_mu, decay_w0, decay_up, iclr_a0, iclr_up, key_k, key_a,
           bonus_r_k, lnx_g, lnx_b):
    bsz, seq, d = x.shape
    depth = w_in.shape[0]
    assert seq % (GROUP * TILE) == 0 and d == w_out.shape[-1]
    topk = min(TOPK_MAX, seq // 4)
    tm = GROUP * TILE
    conv_tt = 256

    mod = _modulation(c, w_ada, b_ada)
    shift = mod[:, :, None, 0:d]
    sc1p = 1.0 + mod[:, :, None, d:2 * d]
    gate_mod = mod[:, :, None, 2 * d:3 * d]

    w_all = _reorder_w_in(w_in)
    qg = jnp.tile(q_norm_g, (1, H_A))[:, None, :] * (HEAD_DIM ** -0.5)
    kg = jnp.tile(k_norm_g, (1, H_A))[:, None, :]
    bias_tiles = _bias_tiles(rel_bias)
    c_shift = 3 * D_C + 2 * D_LORA
    mu_pad = jnp.pad(shift_mu, ((0, 0), (0, RW_COLS - c_shift)))[:, None, :]
    zl = jnp.zeros((depth, D_LORA, D_C), F32)
    lora = jnp.concatenate([jnp.concatenate([decay_up, zl], axis=2),
                            jnp.concatenate([zl, iclr_up], axis=2)], axis=1)
    row = lambda t: t[:, None, :]
    layers = dict(
        shift=shift, sc1p=sc1p, gate_mod=gate_mod, norm_g=row(norm_g), w=w_all, qg=qg, kg=kg,
        w_out=w_out.astype(BF16), conv_w=conv_w, conv_b=row(conv_b), conv_g=row(conv_ln_g),
        conv_beta=row(conv_ln_b), mu=mu_pad, lora=lora, w0=row(decay_w0), a0=row(iclr_a0),
        kk=row(key_k), ka=row(key_a), rk=row(bonus_r_k.reshape(depth, D_C)), lg=row(lnx_g),
        lb=row(lnx_b))

    def layer(x2, p):
        q, k, vt, qi, ki, wit, gates, conv_in, rw_in = _inproj(
            x2, p["sc1p"], p["shift"], p["norm_g"], p["w"], p["qg"], p["kg"], seq, tm)
        r3 = lambda t: t.reshape(bsz, seq, t.shape[-1])
        attn = _attention(r3(q), r3(k), vt, r3(qi), r3(ki), wit, bias_tiles, topk)
        conv = _conv(r3(conv_in), p["conv_w"], p["conv_b"], p["conv_g"], p["conv_beta"], conv_tt)
        rwo = _rwkv(r3(rw_in), p["mu"], p["lora"], p["w0"], p["a0"], p["kk"], p["ka"], p["rk"],
                    p["lg"], p["lb"])
        x2 = _outproj(x2, attn.reshape(-1, D_A), conv.reshape(-1, D_B), rwo.reshape(-1, D_C),
                      gates, p["gate_mod"], p["w_out"], seq, tm)
        return x2

    x2 = x.reshape(bsz * seq, d)
    for l in range(depth):
        x2 = layer(x2, {name: t[l] for name, t in layers.items()})
    return x2.reshape(bsz, seq, d)
```

```python
import functools
import math

import jax
import jax.numpy as jnp
import numpy as np
from jax import lax
from jax.experimental import pallas as pl
from jax.experimental.pallas import tpu as pltpu

F32 = jnp.float32
BF16 = jnp.bfloat16

HEAD_DIM = 64
CHUNK = 64
EPS = 1e-6
H_A = 6
D_A = H_A * HEAD_DIM
N_IDX_HEADS = 8
D_IDX = 64
TOPK_MAX = 256
N_BUCKETS = 32
MAX_DISTANCE = 1024
D_B = 256
CONV_WIDTH = 31
H_C = 6
D_C = H_C * HEAD_DIM
D_LORA = 32
LNX_EPS = 64e-5
D_MIX = D_A + D_B + D_C

LANES = 128
TILE = 128
GROUP = 4
HALF = 1 << 15
DEN_FLOOR = 1e-30
LATE_STEPS = 4
ONES_ROWS = 16
RW_CHUNKS = 4
QW = 256
N_FAR = 9
N_BIAS = N_FAR + QW // TILE + 1
ATTN_VMEM_LIMIT = 60 * 1024 * 1024
VMEM_LIMIT = 48 * 1024 * 1024

C_Q, C_K, C_V = 0, 384, 768
C_QI = 1152
C_KI = 1664
C_WI = 1792
C_GATES = 1920
C_CONV = 2944
C_RW = 3456
N_COLS = 4736
RW_COLS = 1280

INT_MAX = 2 ** 31 - 1
NEG_KEY = int(np.array(-np.inf, np.float32).view(np.int32)) ^ 0x7FFFFFFF
NEG_BIG = -1e30

NT_DIMS = (((1,), (1,)), ((), ()))


def _bdot(a, b):
    return jnp.dot(a.astype(BF16), b.astype(BF16), preferred_element_type=F32)


def _split2(x):
    hi = x.astype(BF16).astype(F32)
    return hi, x - hi


def _sigmoid(x):
    return 1.0 / (1.0 + jnp.exp(-x))


def _silu(x):
    return x * _sigmoid(x)


def _mod_kernel(c_ref, w_ref, b_ref, o_ref):
    ca = _silu(c_ref[...])
    o_ref[0] = jnp.dot(ca, w_ref[0], precision=lax.Precision.HIGHEST,
                       preferred_element_type=F32) + b_ref[0]


def _modulation(c, w_ada, b_ada):
    depth, d, d3 = w_ada.shape
    b = c.shape[0]
    return pl.pallas_call(
        _mod_kernel,
        grid=(depth, d3 // d),
        in_specs=[pl.BlockSpec((b, d), lambda l, j: (0, 0)),
                  pl.BlockSpec((1, d, d), lambda l, j: (l, 0, j)),
                  pl.BlockSpec((1, 1, d), lambda l, j: (l, 0, j))],
        out_specs=pl.BlockSpec((1, b, d), lambda l, j: (l, 0, j)),
        out_shape=jax.ShapeDtypeStruct((depth, b, d3), F32),
        compiler_params=pltpu.CompilerParams(
            dimension_semantics=("arbitrary", "arbitrary"), vmem_limit_bytes=VMEM_LIMIT),
    )(c, w_ada, b_ada.reshape(depth, 1, d3))


def _inproj_kernel(x_ref, sc_ref, sh_ref, g_ref, w_ref, qg_ref, kg_ref,
                   q_ref, k_ref, vt_ref, qi_ref, ki_ref, wit_ref, gates_ref, conv_ref, rw_ref):
    x = x_ref[...]
    ms = jnp.mean(x * x, axis=-1, keepdims=True)
    h = x * lax.rsqrt(ms + EPS) * g_ref[...]
    hb = (h * sc_ref[0] + sh_ref[0]).astype(BF16)

    def mm(c0, c1):
        return jnp.dot(hb, w_ref[:, c0:c1], preferred_element_type=F32)

    lo = lax.broadcasted_iota(jnp.int32, (x.shape[0], LANES), 1) < HEAD_DIM

    def head_rms(t, g):
        sq = t * t
        means = []
        for grp in range(D_A // LANES):
            blk = sq[:, grp * LANES:(grp + 1) * LANES]
            s_lo = jnp.sum(jnp.where(lo, blk, 0.0), axis=-1, keepdims=True)
            s_hi = jnp.sum(jnp.where(lo, 0.0, blk), axis=-1, keepdims=True)
            means.append(jnp.where(lo, s_lo, s_hi) * (1.0 / HEAD_DIM))
        return t * lax.rsqrt(jnp.concatenate(means, axis=1) + EPS) * g

    q_ref[...] = head_rms(mm(C_Q, C_Q + D_A), qg_ref[...]).astype(BF16)
    k_ref[...] = head_rms(mm(C_K, C_K + D_A), kg_ref[...]).astype(BF16)
    vt_ref[0] = mm(C_V, C_V + D_A).T.astype(BF16)
    qi_ref[...] = mm(C_QI, C_QI + 512).astype(BF16)
    ki_ref[...] = mm(C_KI, C_KI + LANES).astype(BF16)
    wit_ref[...] = mm(C_WI, C_WI + LANES).T[:N_IDX_HEADS, :]
    gates_ref[:, 0:512] = mm(C_GATES, C_GATES + 512).astype(BF16)
    gates_ref[:, 512:1024] = mm(C_GATES + 512, C_GATES + 1024).astype(BF16)
    conv_ref[...] = mm(C_CONV, C_CONV + 512)
    rw_ref[:, 0:512] = mm(C_RW, C_RW + 512)
    rw_ref[:, 512:1024] = mm(C_RW + 512, C_RW + 1024)
    rw_ref[:, 1024:RW_COLS] = mm(C_RW + 1024, C_RW + RW_COLS)


def _inproj(x2, sc1p, shift, norm_g, w, qg, kg, seq, tm):
    n, d = x2.shape
    per_b = seq // tm
    row = lambda i: (i, 0)
    full = lambda i: (0, 0)
    bat = lambda i: (i // per_b, 0, 0)
    rows = lambda wd, dt: (pl.BlockSpec((tm, wd), row), jax.ShapeDtypeStruct((n, wd), dt))
    outs = [rows(D_A, BF16), rows(D_A, BF16),
            (pl.BlockSpec((1, D_A, tm), lambda i: (i, 0, 0)),
             jax.ShapeDtypeStruct((n // tm, D_A, tm), BF16)),
            rows(512, BF16), rows(LANES, BF16),
            (pl.BlockSpec((N_IDX_HEADS, tm), lambda i: (0, i)),
             jax.ShapeDtypeStruct((N_IDX_HEADS, n), F32)),
            rows(D_MIX, BF16), rows(512, F32), rows(RW_COLS, F32)]
    return pl.pallas_call(
        _inproj_kernel,
        grid=(n // tm,),
        in_specs=[pl.BlockSpec((tm, d), row),
                  pl.BlockSpec((1, 1, d), bat),
                  pl.BlockSpec((1, 1, d), bat),
                  pl.BlockSpec((1, d), full),
                  pl.BlockSpec((d, N_COLS), full),
                  pl.BlockSpec((1, D_A), full),
                  pl.BlockSpec((1, D_A), full)],
        out_specs=[o[0] for o in outs],
        out_shape=[o[1] for o in outs],
        compiler_params=pltpu.CompilerParams(
            dimension_semantics=("arbitrary",), vmem_limit_bytes=VMEM_LIMIT),
    )(x2, sc1p, shift, norm_g, w, qg, kg)


def _fold(t, op):
    parts = [t[8 * i:8 * i + 8, :] for i in range(TILE // 8)]
    while len(parts) > 1:
        parts = [op(parts[i], parts[i + 1]) for i in range(0, len(parts), 2)]
    return parts[0]


def _attn_kernel(q_ref, k_ref, vt_ref, qi_ref, ki_ref, wit_ref, bias_ref, o_ref,
                 key_scr, khi_scr, klo_scr, klo2_scr, qih_scr, qh_scr, mb_scr, s_scr, m_scr, acc_scr,
                 *, topk, pos_bits):
    qb = pl.program_id(1)
    nst = ((qb + 1) * (QW // TILE) - 1) // GROUP + 1
    kf = float(topk)
    zgap = 1 << pos_bits
    lane = lax.broadcasted_iota(jnp.int32, (TILE, QW), 1)
    krow = lax.broadcasted_iota(jnp.int32, (TILE, QW), 0)
    lo_half = lax.broadcasted_iota(jnp.int32, (QW, LANES), 1) < HEAD_DIM

    wit = wit_ref[...]
    qi = qi_ref[0]
    q = q_ref[0]
    for src, dst, n_pairs in ((qi, qih_scr, N_IDX_HEADS // 2), (q, qh_scr, H_A // 2)):
        for g in range(n_pairs):
            pair = src[:, g * LANES:(g + 1) * LANES]
            dst[g, 0:QW, :] = jnp.where(lo_half, pair, jnp.zeros_like(pair))
            dst[g, QW:2 * QW, :] = jnp.where(lo_half, jnp.zeros_like(pair), pair)

    key_chunk = krow // CHUNK
    query_chunk = qb * (QW // CHUNK) + lane // CHUNK
    inadm = NEG_KEY - zgap - 1

    def score_key(s):
        b = lax.bitcast_convert_type(s, jnp.int32)
        return jnp.where(b < 0, (b ^ jnp.int32(0x7FFFFFFF)) - zgap, b)

    def p1(st, c):
        base = st * GROUP
        kis = ki_ref[0, pl.ds(pl.multiple_of(base * TILE, GROUP * TILE), GROUP * TILE), :]
        acc = [None] * GROUP
        for g in range(N_IDX_HEADS // 2):
            d = lax.dot_general(kis, qih_scr[g], NT_DIMS, preferred_element_type=F32)
            for par in range(2):
                w_row = wit[2 * g + par:2 * g + par + 1, :]
                for u in range(GROUP):
                    t = jnp.maximum(d[u * TILE:(u + 1) * TILE, par * QW:(par + 1) * QW], 0.0) * w_row
                    acc[u] = t if acc[u] is None else acc[u] + t
        for u in range(GROUP):
            kt = base + u
            key = jnp.where(acc[u] == 0.0, -1 - (krow + kt * TILE), score_key(acc[u]))
            ok = key_chunk + kt * (TILE // CHUNK) <= query_chunk
            key = jnp.where(ok, key, inadm)
            key_scr[kt] = key
            khi_scr[kt] = (key >> 16).astype(jnp.int16)
            klo_scr[kt] = ((key & (2 * HALF - 1)) - HALF).astype(jnp.int16)
        k_s = k_ref[0, pl.ds(pl.multiple_of(base * TILE, GROUP * TILE), GROUP * TILE), :]
        for g in range(H_A // 2):
            s = lax.dot_general(k_s[:, g * LANES:(g + 1) * LANES], qh_scr[g], NT_DIMS,
                                preferred_element_type=F32)
            for par in range(2):
                h = 2 * g + par
                mx = m_scr[h]
                for u in range(GROUP):
                    dd = jnp.clip(base + u - qb * (QW // TILE), -N_FAR, QW // TILE) + N_FAR
                    su = (s[u * TILE:(u + 1) * TILE, par * QW:(par + 1) * QW]
                          + bias_ref[h * N_BIAS + dd])
                    s_scr[h, base + u] = su
                    mx = jnp.maximum(mx, _fold(su, jnp.maximum))
                m_scr[h] = mx
        return c

    for h in range(H_A):
        m_scr[h] = jnp.full((8, QW), NEG_BIG, F32)
    lax.fori_loop(0, nst, p1, 0)

    def count_ge(thr_row):
        def body(st, cs):
            return tuple(c + _fold(jnp.where(key_scr[st * GROUP + u] >= thr_row, 1.0, 0.0), jnp.add)
                         for u, c in enumerate(cs))
        cs = lax.fori_loop(0, nst, body, (jnp.zeros((8, QW), F32),) * GROUP)
        return jnp.sum(functools.reduce(jnp.add, cs), axis=0, keepdims=True)

    def fold16(t):
        parts = [t[16 * i:16 * i + 16, :] for i in range(TILE // 16)]
        while len(parts) > 1:
            parts = [parts[i] + parts[i + 1] for i in range(0, len(parts), 2)]
        return parts[0]

    one16 = jnp.ones((TILE, QW), jnp.int16)
    zero16 = jnp.zeros((TILE, QW), jnp.int16)

    def as16(row):
        return jnp.broadcast_to(row, (TILE, QW)).astype(jnp.int16)

    def count16(scr, thr_row):
        thr16 = as16(thr_row)

        def body(st, cs):
            return tuple(c + fold16(jnp.where(scr[st * GROUP + u] >= thr16, one16, zero16))
                         for u, c in enumerate(cs))
        cs = lax.fori_loop(0, nst, body, (jnp.zeros((16, QW), jnp.int16),) * GROUP)
        total = functools.reduce(jnp.add, [c.astype(F32) for c in cs])
        return jnp.sum(total, axis=0, keepdims=True)

    def hi_step(_, carry):
        lo, hi, c_lo, c_hi = carry
        mid = (lo + hi) >> 1
        cnt = count16(khi_scr, mid)
        ge = cnt >= kf
        return (jnp.where(ge, mid, lo), jnp.where(ge, hi, mid),
                jnp.where(ge, cnt, c_lo), jnp.where(ge, c_hi, cnt))

    n_keys = (nst * (GROUP * TILE)).astype(F32)
    top, _, c_top, c_above = lax.fori_loop(
        0, 16, hi_step, (jnp.full((1, QW), -HALF, jnp.int32), jnp.full((1, QW), HALF, jnp.int32),
                         jnp.broadcast_to(n_keys, (1, QW)), jnp.zeros((1, QW), F32)))

    top16 = as16(top)
    min16 = jnp.full((TILE, QW), -HALF, jnp.int16)

    def keep_bucket(kt, c):
        klo2_scr[kt] = jnp.where(khi_scr[kt] == top16, klo_scr[kt], min16)
        return c

    lax.fori_loop(0, nst * GROUP, keep_bucket, 0)
    need = kf - c_above

    def lo_step(lo, hi, c_lo):
        mid = (lo + hi) >> 1
        cnt = count16(klo2_scr, mid)
        active = hi != lo + 1
        ge = jnp.logical_and(active, cnt >= need)
        hi = jnp.where(jnp.logical_and(active, cnt == need), mid + 1,
                       jnp.where(jnp.logical_and(active, cnt < need), mid, hi))
        return jnp.where(ge, mid, lo), hi, jnp.where(ge, cnt, c_lo)

    def unsettled(lo, hi):
        return jnp.max(jnp.where(hi != lo + 1, 1.0, 0.0)) > 0.0

    def late(carry):
        lo, hi, c_lo, it, _ = carry
        lo, hi, c_lo = lo_step(*lo_step(lo, hi, c_lo))
        return lo, hi, c_lo, it + 2, unsettled(lo, hi)

    low0 = (jnp.full((1, QW), -HALF, jnp.int32), jnp.full((1, QW), HALF, jnp.int32), c_top - c_above)
    low = lax.fori_loop(0, 16 - LATE_STEPS, lambda _, c: lo_step(*c), low0)
    low, _, c_low, _, _ = lax.while_loop(lambda carry: jnp.logical_and(carry[3] < 16, carry[4]),
                                         late, (*low, 16 - LATE_STEPS, unsettled(low[0], low[1])))
    thr = top * (2 * HALF) + (low + HALF)
    cnt_ge = c_above + c_low
    tie = jnp.logical_and(cnt_ge > kf, thr > inadm)

    @pl.when(jnp.max(jnp.where(tie, 1.0, 0.0)) > 0.0)
    def _():
        need = kf - count_ge(thr + 1)

        def count_eq_le(pmax):
            def body(kt, c):
                hit = jnp.where(key_scr[kt] == thr, krow + kt * TILE, INT_MAX) <= pmax
                return c + _fold(jnp.where(hit, 1.0, 0.0), jnp.add)
            c = lax.fori_loop(0, nst * GROUP, body, jnp.zeros((8, QW), F32))
            return jnp.sum(c, axis=0, keepdims=True)

        def pbisect(_, lohi):
            plo, phi = lohi
            pmid = (plo + phi) >> 1
            ok = count_eq_le(pmid) >= need
            return jnp.where(ok, plo, pmid), jnp.where(ok, pmid, phi)

        plo0 = jnp.full((1, QW), -1, jnp.int32)
        phi0 = jnp.full((1, QW), (1 << pos_bits) - 1, jnp.int32)
        _, pthr = lax.fori_loop(0, pos_bits + 1, pbisect, (plo0, phi0))
        pthr = jnp.where(tie, pthr, INT_MAX)

        def drop(kt, c):
            kk = key_scr[kt]
            pos = jnp.where(kk == thr, krow + kt * TILE, -1)
            key_scr[kt] = jnp.where(pos > pthr, inadm, kk)
            return c

        lax.fori_loop(0, nst * GROUP, drop, 0)

    thr_eff = jnp.maximum(thr, inadm + 1)

    ones_rows = jnp.ones((ONES_ROWS, GROUP * TILE), BF16)

    def collapse_max():
        for h in range(H_A):
            m_scr[h] = jnp.broadcast_to(jnp.max(m_scr[h], axis=0, keepdims=True), (8, QW))

    def select_bias(base):
        for u in range(GROUP):
            mb_scr[u] = jnp.where(key_scr[base + u] >= thr_eff, 0.0, NEG_BIG)

    def pv(st, c):
        select_bias(st * GROUP)
        for h in range(H_A):
            shift = m_scr[h][0:1, :]
            ps = [jnp.exp((s_scr[h, st * GROUP + u] + mb_scr[u] - shift).astype(BF16))
                  for u in range(GROUP)]
            lhs = jnp.concatenate([vt_ref[st, h * HEAD_DIM:(h + 1) * HEAD_DIM, :], ones_rows], axis=0)
            acc_scr[h] += jnp.dot(lhs, jnp.concatenate(ps, axis=0), preferred_element_type=F32)
        return c

    def run_pv():
        for h in range(H_A):
            acc_scr[h] = jnp.zeros((HEAD_DIM + ONES_ROWS, QW), F32)
        lax.fori_loop(0, nst, pv, 0)

    collapse_max()
    run_pv()
    den_min = acc_scr[0][HEAD_DIM:HEAD_DIM + 1, :]
    for h in range(1, H_A):
        den_min = jnp.minimum(den_min, acc_scr[h][HEAD_DIM:HEAD_DIM + 1, :])

    @pl.when(jnp.logical_not(jnp.min(den_min) > DEN_FLOOR))
    def _():
        for h in range(H_A):
            m_scr[h] = jnp.full((8, QW), NEG_BIG, F32)

        def selected_max(st, c):
            select_bias(st * GROUP)
            for h in range(H_A):
                mx = m_scr[h]
                for u in range(GROUP):
                    mx = jnp.maximum(mx, _fold(s_scr[h, st * GROUP + u] + mb_scr[u], jnp.maximum))
                m_scr[h] = mx
            return c

        lax.fori_loop(0, nst, selected_max, 0)
        collapse_max()
        run_pv()

    outs = []
    for h in range(H_A):
        acc = acc_scr[h]
        outs.append(acc[0:HEAD_DIM, :] / acc[HEAD_DIM:HEAD_DIM + 1, :])
    o_ref[0] = jnp.concatenate(outs, axis=0).T.astype(BF16)


def _attention(q, k, vt, qi, ki, wit, bias_tiles, topk):
    b, s, _ = q.shape
    nt = s // TILE
    nq = s // QW
    ng = s // (GROUP * TILE)
    kern = functools.partial(_attn_kernel, topk=topk, pos_bits=max(1, int(math.ceil(math.log2(s)))))
    qtile = lambda bb, i: (bb, i, 0)
    whole = lambda bb, i: (bb, 0, 0)
    once = pl.Buffered(1)
    return pl.pallas_call(
        kern,
        grid=(b, nq),
        in_specs=[pl.BlockSpec((1, QW, D_A), qtile),
                  pl.BlockSpec((1, s, D_A), whole, pipeline_mode=once),
                  pl.BlockSpec((ng, D_A, GROUP * TILE), whole, pipeline_mode=once),
                  pl.BlockSpec((1, QW, 512), qtile),
                  pl.BlockSpec((1, s, LANES), whole, pipeline_mode=once),
                  pl.BlockSpec((N_IDX_HEADS, QW), lambda bb, i: (0, bb * nq + i)),
                  pl.BlockSpec(bias_tiles.shape, lambda bb, i: (0, 0, 0), pipeline_mode=once)],
        out_specs=pl.BlockSpec((1, QW, D_A), qtile),
        out_shape=jax.ShapeDtypeStruct((b, s, D_A), BF16),
        scratch_shapes=[pltpu.VMEM((nt, TILE, QW), jnp.int32),
                        pltpu.VMEM((nt, TILE, QW), jnp.int16),
                        pltpu.VMEM((nt, TILE, QW), jnp.int16),
                        pltpu.VMEM((nt, TILE, QW), jnp.int16),
                        pltpu.VMEM((N_IDX_HEADS // 2, 2 * QW, LANES), BF16),
                        pltpu.VMEM((H_A // 2, 2 * QW, LANES), BF16),
                        pltpu.VMEM((GROUP, TILE, QW), F32),
                        pltpu.VMEM((H_A, nt, TILE, QW), F32),
                        pltpu.VMEM((H_A, 8, QW), F32),
                        pltpu.VMEM((H_A, HEAD_DIM + ONES_ROWS, QW), F32)],
        compiler_params=pltpu.CompilerParams(
            dimension_semantics=("arbitrary", "arbitrary"), vmem_limit_bytes=ATTN_VMEM_LIMIT),
    )(q, k, vt, qi, ki, wit, bias_tiles)


def _t5_bucket(rel):
    nb = N_BUCKETS // 2
    max_exact = nb // 2
    ret = jnp.where(rel > 0, nb, 0)
    n = jnp.abs(rel)
    nf = jnp.maximum(n, 1).astype(F32)
    large = max_exact + (jnp.log(nf / max_exact) / math.log(MAX_DISTANCE / max_exact)
                         * (nb - max_exact)).astype(jnp.int32)
    large = jnp.minimum(large, nb - 1)
    return ret + jnp.where(n < max_exact, n, large)


def _bias_tiles(rel_bias):
    n = TILE + QW - 1
    n_off = N_BIAS - 1
    m = np.arange(n)
    q_minus_k = np.where(m < QW, m, m - n)
    first_key = (np.arange(n_off)[:, None] - N_FAR) * TILE
    rel = first_key - q_minus_k[None, :]
    u = rel_bias[_t5_bucket(jnp.asarray(rel, jnp.int32))].astype(F32)
    u = u.transpose(2, 0, 1)
    flat = jnp.tile(u, (1, 1, TILE))[:, :, :TILE * (n - 1)]
    tiles = flat.reshape(H_A, n_off, TILE, n - 1)[..., :QW]
    j = np.arange(TILE)[None, :, None]
    t = np.arange(QW)[None, None, :]
    admissible = (first_key[:, :, None] + j) // CHUNK <= t // CHUNK
    tiles = jnp.where(jnp.asarray(admissible)[None], tiles, NEG_BIG)
    tiles = jnp.concatenate([tiles, jnp.full((H_A, 1, TILE, QW), NEG_BIG, F32)], axis=1)
    return tiles.reshape(H_A * N_BIAS, TILE, QW)


CONV_HALO = 32


def _conv_kernel(cur_ref, halo_ref, w_ref, b_ref, g_ref, beta_ref, o_ref, u_scr, sh_scr, *, tt):
    i = pl.program_id(1)

    def glu(t):
        return t[:, :D_B] * _sigmoid(t[:, D_B:])

    u_scr[0:CONV_HALO, :] = jnp.where(i > 0, glu(halo_ref[0]), 0.0)
    u_scr[CONV_HALO:CONV_HALO + tt, :] = glu(cur_ref[0])
    rows = 64
    first = CONV_HALO - (CONV_WIDTH - 1)
    for r0 in range(0, tt, rows):
        acc = jnp.broadcast_to(b_ref[...], (rows, D_B))
        for res in range(8):
            offs = [first + j for j in range(CONV_WIDTH) if (first + j) % 8 == res]
            span = offs[-1] - offs[0] + rows
            sh_scr[0:span, :] = u_scr[r0 + offs[0]:r0 + offs[0] + span, :]
            for off in offs:
                acc = acc + w_ref[off - first:off - first + 1, :] * sh_scr[off - offs[0]:off - offs[0] + rows, :]
        mu = jnp.mean(acc, axis=-1, keepdims=True)
        cen = acc - mu
        var = jnp.mean(cen * cen, axis=-1, keepdims=True)
        y = cen * lax.rsqrt(var + EPS) * g_ref[...] + beta_ref[...]
        o_ref[0, r0:r0 + rows, :] = _silu(y).astype(BF16)


def _conv(conv_in, w, b, g, beta, tt):
    bsz, s, _ = conv_in.shape
    per = tt // CONV_HALO
    vec = lambda bb, i: (0, 0)
    return pl.pallas_call(
        functools.partial(_conv_kernel, tt=tt),
        grid=(bsz, s // tt),
        in_specs=[pl.BlockSpec((1, tt, 2 * D_B), lambda bb, i: (bb, i, 0)),
                  pl.BlockSpec((1, CONV_HALO, 2 * D_B),
                               lambda bb, i: (bb, jnp.maximum(i * per - 1, 0), 0)),
                  pl.BlockSpec((CONV_WIDTH, D_B), vec),
                  pl.BlockSpec((1, D_B), vec),
                  pl.BlockSpec((1, D_B), vec),
                  pl.BlockSpec((1, D_B), vec)],
        out_specs=pl.BlockSpec((1, tt, D_B), lambda bb, i: (bb, i, 0)),
        out_shape=jax.ShapeDtypeStruct((bsz, s, D_B), BF16),
        scratch_shapes=[pltpu.VMEM((CONV_HALO + tt, D_B), F32),
                        pltpu.VMEM((CONV_HALO + 64, D_B), F32)],
        compiler_params=pltpu.CompilerParams(
            dimension_semantics=("arbitrary", "arbitrary"), vmem_limit_bytes=VMEM_LIMIT),
    )(conv_in, conv_in, w, b, g, beta)


def _rwkv_kernel(rw_ref, mu_ref, lora_ref, w0_ref, a0_ref, kk_ref, ka_ref, rk_ref, lg_ref, lb_ref,
                 o_ref, s_scr, prev_scr):
    c = pl.program_id(1)
    n = CHUNK
    nb = RW_CHUNKS * CHUNK

    @pl.when(c == 0)
    def _():
        s_scr[...] = jnp.zeros_like(s_scr)
        prev_scr[...] = jnp.zeros_like(prev_scr)

    ps = rw_ref[0]
    row = lax.broadcasted_iota(jnp.int32, (nb, 1), 0)
    prev = jnp.where(row == 0, prev_scr[0:1, :], pltpu.roll(ps, 1, axis=0))
    prev_scr[0:1, :] = ps[nb - 1:nb, :]
    xs = ps + mu_ref[...] * (prev - ps)
    r = xs[:, 0:D_C]
    k = xs[:, D_C:2 * D_C]
    v = xs[:, 2 * D_C:3 * D_C]
    dn = xs[:, 3 * D_C:3 * D_C + 2 * D_LORA]
    lane64 = lax.broadcasted_iota(jnp.int32, (nb, 2 * D_LORA), 1)
    dn = jnp.where(lane64 < D_LORA, jnp.tanh(dn), dn)
    dn_hi, dn_lo = _split2(dn)
    lora_hi, lora_lo = _split2(lora_ref[...])
    pre = _bdot(dn_hi, lora_hi) + _bdot(dn_hi, lora_lo) + _bdot(dn_lo, lora_hi)
    z = -(w0_ref[...] + pre[:, :D_C])
    softplus = jnp.maximum(z, 0.0) + jnp.log(1.0 + jnp.exp(-jnp.abs(z)))
    logdec = -jnp.exp(-softplus - 0.5)
    a = _sigmoid(a0_ref[...] + pre[:, D_C:])
    kkraw = k * kk_ref[...]
    k2 = k * (1.0 + (a - 1.0) * ka_ref[...])
    bonus_pre = r * k2 * rk_ref[...]

    step_in_chunk = row & (n - 1)
    cum_all = logdec
    shift = 1
    while shift < n:
        cum_all = cum_all + jnp.where(step_in_chunk >= shift, pltpu.roll(cum_all, shift, axis=0), 0.0)
        shift *= 2
    p_in_all = jnp.exp(cum_all)
    p_inv_all = jnp.exp(-cum_all)
    p_ex_all = jnp.exp(cum_all - logdec)

    row_n = lax.broadcasted_iota(jnp.int32, (n, LANES), 0)
    lane_n = lax.broadcasted_iota(jnp.int32, (n, LANES), 1)
    lo = lane_n < HEAD_DIM
    col = lane_n & (HEAD_DIM - 1)
    incl = col <= row_n
    strict = col < row_n
    mask2 = jnp.concatenate([strict, incl], axis=0)
    r2 = lax.broadcasted_iota(jnp.int32, (2 * n, LANES), 0)
    l2 = lax.broadcasted_iota(jnp.int32, (2 * n, LANES), 1)
    same_head = (r2 // HEAD_DIM) == (l2 // HEAD_DIM)

    def bd(t):
        tb = t.astype(BF16)
        z = jnp.zeros_like(tb)
        return jnp.concatenate([jnp.where(lo, tb, z), jnp.where(lo, z, tb)], axis=0)

    def hsum(t):
        s_lo = jnp.sum(jnp.where(lo, t, 0.0), axis=-1, keepdims=True)
        s_hi = jnp.sum(jnp.where(lo, 0.0, t), axis=-1, keepdims=True)
        return jnp.where(lo, s_lo, s_hi)

    pairs = range(H_C // 2)
    chains = [(j, g) for j in range(RW_CHUNKS) for g in pairs]
    ids = range(len(chains))
    sl = lambda t, i: t[chains[i][0] * n:(chains[i][0] + 1) * n,
                        chains[i][1] * LANES:(chains[i][1] + 1) * LANES]
    v_p = [sl(v, i) for i in ids]
    p_in = [sl(p_in_all, i) for i in ids]
    kkn = []
    for i in ids:
        kr = sl(kkraw, i)
        kkn.append(kr * lax.rsqrt(jnp.maximum(hsum(kr * kr), 1e-12)))
    rt = [sl(r, i) * p_in[i] for i in ids]
    at = [kkn[i] * sl(p_ex_all, i) for i in ids]
    kt = [sl(k2, i) * sl(p_inv_all, i) for i in ids]
    bt = [kkn[i] * sl(a, i) * sl(p_inv_all, i) for i in ids]
    lhs = [jnp.concatenate([at[i], rt[i]], axis=0).astype(BF16) for i in ids]
    ak = [lax.dot_general(lhs[i], bd(kt[i]), NT_DIMS, preferred_element_type=F32) for i in ids]
    ab = [lax.dot_general(lhs[i], bd(bt[i]), NT_DIMS, preferred_element_type=F32) for i in ids]
    av = [_bdot(jnp.where(mask2, ak[i], 0.0), bd(v_p[i])) for i in ids]
    a_rb = [jnp.where(incl, ab[i][n:], 0.0) for i in ids]
    npow = [jnp.where(strict, -ab[i][:n], 0.0) for i in ids]
    tm = list(npow)
    for _ in range(5):
        both = [_bdot(jnp.concatenate([npow[i], tm[i]], axis=0), bd(npow[i])) for i in ids]
        npow = [both[i][:n] for i in ids]
        tm = [tm[i] + both[i][n:] for i in ids]
    tm = [tm[i] + _bdot(tm[i], bd(npow[i])) for i in ids]
    wt = [at[i] + _bdot(tm[i], bd(at[i])) for i in ids]
    uloc = [av[i][:n] + _bdot(tm[i], bd(av[i][:n])) for i in ids]
    m_t = [jnp.concatenate([kt[i], bt[i]], axis=0).T.astype(BF16) for i in ids]
    eye = r2 == l2
    p_col = [jnp.sum(jnp.where(eye, p_in[i][n - 1:n, :], 0.0), axis=-1, keepdims=True) for i in ids]
    bonus = [hsum(sl(bonus_pre, i)) * v_p[i] for i in ids]

    state = [s_scr[g] for g in pairs]
    for j in range(RW_CHUNKS):
        cid = [j * (H_C // 2) + g for g in pairs]
        ws = [_bdot(jnp.concatenate([wt[i], rt[i]], axis=0), state[g])
              for g, i in zip(pairs, cid)]
        u = [ws[g][:n] + uloc[i] for g, i in zip(pairs, cid)]
        o = [ws[g][n:] + av[i][n:] - _bdot(a_rb[i], bd(u[g])) for g, i in zip(pairs, cid)]
        for g, i in zip(pairs, cid):
            ncat = jnp.concatenate([v_p[i], -u[g]], axis=0).astype(BF16)
            upd = jnp.dot(m_t[i], ncat, preferred_element_type=F32)
            state[g] = (state[g] + jnp.where(same_head, upd, 0.0)) * p_col[i]
        for g, i in zip(pairs, cid):
            lanes = slice(g * LANES, (g + 1) * LANES)
            cen = o[g] - hsum(o[g]) * (1.0 / HEAD_DIM)
            var = hsum(cen * cen) * (1.0 / HEAD_DIM)
            y = cen * lax.rsqrt(var + LNX_EPS) * lg_ref[:, lanes] + lb_ref[:, lanes]
            o_ref[0, j * n:(j + 1) * n, lanes] = (y + bonus[i]).astype(BF16)
    for g in pairs:
        s_scr[g] = state[g]


def _rwkv(rw, mu, lora, w0, a0, kk, ka, rk, lg, lb):
    bsz, s, _ = rw.shape
    vec = lambda bb, i: (0, 0)
    return pl.pallas_call(
        _rwkv_kernel,
        grid=(bsz, s // (RW_CHUNKS * CHUNK)),
        in_specs=[pl.BlockSpec((1, RW_CHUNKS * CHUNK, RW_COLS), lambda bb, i: (bb, i, 0)),
                  pl.BlockSpec((1, RW_COLS), vec),
                  pl.BlockSpec((2 * D_LORA, 2 * D_C), vec)] +
                 [pl.BlockSpec((1, D_C), vec)] * 7,
        out_specs=pl.BlockSpec((1, RW_CHUNKS * CHUNK, D_C), lambda bb, i: (bb, i, 0)),
        out_shape=jax.ShapeDtypeStruct((bsz, s, D_C), BF16),
        scratch_shapes=[pltpu.VMEM((H_C // 2, 2 * HEAD_DIM, LANES), F32),
                        pltpu.VMEM((8, RW_COLS), F32)],
        compiler_params=pltpu.CompilerParams(
            dimension_semantics=("arbitrary", "arbitrary"), vmem_limit_bytes=VMEM_LIMIT),
    )(rw, mu, lora, w0, a0, kk, ka, rk, lg, lb)


def _outproj_kernel(x_ref, attn_ref, conv_ref, rw_ref, gates_ref, gm_ref, w_ref, o_ref):
    sg = _silu(gates_ref[...].astype(F32))
    ya = (attn_ref[...].astype(F32) * sg[:, 0:D_A]).astype(BF16)
    yb = (conv_ref[...].astype(F32) * sg[:, D_A:D_A + D_B]).astype(BF16)
    yc = (rw_ref[...].astype(F32) * sg[:, D_A + D_B:D_MIX]).astype(BF16)
    y = (jnp.dot(ya, w_ref[0:D_A, :], preferred_element_type=F32)
         + jnp.dot(yb, w_ref[D_A:D_A + D_B, :], preferred_element_type=F32)
         + jnp.dot(yc, w_ref[D_A + D_B:D_MIX, :], preferred_element_type=F32))
    o_ref[...] = x_ref[...] + gm_ref[0] * y


def _outproj(x2, attn, conv, rw, gates, gate_mod, w, seq, tm):
    n, d = x2.shape
    per_b = seq // tm
    row = lambda i: (i, 0)
    return pl.pallas_call(
        _outproj_kernel,
        grid=(n // tm,),
        in_specs=[pl.BlockSpec((tm, d), row),
                  pl.BlockSpec((tm, D_A), row),
                  pl.BlockSpec((tm, D_B), row),
                  pl.BlockSpec((tm, D_C), row),
                  pl.BlockSpec((tm, D_MIX), row),
                  pl.BlockSpec((1, 1, d), lambda i: (i // per_b, 0, 0)),
                  pl.BlockSpec((D_MIX, d), lambda i: (0, 0))],
        out_specs=pl.BlockSpec((tm, d), row),
        out_shape=jax.ShapeDtypeStruct((n, d), F32),
        compiler_params=pltpu.CompilerParams(
            dimension_semantics=("arbitrary",), vmem_limit_bytes=VMEM_LIMIT),
    )(x2, attn, conv, rw, gates, gate_mod, w)


def _reorder_w_in(w_in):
    depth, d, _ = w_in.shape
    a_cols = 4 * D_A + N_IDX_HEADS * D_IDX + D_IDX + N_IDX_HEADS
    b0 = a_cols
    c0 = a_cols + 3 * D_B
    c_shift = 3 * D_C + 2 * D_LORA
    zeros = lambda m: jnp.zeros((depth, d, m), w_in.dtype)
    qi0 = 4 * D_A
    ki0 = qi0 + N_IDX_HEADS * D_IDX
    wi0 = ki0 + D_IDX
    parts = [w_in[..., 0:3 * D_A],
             w_in[..., qi0:ki0],
             w_in[..., ki0:wi0], w_in[..., ki0:wi0],
             w_in[..., wi0:a_cols], zeros(LANES - N_IDX_HEADS),
             w_in[..., 3 * D_A:4 * D_A], w_in[..., b0 + 2 * D_B:b0 + 3 * D_B],
             w_in[..., c0 + c_shift:c0 + c_shift + D_C],
             w_in[..., b0:b0 + 2 * D_B],
             w_in[..., c0:c0 + c_shift], zeros(RW_COLS - c_shift)]
    w = jnp.concatenate(parts, axis=-1)
    assert w.shape[-1] == N_COLS
    return w.astype(BF16)


def kernel(x, c, norm_g, w_ada, b_ada, w_in, w_out, q_norm_g, k_norm_g, rel_bias, conv_w, conv_b,
           conv_ln_g, conv_ln_b, shift_mu, decay_w0, decay_up, iclr_a0, iclr_up, key_k, key_a,
           bonus_r_k, lnx_g, lnx_b):
    bsz, seq, d = x.shape
    depth = w_in.shape[0]
    assert seq % (GROUP * TILE) == 0 and d == w_out.shape[-1]
    topk = min(TOPK_MAX, seq // 4)
    tm = GROUP * TILE
    conv_tt = 256

    mod = _modulation(c, w_ada, b_ada)
    shift = mod[:, :, None, 0:d]
    sc1p = 1.0 + mod[:, :, None, d:2 * d]
    gate_mod = mod[:, :, None, 2 * d:3 * d]

    w_all = _reorder_w_in(w_in)
    qg = jnp.tile(q_norm_g, (1, H_A))[:, None, :] * (HEAD_DIM ** -0.5)
    kg = jnp.tile(k_norm_g, (1, H_A))[:, None, :]
    bias_tiles = _bias_tiles(rel_bias)
    c_shift = 3 * D_C + 2 * D_LORA
    mu_pad = jnp.pad(shift_mu, ((0, 0), (0, RW_COLS - c_shift)))[:, None, :]
    zl = jnp.zeros((depth, D_LORA, D_C), F32)
    lora = jnp.concatenate([jnp.concatenate([decay_up, zl], axis=2),
                            jnp.concatenate([zl, iclr_up], axis=2)], axis=1)
    row = lambda t: t[:, None, :]
    layers = dict(
        shift=shift, sc1p=sc1p, gate_mod=gate_mod, norm_g=row(norm_g), w=w_all, qg=qg, kg=kg,
        w_out=w_out.astype(BF16), conv_w=conv_w, conv_b=row(conv_b), conv_g=row(conv_ln_g),
        conv_beta=row(conv_ln_b), mu=mu_pad, lora=lora, w0=row(decay_w0), a0=row(iclr_a0),
        kk=row(key_k), ka=row(key_a), rk=row(bonus_r_k.reshape(depth, D_C)), lg=row(lnx_g),
        lb=row(lnx_b))

    def layer(x2, p):
        q, k, vt, qi, ki, wit, gates, conv_in, rw_in = _inproj(
            x2, p["sc1p"], p["shift"], p["norm_g"], p["w"], p["qg"], p["kg"], seq, tm)
        r3 = lambda t: t.reshape(bsz, seq, t.shape[-1])
        attn = _attention(r3(q), r3(k), vt, r3(qi), r3(ki), wit, bias_tiles, topk)
        conv = _conv(r3(conv_in), p["conv_w"], p["conv_b"], p["conv_g"], p["conv_beta"], conv_tt)
        rwo = _rwkv(r3(rw_in), p["mu"], p["lora"], p["w0"], p["a0"], p["kk"], p["ka"], p["rk"],
                    p["lg"], p["lb"])
        x2 = _outproj(x2, attn.reshape(-1, D_A), conv.reshape(-1, D_B), rwo.reshape(-1, D_C),
                      gates, p["gate_mod"], p["w_out"], seq, tm)
        return x2

    x2 = x.reshape(bsz * seq, d)
    for l in range(depth):
        x2 = layer(x2, {name: t[l] for name, t in layers.items()})
    return x2.reshape(bsz, seq, d)
```

```python
import functools
import math

import jax
import jax.numpy as jnp
import numpy as np
from jax import lax
from jax.experimental import pallas as pl
from jax.experimental.pallas import tpu as pltpu

F32 = jnp.float32
BF16 = jnp.bfloat16

HEAD_DIM = 64
CHUNK = 64
EPS = 1e-6
H_A = 6
D_A = H_A * HEAD_DIM
N_IDX_HEADS = 8
D_IDX = 64
TOPK_MAX = 256
N_BUCKETS = 32
MAX_DISTANCE = 1024
D_B = 256
CONV_WIDTH = 31
H_C = 6
D_C = H_C * HEAD_DIM
D_LORA = 32
LNX_EPS = 64e-5
D_MIX = D_A + D_B + D_C

LANES = 128
TILE = 128
GROUP = 4
HALF = 1 << 15
DEN_FLOOR = 1e-30
LATE_STEPS = 6
ONES_ROWS = 16
RW_CHUNKS = 4
QW = 256
N_FAR = 9
N_BIAS = N_FAR + QW // TILE + 1
ATTN_VMEM_LIMIT = 60 * 1024 * 1024
VMEM_LIMIT = 48 * 1024 * 1024

C_Q, C_K, C_V = 0, 384, 768
C_QI = 1152
C_KI = 1664
C_WI = 1792
C_GATES = 1920
C_CONV = 2944
C_RW = 3456
N_COLS = 4736
RW_COLS = 1280

INT_MAX = 2 ** 31 - 1
NEG_KEY = int(np.array(-np.inf, np.float32).view(np.int32)) ^ 0x7FFFFFFF
NEG_BIG = -1e30

NT_DIMS = (((1,), (1,)), ((), ()))


def _bdot(a, b):
    return jnp.dot(a.astype(BF16), b.astype(BF16), preferred_element_type=F32)


def _split2(x):
    hi = x.astype(BF16).astype(F32)
    return hi, x - hi


def _sigmoid(x):
    return 1.0 / (1.0 + jnp.exp(-x))


def _silu(x):
    return x * _sigmoid(x)


def _mod_kernel(c_ref, w_ref, b_ref, o_ref):
    ca = _silu(c_ref[...])
    o_ref[0] = jnp.dot(ca, w_ref[0], precision=lax.Precision.HIGHEST,
                       preferred_element_type=F32) + b_ref[0]


def _modulation(c, w_ada, b_ada):
    depth, d, d3 = w_ada.shape
    b = c.shape[0]
    return pl.pallas_call(
        _mod_kernel,
        grid=(depth, d3 // d),
        in_specs=[pl.BlockSpec((b, d), lambda l, j: (0, 0)),
                  pl.BlockSpec((1, d, d), lambda l, j: (l, 0, j)),
                  pl.BlockSpec((1, 1, d), lambda l, j: (l, 0, j))],
        out_specs=pl.BlockSpec((1, b, d), lambda l, j: (l, 0, j)),
        out_shape=jax.ShapeDtypeStruct((depth, b, d3), F32),
        compiler_params=pltpu.CompilerParams(
            dimension_semantics=("arbitrary", "arbitrary"), vmem_limit_bytes=VMEM_LIMIT),
    )(c, w_ada, b_ada.reshape(depth, 1, d3))


def _inproj_kernel(x_ref, sc_ref, sh_ref, g_ref, w_ref, qg_ref, kg_ref,
                   q_ref, k_ref, vt_ref, qi_ref, ki_ref, wit_ref, gates_ref, conv_ref, rw_ref):
    x = x_ref[...]
    ms = jnp.mean(x * x, axis=-1, keepdims=True)
    h = x * lax.rsqrt(ms + EPS) * g_ref[...]
    hb = (h * sc_ref[0] + sh_ref[0]).astype(BF16)

    def mm(c0, c1):
        return jnp.dot(hb, w_ref[:, c0:c1], preferred_element_type=F32)

    lo = lax.broadcasted_iota(jnp.int32, (x.shape[0], LANES), 1) < HEAD_DIM

    def head_rms(t, g):
        sq = t * t
        means = []
        for grp in range(D_A // LANES):
            blk = sq[:, grp * LANES:(grp + 1) * LANES]
            s_lo = jnp.sum(jnp.where(lo, blk, 0.0), axis=-1, keepdims=True)
            s_hi = jnp.sum(jnp.where(lo, 0.0, blk), axis=-1, keepdims=True)
            means.append(jnp.where(lo, s_lo, s_hi) * (1.0 / HEAD_DIM))
        return t * lax.rsqrt(jnp.concatenate(means, axis=1) + EPS) * g

    q_ref[...] = head_rms(mm(C_Q, C_Q + D_A), qg_ref[...]).astype(BF16)
    k_ref[...] = head_rms(mm(C_K, C_K + D_A), kg_ref[...]).astype(BF16)
    vt_ref[0] = mm(C_V, C_V + D_A).T.astype(BF16)
    qi_ref[...] = mm(C_QI, C_QI + 512).astype(BF16)
    ki_ref[...] = mm(C_KI, C_KI + LANES).astype(BF16)
    wit_ref[...] = mm(C_WI, C_WI + LANES).T[:N_IDX_HEADS, :]
    gates_ref[:, 0:512] = mm(C_GATES, C_GATES + 512).astype(BF16)
    gates_ref[:, 512:1024] = mm(C_GATES + 512, C_GATES + 1024).astype(BF16)
    conv_ref[...] = mm(C_CONV, C_CONV + 512)
    rw_ref[:, 0:512] = mm(C_RW, C_RW + 512)
    rw_ref[:, 512:1024] = mm(C_RW + 512, C_RW + 1024)
    rw_ref[:, 1024:RW_COLS] = mm(C_RW + 1024, C_RW + RW_COLS)


def _inproj(x2, sc1p, shift, norm_g, w, qg, kg, seq, tm):
    n, d = x2.shape
    per_b = seq // tm
    row = lambda i: (i, 0)
    full = lambda i: (0, 0)
    bat = lambda i: (i // per_b, 0, 0)
    rows = lambda wd, dt: (pl.BlockSpec((tm, wd), row), jax.ShapeDtypeStruct((n, wd), dt))
    outs = [rows(D_A, BF16), rows(D_A, BF16),
            (pl.BlockSpec((1, D_A, tm), lambda i: (i, 0, 0)),
             jax.ShapeDtypeStruct((n // tm, D_A, tm), BF16)),
            rows(512, BF16), rows(LANES, BF16),
            (pl.BlockSpec((N_IDX_HEADS, tm), lambda i: (0, i)),
             jax.ShapeDtypeStruct((N_IDX_HEADS, n), F32)),
            rows(D_MIX, BF16), rows(512, F32), rows(RW_COLS, F32)]
    return pl.pallas_call(
        _inproj_kernel,
        grid=(n // tm,),
        in_specs=[pl.BlockSpec((tm, d), row),
                  pl.BlockSpec((1, 1, d), bat),
                  pl.BlockSpec((1, 1, d), bat),
                  pl.BlockSpec((1, d), full),
                  pl.BlockSpec((d, N_COLS), full),
                  pl.BlockSpec((1, D_A), full),
                  pl.BlockSpec((1, D_A), full)],
        out_specs=[o[0] for o in outs],
        out_shape=[o[1] for o in outs],
        compiler_params=pltpu.CompilerParams(
            dimension_semantics=("arbitrary",), vmem_limit_bytes=VMEM_LIMIT),
    )(x2, sc1p, shift, norm_g, w, qg, kg)


def _fold(t, op):
    parts = [t[8 * i:8 * i + 8, :] for i in range(TILE // 8)]
    while len(parts) > 1:
        parts = [op(parts[i], parts[i + 1]) for i in range(0, len(parts), 2)]
    return parts[0]


def _attn_kernel(q_ref, k_ref, vt_ref, qi_ref, ki_ref, wit_ref, bias_ref, o_ref,
                 key_scr, khi_scr, klo_scr, klo2_scr, qih_scr, qh_scr, mb_scr, s_scr, m_scr, acc_scr,
                 *, topk, pos_bits):
    qb = pl.program_id(1)
    nst = ((qb + 1) * (QW // TILE) - 1) // GROUP + 1
    kf = float(topk)
    zgap = 1 << pos_bits
    lane = lax.broadcasted_iota(jnp.int32, (TILE, QW), 1)
    krow = lax.broadcasted_iota(jnp.int32, (TILE, QW), 0)
    lo_half = lax.broadcasted_iota(jnp.int32, (QW, LANES), 1) < HEAD_DIM

    wit = wit_ref[...]
    qi = qi_ref[0]
    q = q_ref[0]
    for src, dst, n_pairs in ((qi, qih_scr, N_IDX_HEADS // 2), (q, qh_scr, H_A // 2)):
        for g in range(n_pairs):
            pair = src[:, g * LANES:(g + 1) * LANES]
            dst[g, 0:QW, :] = jnp.where(lo_half, pair, jnp.zeros_like(pair))
            dst[g, QW:2 * QW, :] = jnp.where(lo_half, jnp.zeros_like(pair), pair)

    key_chunk = krow // CHUNK
    query_chunk = qb * (QW // CHUNK) + lane // CHUNK
    inadm = NEG_KEY - zgap - 1

    def score_key(s):
        b = lax.bitcast_convert_type(s, jnp.int32)
        return jnp.where(b < 0, (b ^ jnp.int32(0x7FFFFFFF)) - zgap, b)

    def p1(st, c):
        base = st * GROUP
        kis = ki_ref[0, pl.ds(pl.multiple_of(base * TILE, GROUP * TILE), GROUP * TILE), :]
        acc = [None] * GROUP
        for g in range(N_IDX_HEADS // 2):
            d = lax.dot_general(kis, qih_scr[g], NT_DIMS, preferred_element_type=F32)
            for par in range(2):
                w_row = wit[2 * g + par:2 * g + par + 1, :]
                for u in range(GROUP):
                    t = jnp.maximum(d[u * TILE:(u + 1) * TILE, par * QW:(par + 1) * QW], 0.0) * w_row
                    acc[u] = t if acc[u] is None else acc[u] + t
        for u in range(GROUP):
            kt = base + u
            key = jnp.where(acc[u] == 0.0, -1 - (krow + kt * TILE), score_key(acc[u]))
            ok = key_chunk + kt * (TILE // CHUNK) <= query_chunk
            key = jnp.where(ok, key, inadm)
            key_scr[kt] = key
            khi_scr[kt] = (key >> 16).astype(jnp.int16)
            klo_scr[kt] = ((key & (2 * HALF - 1)) - HALF).astype(jnp.int16)
        k_s = k_ref[0, pl.ds(pl.multiple_of(base * TILE, GROUP * TILE), GROUP * TILE), :]
        for g in range(H_A // 2):
            s = lax.dot_general(k_s[:, g * LANES:(g + 1) * LANES], qh_scr[g], NT_DIMS,
                                preferred_element_type=F32)
            for par in range(2):
                h = 2 * g + par
                mx = m_scr[h]
                for u in range(GROUP):
                    dd = jnp.clip(base + u - qb * (QW // TILE), -N_FAR, QW // TILE) + N_FAR
                    su = (s[u * TILE:(u + 1) * TILE, par * QW:(par + 1) * QW]
                          + bias_ref[h * N_BIAS + dd])
                    s_scr[h, base + u] = su
                    mx = jnp.maximum(mx, _fold(su, jnp.maximum))
                m_scr[h] = mx
        return c

    for h in range(H_A):
        m_scr[h] = jnp.full((8, QW), NEG_BIG, F32)
    lax.fori_loop(0, nst, p1, 0)

    def count_ge(thr_row):
        def body(st, cs):
            return tuple(c + _fold(jnp.where(key_scr[st * GROUP + u] >= thr_row, 1.0, 0.0), jnp.add)
                         for u, c in enumerate(cs))
        cs = lax.fori_loop(0, nst, body, (jnp.zeros((8, QW), F32),) * GROUP)
        return jnp.sum(functools.reduce(jnp.add, cs), axis=0, keepdims=True)

    def fold16(t):
        parts = [t[16 * i:16 * i + 16, :] for i in range(TILE // 16)]
        while len(parts) > 1:
            parts = [parts[i] + parts[i + 1] for i in range(0, len(parts), 2)]
        return parts[0]

    one16 = jnp.ones((TILE, QW), jnp.int16)
    zero16 = jnp.zeros((TILE, QW), jnp.int16)

    def as16(row):
        return jnp.broadcast_to(row, (TILE, QW)).astype(jnp.int16)

    def count16(scr, thr_row):
        thr16 = as16(thr_row)

        def body(st, cs):
            return tuple(c + fold16(jnp.where(scr[st * GROUP + u] >= thr16, one16, zero16))
                         for u, c in enumerate(cs))
        cs = lax.fori_loop(0, nst, body, (jnp.zeros((16, QW), jnp.int16),) * GROUP)
        total = functools.reduce(jnp.add, [c.astype(F32) for c in cs])
        return jnp.sum(total, axis=0, keepdims=True)

    def hi_step(_, carry):
        lo, hi, c_lo, c_hi = carry
        mid = (lo + hi) >> 1
        cnt = count16(khi_scr, mid)
        ge = cnt >= kf
        return (jnp.where(ge, mid, lo), jnp.where(ge, hi, mid),
                jnp.where(ge, cnt, c_lo), jnp.where(ge, c_hi, cnt))

    n_keys = (nst * (GROUP * TILE)).astype(F32)
    top, _, c_top, c_above = lax.fori_loop(
        0, 16, hi_step, (jnp.full((1, QW), -HALF, jnp.int32), jnp.full((1, QW), HALF, jnp.int32),
                         jnp.broadcast_to(n_keys, (1, QW)), jnp.zeros((1, QW), F32)))

    top16 = as16(top)
    min16 = jnp.full((TILE, QW), -HALF, jnp.int16)

    def keep_bucket(kt, c):
        klo2_scr[kt] = jnp.where(khi_scr[kt] == top16, klo_scr[kt], min16)
        return c

    lax.fori_loop(0, nst * GROUP, keep_bucket, 0)
    need = kf - c_above

    def lo_step(lo, hi, c_lo):
        mid = (lo + hi) >> 1
        cnt = count16(klo2_scr, mid)
        active = hi != lo + 1
        ge = jnp.logical_and(active, cnt >= need)
        hi = jnp.where(jnp.logical_and(active, cnt == need), mid + 1,
                       jnp.where(jnp.logical_and(active, cnt < need), mid, hi))
        return jnp.where(ge, mid, lo), hi, jnp.where(ge, cnt, c_lo)

    def unsettled(lo, hi):
        return jnp.max(jnp.where(hi != lo + 1, 1.0, 0.0)) > 0.0

    def late(carry):
        lo, hi, c_lo, it, _ = carry
        lo, hi, c_lo = lo_step(*lo_step(lo, hi, c_lo))
        return lo, hi, c_lo, it + 2, unsettled(lo, hi)

    low0 = (jnp.full((1, QW), -HALF, jnp.int32), jnp.full((1, QW), HALF, jnp.int32), c_top - c_above)
    low = lax.fori_loop(0, 16 - LATE_STEPS, lambda _, c: lo_step(*c), low0)
    low, _, c_low, _, _ = lax.while_loop(lambda carry: jnp.logical_and(carry[3] < 16, carry[4]),
                                         late, (*low, 16 - LATE_STEPS, unsettled(low[0], low[1])))
    thr = top * (2 * HALF) + (low + HALF)
    cnt_ge = c_above + c_low
    tie = jnp.logical_and(cnt_ge > kf, thr > inadm)

    @pl.when(jnp.max(jnp.where(tie, 1.0, 0.0)) > 0.0)
    def _():
        need = kf - count_ge(thr + 1)

        def count_eq_le(pmax):
            def body(kt, c):
                hit = jnp.where(key_scr[kt] == thr, krow + kt * TILE, INT_MAX) <= pmax
                return c + _fold(jnp.where(hit, 1.0, 0.0), jnp.add)
            c = lax.fori_loop(0, nst * GROUP, body, jnp.zeros((8, QW), F32))
            return jnp.sum(c, axis=0, keepdims=True)

        def pbisect(_, lohi):
            plo, phi = lohi
            pmid = (plo + phi) >> 1
            ok = count_eq_le(pmid) >= need
            return jnp.where(ok, plo, pmid), jnp.where(ok, pmid, phi)

        plo0 = jnp.full((1, QW), -1, jnp.int32)
        phi0 = jnp.full((1, QW), (1 << pos_bits) - 1, jnp.int32)
        _, pthr = lax.fori_loop(0, pos_bits + 1, pbisect, (plo0, phi0))
        pthr = jnp.where(tie, pthr, INT_MAX)

        def drop(kt, c):
            kk = key_scr[kt]
            pos = jnp.where(kk == thr, krow + kt * TILE, -1)
            key_scr[kt] = jnp.where(pos > pthr, inadm, kk)
            return c

        lax.fori_loop(0, nst * GROUP, drop, 0)

    thr_eff = jnp.maximum(thr, inadm + 1)

    ones_rows = jnp.ones((ONES_ROWS, GROUP * TILE), BF16)

    def collapse_max():
        for h in range(H_A):
            m_scr[h] = jnp.broadcast_to(jnp.max(m_scr[h], axis=0, keepdims=True), (8, QW))

    def select_bias(base):
        for u in range(GROUP):
            mb_scr[u] = jnp.where(key_scr[base + u] >= thr_eff, 0.0, NEG_BIG)

    def pv(st, c):
        select_bias(st * GROUP)
        for h in range(H_A):
            shift = m_scr[h][0:1, :]
            ps = [jnp.exp((s_scr[h, st * GROUP + u] + mb_scr[u] - shift).astype(BF16))
                  for u in range(GROUP)]
            lhs = jnp.concatenate([vt_ref[st, h * HEAD_DIM:(h + 1) * HEAD_DIM, :], ones_rows], axis=0)
            acc_scr[h] += jnp.dot(lhs, jnp.concatenate(ps, axis=0), preferred_element_type=F32)
        return c

    def run_pv():
        for h in range(H_A):
            acc_scr[h] = jnp.zeros((HEAD_DIM + ONES_ROWS, QW), F32)
        lax.fori_loop(0, nst, pv, 0)

    collapse_max()
    run_pv()
    den_min = acc_scr[0][HEAD_DIM:HEAD_DIM + 1, :]
    for h in range(1, H_A):
        den_min = jnp.minimum(den_min, acc_scr[h][HEAD_DIM:HEAD_DIM + 1, :])

    @pl.when(jnp.logical_not(jnp.min(den_min) > DEN_FLOOR))
    def _():
        for h in range(H_A):
            m_scr[h] = jnp.full((8, QW), NEG_BIG, F32)

        def selected_max(st, c):
            select_bias(st * GROUP)
            for h in range(H_A):
                mx = m_scr[h]
                for u in range(GROUP):
                    mx = jnp.maximum(mx, _fold(s_scr[h, st * GROUP + u] + mb_scr[u], jnp.maximum))
                m_scr[h] = mx
            return c

        lax.fori_loop(0, nst, selected_max, 0)
        collapse_max()
        run_pv()

    outs = []
    for h in range(H_A):
        acc = acc_scr[h]
        outs.append(acc[0:HEAD_DIM, :] / acc[HEAD_DIM:HEAD_DIM + 1, :])
    o_ref[0] = jnp.concatenate(outs, axis=0).T.astype(BF16)


def _attention(q, k, vt, qi, ki, wit, bias_tiles, topk):
    b, s, _ = q.shape
    nt = s // TILE
    nq = s // QW
    ng = s // (GROUP * TILE)
    kern = functools.partial(_attn_kernel, topk=topk, pos_bits=max(1, int(math.ceil(math.log2(s)))))
    qtile = lambda bb, i: (bb, i, 0)
    whole = lambda bb, i: (bb, 0, 0)
    once = pl.Buffered(1)
    return pl.pallas_call(
        kern,
        grid=(b, nq),
        in_specs=[pl.BlockSpec((1, QW, D_A), qtile),
                  pl.BlockSpec((1, s, D_A), whole, pipeline_mode=once),
                  pl.BlockSpec((ng, D_A, GROUP * TILE), whole, pipeline_mode=once),
                  pl.BlockSpec((1, QW, 512), qtile),
                  pl.BlockSpec((1, s, LANES), whole, pipeline_mode=once),
                  pl.BlockSpec((N_IDX_HEADS, QW), lambda bb, i: (0, bb * nq + i)),
                  pl.BlockSpec(bias_tiles.shape, lambda bb, i: (0, 0, 0), pipeline_mode=once)],
        out_specs=pl.BlockSpec((1, QW, D_A), qtile),
        out_shape=jax.ShapeDtypeStruct((b, s, D_A), BF16),
        scratch_shapes=[pltpu.VMEM((nt, TILE, QW), jnp.int32),
                        pltpu.VMEM((nt, TILE, QW), jnp.int16),
                        pltpu.VMEM((nt, TILE, QW), jnp.int16),
                        pltpu.VMEM((nt, TILE, QW), jnp.int16),
                        pltpu.VMEM((N_IDX_HEADS // 2, 2 * QW, LANES), BF16),
                        pltpu.VMEM((H_A // 2, 2 * QW, LANES), BF16),
                        pltpu.VMEM((GROUP, TILE, QW), F32),
                        pltpu.VMEM((H_A, nt, TILE, QW), F32),
                        pltpu.VMEM((H_A, 8, QW), F32),
                        pltpu.VMEM((H_A, HEAD_DIM + ONES_ROWS, QW), F32)],
        compiler_params=pltpu.CompilerParams(
            dimension_semantics=("arbitrary", "arbitrary"), vmem_limit_bytes=ATTN_VMEM_LIMIT),
    )(q, k, vt, qi, ki, wit, bias_tiles)


def _t5_bucket(rel):
    nb = N_BUCKETS // 2
    max_exact = nb // 2
    ret = jnp.where(rel > 0, nb, 0)
    n = jnp.abs(rel)
    nf = jnp.maximum(n, 1).astype(F32)
    large = max_exact + (jnp.log(nf / max_exact) / math.log(MAX_DISTANCE / max_exact)
                         * (nb - max_exact)).astype(jnp.int32)
    large = jnp.minimum(large, nb - 1)
    return ret + jnp.where(n < max_exact, n, large)


def _bias_tiles(rel_bias):
    n = TILE + QW - 1
    n_off = N_BIAS - 1
    m = np.arange(n)
    q_minus_k = np.where(m < QW, m, m - n)
    first_key = (np.arange(n_off)[:, None] - N_FAR) * TILE
    rel = first_key - q_minus_k[None, :]
    u = rel_bias[_t5_bucket(jnp.asarray(rel, jnp.int32))].astype(F32)
    u = u.transpose(2, 0, 1)
    flat = jnp.tile(u, (1, 1, TILE))[:, :, :TILE * (n - 1)]
    tiles = flat.reshape(H_A, n_off, TILE, n - 1)[..., :QW]
    j = np.arange(TILE)[None, :, None]
    t = np.arange(QW)[None, None, :]
    admissible = (first_key[:, :, None] + j) // CHUNK <= t // CHUNK
    tiles = jnp.where(jnp.asarray(admissible)[None], tiles, NEG_BIG)
    tiles = jnp.concatenate([tiles, jnp.full((H_A, 1, TILE, QW), NEG_BIG, F32)], axis=1)
    return tiles.reshape(H_A * N_BIAS, TILE, QW)


CONV_HALO = 32


def _conv_kernel(cur_ref, halo_ref, w_ref, b_ref, g_ref, beta_ref, o_ref, u_scr, sh_scr, *, tt):
    i = pl.program_id(1)

    def glu(t):
        return t[:, :D_B] * _sigmoid(t[:, D_B:])

    u_scr[0:CONV_HALO, :] = jnp.where(i > 0, glu(halo_ref[0]), 0.0)
    u_scr[CONV_HALO:CONV_HALO + tt, :] = glu(cur_ref[0])
    rows = 64
    first = CONV_HALO - (CONV_WIDTH - 1)
    for r0 in range(0, tt, rows):
        acc = jnp.broadcast_to(b_ref[...], (rows, D_B))
        for res in range(8):
            offs = [first + j for j in range(CONV_WIDTH) if (first + j) % 8 == res]
            span = offs[-1] - offs[0] + rows
            sh_scr[0:span, :] = u_scr[r0 + offs[0]:r0 + offs[0] + span, :]
            for off in offs:
                acc = acc + w_ref[off - first:off - first + 1, :] * sh_scr[off - offs[0]:off - offs[0] + rows, :]
        mu = jnp.mean(acc, axis=-1, keepdims=True)
        cen = acc - mu
        var = jnp.mean(cen * cen, axis=-1, keepdims=True)
        y = cen * lax.rsqrt(var + EPS) * g_ref[...] + beta_ref[...]
        o_ref[0, r0:r0 + rows, :] = _silu(y).astype(BF16)


def _conv(conv_in, w, b, g, beta, tt):
    bsz, s, _ = conv_in.shape
    per = tt // CONV_HALO
    vec = lambda bb, i: (0, 0)
    return pl.pallas_call(
        functools.partial(_conv_kernel, tt=tt),
        grid=(bsz, s // tt),
        in_specs=[pl.BlockSpec((1, tt, 2 * D_B), lambda bb, i: (bb, i, 0)),
                  pl.BlockSpec((1, CONV_HALO, 2 * D_B),
                               lambda bb, i: (bb, jnp.maximum(i * per - 1, 0), 0)),
                  pl.BlockSpec((CONV_WIDTH, D_B), vec),
                  pl.BlockSpec((1, D_B), vec),
                  pl.BlockSpec((1, D_B), vec),
                  pl.BlockSpec((1, D_B), vec)],
        out_specs=pl.BlockSpec((1, tt, D_B), lambda bb, i: (bb, i, 0)),
        out_shape=jax.ShapeDtypeStruct((bsz, s, D_B), BF16),
        scratch_shapes=[pltpu.VMEM((CONV_HALO + tt, D_B), F32),
                        pltpu.VMEM((CONV_HALO + 64, D_B), F32)],
        compiler_params=pltpu.CompilerParams(
            dimension_semantics=("arbitrary", "arbitrary"), vmem_limit_bytes=VMEM_LIMIT),
    )(conv_in, conv_in, w, b, g, beta)


def _rwkv_kernel(rw_ref, mu_ref, lora_ref, w0_ref, a0_ref, kk_ref, ka_ref, rk_ref, lg_ref, lb_ref,
                 o_ref, s_scr, prev_scr):
    c = pl.program_id(1)
    n = CHUNK
    nb = RW_CHUNKS * CHUNK

    @pl.when(c == 0)
    def _():
        s_scr[...] = jnp.zeros_like(s_scr)
        prev_scr[...] = jnp.zeros_like(prev_scr)

    ps = rw_ref[0]
    row = lax.broadcasted_iota(jnp.int32, (nb, 1), 0)
    prev = jnp.where(row == 0, prev_scr[0:1, :], pltpu.roll(ps, 1, axis=0))
    prev_scr[0:1, :] = ps[nb - 1:nb, :]
    xs = ps + mu_ref[...] * (prev - ps)
    r = xs[:, 0:D_C]
    k = xs[:, D_C:2 * D_C]
    v = xs[:, 2 * D_C:3 * D_C]
    dn = xs[:, 3 * D_C:3 * D_C + 2 * D_LORA]
    lane64 = lax.broadcasted_iota(jnp.int32, (nb, 2 * D_LORA), 1)
    dn = jnp.where(lane64 < D_LORA, jnp.tanh(dn), dn)
    dn_hi, dn_lo = _split2(dn)
    lora_hi, lora_lo = _split2(lora_ref[...])
    pre = _bdot(dn_hi, lora_hi) + _bdot(dn_hi, lora_lo) + _bdot(dn_lo, lora_hi)
    z = -(w0_ref[...] + pre[:, :D_C])
    softplus = jnp.maximum(z, 0.0) + jnp.log(1.0 + jnp.exp(-jnp.abs(z)))
    logdec = -jnp.exp(-softplus - 0.5)
    a = _sigmoid(a0_ref[...] + pre[:, D_C:])
    kkraw = k * kk_ref[...]
    k2 = k * (1.0 + (a - 1.0) * ka_ref[...])
    bonus_pre = r * k2 * rk_ref[...]

    step_in_chunk = row & (n - 1)
    cum_all = logdec
    shift = 1
    while shift < n:
        cum_all = cum_all + jnp.where(step_in_chunk >= shift, pltpu.roll(cum_all, shift, axis=0), 0.0)
        shift *= 2
    p_in_all = jnp.exp(cum_all)
    p_inv_all = jnp.exp(-cum_all)
    p_ex_all = jnp.exp(cum_all - logdec)

    row_n = lax.broadcasted_iota(jnp.int32, (n, LANES), 0)
    lane_n = lax.broadcasted_iota(jnp.int32, (n, LANES), 1)
    lo = lane_n < HEAD_DIM
    col = lane_n & (HEAD_DIM - 1)
    incl = col <= row_n
    strict = col < row_n
    mask2 = jnp.concatenate([strict, incl], axis=0)
    r2 = lax.broadcasted_iota(jnp.int32, (2 * n, LANES), 0)
    l2 = lax.broadcasted_iota(jnp.int32, (2 * n, LANES), 1)
    same_head = (r2 // HEAD_DIM) == (l2 // HEAD_DIM)

    def bd(t):
        tb = t.astype(BF16)
        z = jnp.zeros_like(tb)
        return jnp.concatenate([jnp.where(lo, tb, z), jnp.where(lo, z, tb)], axis=0)

    def hsum(t):
        s_lo = jnp.sum(jnp.where(lo, t, 0.0), axis=-1, keepdims=True)
        s_hi = jnp.sum(jnp.where(lo, 0.0, t), axis=-1, keepdims=True)
        return jnp.where(lo, s_lo, s_hi)

    pairs = range(H_C // 2)
    chains = [(j, g) for j in range(RW_CHUNKS) for g in pairs]
    ids = range(len(chains))
    sl = lambda t, i: t[chains[i][0] * n:(chains[i][0] + 1) * n,
                        chains[i][1] * LANES:(chains[i][1] + 1) * LANES]
    v_p = [sl(v, i) for i in ids]
    p_in = [sl(p_in_all, i) for i in ids]
    kkn = []
    for i in ids:
        kr = sl(kkraw, i)
        kkn.append(kr * lax.rsqrt(jnp.maximum(hsum(kr * kr), 1e-12)))
    rt = [sl(r, i) * p_in[i] for i in ids]
    at = [kkn[i] * sl(p_ex_all, i) for i in ids]
    kt = [sl(k2, i) * sl(p_inv_all, i) for i in ids]
    bt = [kkn[i] * sl(a, i) * sl(p_inv_all, i) for i in ids]
    lhs = [jnp.concatenate([at[i], rt[i]], axis=0).astype(BF16) for i in ids]
    ak = [lax.dot_general(lhs[i], bd(kt[i]), NT_DIMS, preferred_element_type=F32) for i in ids]
    ab = [lax.dot_general(lhs[i], bd(bt[i]), NT_DIMS, preferred_element_type=F32) for i in ids]
    av = [_bdot(jnp.where(mask2, ak[i], 0.0), bd(v_p[i])) for i in ids]
    a_rb = [jnp.where(incl, ab[i][n:], 0.0) for i in ids]
    npow = [jnp.where(strict, -ab[i][:n], 0.0) for i in ids]
    tm = list(npow)
    for _ in range(5):
        both = [_bdot(jnp.concatenate([npow[i], tm[i]], axis=0), bd(npow[i])) for i in ids]
        npow = [both[i][:n] for i in ids]
        tm = [tm[i] + both[i][n:] for i in ids]
    tm = [tm[i] + _bdot(tm[i], bd(npow[i])) for i in ids]
    wt = [at[i] + _bdot(tm[i], bd(at[i])) for i in ids]
    uloc = [av[i][:n] + _bdot(tm[i], bd(av[i][:n])) for i in ids]
    m_t = [jnp.concatenate([kt[i], bt[i]], axis=0).T.astype(BF16) for i in ids]
    eye = r2 == l2
    p_col = [jnp.sum(jnp.where(eye, p_in[i][n - 1:n, :], 0.0), axis=-1, keepdims=True) for i in ids]
    bonus = [hsum(sl(bonus_pre, i)) * v_p[i] for i in ids]

    state = [s_scr[g] for g in pairs]
    for j in range(RW_CHUNKS):
        cid = [j * (H_C // 2) + g for g in pairs]
        ws = [_bdot(jnp.concatenate([wt[i], rt[i]], axis=0), state[g])
              for g, i in zip(pairs, cid)]
        u = [ws[g][:n] + uloc[i] for g, i in zip(pairs, cid)]
        o = [ws[g][n:] + av[i][n:] - _bdot(a_rb[i], bd(u[g])) for g, i in zip(pairs, cid)]
        for g, i in zip(pairs, cid):
            ncat = jnp.concatenate([v_p[i], -u[g]], axis=0).astype(BF16)
            upd = jnp.dot(m_t[i], ncat, preferred_element_type=F32)
            state[g] = (state[g] + jnp.where(same_head, upd, 0.0)) * p_col[i]
        for g, i in zip(pairs, cid):
            lanes = slice(g * LANES, (g + 1) * LANES)
            cen = o[g] - hsum(o[g]) * (1.0 / HEAD_DIM)
            var = hsum(cen * cen) * (1.0 / HEAD_DIM)
            y = cen * lax.rsqrt(var + LNX_EPS) * lg_ref[:, lanes] + lb_ref[:, lanes]
            o_ref[0, j * n:(j + 1) * n, lanes] = (y + bonus[i]).astype(BF16)
    for g in pairs:
        s_scr[g] = state[g]


def _rwkv(rw, mu, lora, w0, a0, kk, ka, rk, lg, lb):
    bsz, s, _ = rw.shape
    vec = lambda bb, i: (0, 0)
    return pl.pallas_call(
        _rwkv_kernel,
        grid=(bsz, s // (RW_CHUNKS * CHUNK)),
        in_specs=[pl.BlockSpec((1, RW_CHUNKS * CHUNK, RW_COLS), lambda bb, i: (bb, i, 0)),
                  pl.BlockSpec((1, RW_COLS), vec),
                  pl.BlockSpec((2 * D_LORA, 2 * D_C), vec)] +
                 [pl.BlockSpec((1, D_C), vec)] * 7,
        out_specs=pl.BlockSpec((1, RW_CHUNKS * CHUNK, D_C), lambda bb, i: (bb, i, 0)),
        out_shape=jax.ShapeDtypeStruct((bsz, s, D_C), BF16),
        scratch_shapes=[pltpu.VMEM((H_C // 2, 2 * HEAD_DIM, LANES), F32),
                        pltpu.VMEM((8, RW_COLS), F32)],
        compiler_params=pltpu.CompilerParams(
            dimension_semantics=("arbitrary", "arbitrary"), vmem_limit_bytes=VMEM_LIMIT),
    )(rw, mu, lora, w0, a0, kk, ka, rk, lg, lb)


def _outproj_kernel(x_ref, attn_ref, conv_ref, rw_ref, gates_ref, gm_ref, w_ref, o_ref):
    sg = _silu(gates_ref[...].astype(F32))
    ya = (attn_ref[...].astype(F32) * sg[:, 0:D_A]).astype(BF16)
    yb = (conv_ref[...].astype(F32) * sg[:, D_A:D_A + D_B]).astype(BF16)
    yc = (rw_ref[...].astype(F32) * sg[:, D_A + D_B:D_MIX]).astype(BF16)
    y = (jnp.dot(ya, w_ref[0:D_A, :], preferred_element_type=F32)
         + jnp.dot(yb, w_ref[D_A:D_A + D_B, :], preferred_element_type=F32)
         + jnp.dot(yc, w_ref[D_A + D_B:D_MIX, :], preferred_element_type=F32))
    o_ref[...] = x_ref[...] + gm_ref[0] * y


def _outproj(x2, attn, conv, rw, gates, gate_mod, w, seq, tm):
    n, d = x2.shape
    per_b = seq // tm
    row = lambda i: (i, 0)
    return pl.pallas_call(
        _outproj_kernel,
        grid=(n // tm,),
        in_specs=[pl.BlockSpec((tm, d), row),
                  pl.BlockSpec((tm, D_A), row),
                  pl.BlockSpec((tm, D_B), row),
                  pl.BlockSpec((tm, D_C), row),
                  pl.BlockSpec((tm, D_MIX), row),
                  pl.BlockSpec((1, 1, d), lambda i: (i // per_b, 0, 0)),
                  pl.BlockSpec((D_MIX, d), lambda i: (0, 0))],
        out_specs=pl.BlockSpec((tm, d), row),
        out_shape=jax.ShapeDtypeStruct((n, d), F32),
        compiler_params=pltpu.CompilerParams(
            dimension_semantics=("arbitrary",), vmem_limit_bytes=VMEM_LIMIT),
    )(x2, attn, conv, rw, gates, gate_mod, w)


def _reorder_w_in(w_in):
    depth, d, _ = w_in.shape
    a_cols = 4 * D_A + N_IDX_HEADS * D_IDX + D_IDX + N_IDX_HEADS
    b0 = a_cols
    c0 = a_cols + 3 * D_B
    c_shift = 3 * D_C + 2 * D_LORA
    zeros = lambda m: jnp.zeros((depth, d, m), w_in.dtype)
    qi0 = 4 * D_A
    ki0 = qi0 + N_IDX_HEADS * D_IDX
    wi0 = ki0 + D_IDX
    parts = [w_in[..., 0:3 * D_A],
             w_in[..., qi0:ki0],
             w_in[..., ki0:wi0], w_in[..., ki0:wi0],
             w_in[..., wi0:a_cols], zeros(LANES - N_IDX_HEADS),
             w_in[..., 3 * D_A:4 * D_A], w_in[..., b0 + 2 * D_B:b0 + 3 * D_B],
             w_in[..., c0 + c_shift:c0 + c_shift + D_C],
             w_in[..., b0:b0 + 2 * D_B],
             w_in[..., c0:c0 + c_shift], zeros(RW_COLS - c_shift)]
    w = jnp.concatenate(parts, axis=-1)
    assert w.shape[-1] == N_COLS
    return w.astype(BF16)


def kernel(x, c, norm_g, w_ada, b_ada, w_in, w_out, q_norm_g, k_norm_g, rel_bias, conv_w, conv_b,
           conv_ln_g, conv_ln_b, shift_mu, decay_w0, decay_up, iclr_a0, iclr_up, key_k, key_a,
           bonus_r_k, lnx_g, lnx_b):
    bsz, seq, d = x.shape
    depth = w_in.shape[0]
    assert seq % (GROUP * TILE) == 0 and d == w_out.shape[-1]
    topk = min(TOPK_MAX, seq // 4)
    tm = GROUP * TILE
    conv_tt = 512
    tm_out = 2 * tm if seq % (2 * tm) == 0 else tm

    mod = _modulation(c, w_ada, b_ada)
    shift = mod[:, :, None, 0:d]
    sc1p = 1.0 + mod[:, :, None, d:2 * d]
    gate_mod = mod[:, :, None, 2 * d:3 * d]

    w_all = _reorder_w_in(w_in)
    qg = jnp.tile(q_norm_g, (1, H_A))[:, None, :] * (HEAD_DIM ** -0.5)
    kg = jnp.tile(k_norm_g, (1, H_A))[:, None, :]
    bias_tiles = _bias_tiles(rel_bias)
    c_shift = 3 * D_C + 2 * D_LORA
    mu_pad = jnp.pad(shift_mu, ((0, 0), (0, RW_COLS - c_shift)))[:, None, :]
    zl = jnp.zeros((depth, D_LORA, D_C), F32)
    lora = jnp.concatenate([jnp.concatenate([decay_up, zl], axis=2),
                            jnp.concatenate([zl, iclr_up], axis=2)], axis=1)
    row = lambda t: t[:, None, :]
    layers = dict(
        shift=shift, sc1p=sc1p, gate_mod=gate_mod, norm_g=row(norm_g), w=w_all, qg=qg, kg=kg,
        w_out=w_out.astype(BF16), conv_w=conv_w, conv_b=row(conv_b), conv_g=row(conv_ln_g),
        conv_beta=row(conv_ln_b), mu=mu_pad, lora=lora, w0=row(decay_w0), a0=row(iclr_a0),
        kk=row(key_k), ka=row(key_a), rk=row(bonus_r_k.reshape(depth, D_C)), lg=row(lnx_g),
        lb=row(lnx_b))

    def layer(x2, p):
        q, k, vt, qi, ki, wit, gates, conv_in, rw_in = _inproj(
            x2, p["sc1p"], p["shift"], p["norm_g"], p["w"], p["qg"], p["kg"], seq, tm)
        r3 = lambda t: t.reshape(bsz, seq, t.shape[-1])
        attn = _attention(r3(q), r3(k), vt, r3(qi), r3(ki), wit, bias_tiles, topk)
        conv = _conv(r3(conv_in), p["conv_w"], p["conv_b"], p["conv_g"], p["conv_beta"], conv_tt)
        rwo = _rwkv(r3(rw_in), p["mu"], p["lora"], p["w0"], p["a0"], p["kk"], p["ka"], p["rk"],
                    p["lg"], p["lb"])
        x2 = _outproj(x2, attn.reshape(-1, D_A), conv.reshape(-1, D_B), rwo.reshape(-1, D_C),
                      gates, p["gate_mod"], p["w_out"], seq, tm_out)
        return x2

    x2 = x.reshape(bsz * seq, d)
    for l in range(depth):
        x2 = layer(x2, {name: t[l] for name, t in layers.items()})
    return x2.reshape(bsz, seq, d)
```

```python
import functools
import math

import jax
import jax.numpy as jnp
import numpy as np
from jax import lax
from jax.experimental import pallas as pl
from jax.experimental.pallas import tpu as pltpu

F32 = jnp.float32
BF16 = jnp.bfloat16

HEAD_DIM = 64
CHUNK = 64
EPS = 1e-6
H_A = 6
D_A = H_A * HEAD_DIM
N_IDX_HEADS = 8
D_IDX = 64
TOPK_MAX = 256
N_BUCKETS = 32
MAX_DISTANCE = 1024
D_B = 256
CONV_WIDTH = 31
H_C = 6
D_C = H_C * HEAD_DIM
D_LORA = 32
LNX_EPS = 64e-5
D_MIX = D_A + D_B + D_C

LANES = 128
TILE = 128
GROUP = 4
HALF = 1 << 15
DEN_FLOOR = 1e-30
LATE_STEPS = 6
ONES_ROWS = 16
RW_CHUNKS = 8
QW = 256
N_FAR = 9
N_BIAS = N_FAR + QW // TILE + 1
ATTN_VMEM_LIMIT = 60 * 1024 * 1024
VMEM_LIMIT = 48 * 1024 * 1024

C_Q, C_K, C_V = 0, 384, 768
C_QI = 1152
C_KI = 1664
C_WI = 1792
C_GATES = 1920
C_CONV = 2944
C_RW = 3456
N_COLS = 4736
RW_COLS = 1280

INT_MAX = 2 ** 31 - 1
NEG_KEY = int(np.array(-np.inf, np.float32).view(np.int32)) ^ 0x7FFFFFFF
NEG_BIG = -1e30

NT_DIMS = (((1,), (1,)), ((), ()))


def _bdot(a, b):
    return jnp.dot(a.astype(BF16), b.astype(BF16), preferred_element_type=F32)


def _split2(x):
    hi = x.astype(BF16).astype(F32)
    return hi, x - hi


def _sigmoid(x):
    return 1.0 / (1.0 + jnp.exp(-x))


def _silu(x):
    return x * _sigmoid(x)


def _mod_kernel(c_ref, w_ref, b_ref, o_ref):
    ca = _silu(c_ref[...])
    o_ref[0] = jnp.dot(ca, w_ref[0], precision=lax.Precision.HIGHEST,
                       preferred_element_type=F32) + b_ref[0]


def _modulation(c, w_ada, b_ada):
    depth, d, d3 = w_ada.shape
    b = c.shape[0]
    return pl.pallas_call(
        _mod_kernel,
        grid=(depth, d3 // d),
        in_specs=[pl.BlockSpec((b, d), lambda l, j: (0, 0)),
                  pl.BlockSpec((1, d, d), lambda l, j: (l, 0, j)),
                  pl.BlockSpec((1, 1, d), lambda l, j: (l, 0, j))],
        out_specs=pl.BlockSpec((1, b, d), lambda l, j: (l, 0, j)),
        out_shape=jax.ShapeDtypeStruct((depth, b, d3), F32),
        compiler_params=pltpu.CompilerParams(
            dimension_semantics=("arbitrary", "arbitrary"), vmem_limit_bytes=VMEM_LIMIT),
    )(c, w_ada, b_ada.reshape(depth, 1, d3))


def _inproj_kernel(x_ref, sc_ref, sh_ref, g_ref, w_ref, qg_ref, kg_ref,
                   q_ref, k_ref, vt_ref, qi_ref, ki_ref, wit_ref, gates_ref, conv_ref, rw_ref):
    x = x_ref[...]
    ms = jnp.mean(x * x, axis=-1, keepdims=True)
    h = x * lax.rsqrt(ms + EPS) * g_ref[...]
    hb = (h * sc_ref[0] + sh_ref[0]).astype(BF16)

    def mm(c0, c1):
        return jnp.dot(hb, w_ref[:, c0:c1], preferred_element_type=F32)

    lo = lax.broadcasted_iota(jnp.int32, (x.shape[0], LANES), 1) < HEAD_DIM

    def head_rms(t, g):
        sq = t * t
        means = []
        for grp in range(D_A // LANES):
            blk = sq[:, grp * LANES:(grp + 1) * LANES]
            s_lo = jnp.sum(jnp.where(lo, blk, 0.0), axis=-1, keepdims=True)
            s_hi = jnp.sum(jnp.where(lo, 0.0, blk), axis=-1, keepdims=True)
            means.append(jnp.where(lo, s_lo, s_hi) * (1.0 / HEAD_DIM))
        return t * lax.rsqrt(jnp.concatenate(means, axis=1) + EPS) * g

    q_ref[...] = head_rms(mm(C_Q, C_Q + D_A), qg_ref[...]).astype(BF16)
    k_ref[...] = head_rms(mm(C_K, C_K + D_A), kg_ref[...]).astype(BF16)
    vt_ref[0] = mm(C_V, C_V + D_A).T.astype(BF16)
    qi_ref[...] = mm(C_QI, C_QI + 512).astype(BF16)
    ki_ref[...] = mm(C_KI, C_KI + LANES).astype(BF16)
    wit_ref[...] = mm(C_WI, C_WI + LANES).T[:N_IDX_HEADS, :]
    gates_ref[:, 0:512] = mm(C_GATES, C_GATES + 512).astype(BF16)
    gates_ref[:, 512:1024] = mm(C_GATES + 512, C_GATES + 1024).astype(BF16)
    conv_ref[...] = mm(C_CONV, C_CONV + 512)
    rw_ref[:, 0:512] = mm(C_RW, C_RW + 512)
    rw_ref[:, 512:1024] = mm(C_RW + 512, C_RW + 1024)
    rw_ref[:, 1024:RW_COLS] = mm(C_RW + 1024, C_RW + RW_COLS)


def _inproj(x2, sc1p, shift, norm_g, w, qg, kg, seq, tm):
    n, d = x2.shape
    per_b = seq // tm
    row = lambda i: (i, 0)
    full = lambda i: (0, 0)
    bat = lambda i: (i // per_b, 0, 0)
    rows = lambda wd, dt: (pl.BlockSpec((tm, wd), row), jax.ShapeDtypeStruct((n, wd), dt))
    outs = [rows(D_A, BF16), rows(D_A, BF16),
            (pl.BlockSpec((1, D_A, tm), lambda i: (i, 0, 0)),
             jax.ShapeDtypeStruct((n // tm, D_A, tm), BF16)),
            rows(512, BF16), rows(LANES, BF16),
            (pl.BlockSpec((N_IDX_HEADS, tm), lambda i: (0, i)),
             jax.ShapeDtypeStruct((N_IDX_HEADS, n), F32)),
            rows(D_MIX, BF16), rows(512, F32), rows(RW_COLS, F32)]
    return pl.pallas_call(
        _inproj_kernel,
        grid=(n // tm,),
        in_specs=[pl.BlockSpec((tm, d), row),
                  pl.BlockSpec((1, 1, d), bat),
                  pl.BlockSpec((1, 1, d), bat),
                  pl.BlockSpec((1, d), full),
                  pl.BlockSpec((d, N_COLS), full),
                  pl.BlockSpec((1, D_A), full),
                  pl.BlockSpec((1, D_A), full)],
        out_specs=[o[0] for o in outs],
        out_shape=[o[1] for o in outs],
        compiler_params=pltpu.CompilerParams(
            dimension_semantics=("arbitrary",), vmem_limit_bytes=VMEM_LIMIT),
    )(x2, sc1p, shift, norm_g, w, qg, kg)


def _fold(t, op):
    parts = [t[8 * i:8 * i + 8, :] for i in range(TILE // 8)]
    while len(parts) > 1:
        parts = [op(parts[i], parts[i + 1]) for i in range(0, len(parts), 2)]
    return parts[0]


def _attn_kernel(q_ref, k_ref, vt_ref, qi_ref, ki_ref, wit_ref, bias_ref, o_ref,
                 key_scr, khi_scr, klo_scr, klo2_scr, qih_scr, qh_scr, mb_scr, s_scr, m_scr, acc_scr,
                 *, topk, pos_bits):
    qb = pl.program_id(1)
    nst = ((qb + 1) * (QW // TILE) - 1) // GROUP + 1
    kf = float(topk)
    zgap = 1 << pos_bits
    lane = lax.broadcasted_iota(jnp.int32, (TILE, QW), 1)
    krow = lax.broadcasted_iota(jnp.int32, (TILE, QW), 0)
    lo_half = lax.broadcasted_iota(jnp.int32, (QW, LANES), 1) < HEAD_DIM

    wit = wit_ref[...]
    qi = qi_ref[0]
    q = q_ref[0]
    for src, dst, n_pairs in ((qi, qih_scr, N_IDX_HEADS // 2), (q, qh_scr, H_A // 2)):
        for g in range(n_pairs):
            pair = src[:, g * LANES:(g + 1) * LANES]
            dst[g, 0:QW, :] = jnp.where(lo_half, pair, jnp.zeros_like(pair))
            dst[g, QW:2 * QW, :] = jnp.where(lo_half, jnp.zeros_like(pair), pair)

    key_chunk = krow // CHUNK
    query_chunk = qb * (QW // CHUNK) + lane // CHUNK
    inadm = NEG_KEY - zgap - 1

    def score_key(s):
        b = lax.bitcast_convert_type(s, jnp.int32)
        return jnp.where(b < 0, (b ^ jnp.int32(0x7FFFFFFF)) - zgap, b)

    def p1(st, c):
        base = st * GROUP
        kis = ki_ref[0, pl.ds(pl.multiple_of(base * TILE, GROUP * TILE), GROUP * TILE), :]
        acc = [None] * GROUP
        for g in range(N_IDX_HEADS // 2):
            d = lax.dot_general(kis, qih_scr[g], NT_DIMS, preferred_element_type=F32)
            for par in range(2):
                w_row = wit[2 * g + par:2 * g + par + 1, :]
                for u in range(GROUP):
                    t = jnp.maximum(d[u * TILE:(u + 1) * TILE, par * QW:(par + 1) * QW], 0.0) * w_row
                    acc[u] = t if acc[u] is None else acc[u] + t
        for u in range(GROUP):
            kt = base + u
            key = jnp.where(acc[u] == 0.0, -1 - (krow + kt * TILE), score_key(acc[u]))
            ok = key_chunk + kt * (TILE // CHUNK) <= query_chunk
            key = jnp.where(ok, key, inadm)
            key_scr[kt] = key
            khi_scr[kt] = (key >> 16).astype(jnp.int16)
            klo_scr[kt] = ((key & (2 * HALF - 1)) - HALF).astype(jnp.int16)
        k_s = k_ref[0, pl.ds(pl.multiple_of(base * TILE, GROUP * TILE), GROUP * TILE), :]
        for g in range(H_A // 2):
            s = lax.dot_general(k_s[:, g * LANES:(g + 1) * LANES], qh_scr[g], NT_DIMS,
                                preferred_element_type=F32)
            for par in range(2):
                h = 2 * g + par
                mx = m_scr[h]
                for u in range(GROUP):
                    dd = jnp.clip(base + u - qb * (QW // TILE), -N_FAR, QW // TILE) + N_FAR
                    su = (s[u * TILE:(u + 1) * TILE, par * QW:(par + 1) * QW]
                          + bias_ref[h * N_BIAS + dd])
                    s_scr[h, base + u] = su
                    mx = jnp.maximum(mx, _fold(su, jnp.maximum))
                m_scr[h] = mx
        return c

    for h in range(H_A):
        m_scr[h] = jnp.full((8, QW), NEG_BIG, F32)
    lax.fori_loop(0, nst, p1, 0)

    def count_ge(thr_row):
        def body(st, cs):
            return tuple(c + _fold(jnp.where(key_scr[st * GROUP + u] >= thr_row, 1.0, 0.0), jnp.add)
                         for u, c in enumerate(cs))
        cs = lax.fori_loop(0, nst, body, (jnp.zeros((8, QW), F32),) * GROUP)
        return jnp.sum(functools.reduce(jnp.add, cs), axis=0, keepdims=True)

    def fold16(t):
        parts = [t[16 * i:16 * i + 16, :] for i in range(TILE // 16)]
        while len(parts) > 1:
            parts = [parts[i] + parts[i + 1] for i in range(0, len(parts), 2)]
        return parts[0]

    one16 = jnp.ones((TILE, QW), jnp.int16)
    zero16 = jnp.zeros((TILE, QW), jnp.int16)

    def as16(row):
        return jnp.broadcast_to(row, (TILE, QW)).astype(jnp.int16)

    def count16(scr, thr_row):
        thr16 = as16(thr_row)

        def body(st, cs):
            return tuple(c + fold16(jnp.where(scr[st * GROUP + u] >= thr16, one16, zero16))
                         for u, c in enumerate(cs))
        cs = lax.fori_loop(0, nst, body, (jnp.zeros((16, QW), jnp.int16),) * GROUP)
        total = functools.reduce(jnp.add, [c.astype(F32) for c in cs])
        return jnp.sum(total, axis=0, keepdims=True)

    def hi_step(_, carry):
        lo, hi, c_lo, c_hi = carry
        mid = (lo + hi) >> 1
        cnt = count16(khi_scr, mid)
        ge = cnt >= kf
        return (jnp.where(ge, mid, lo), jnp.where(ge, hi, mid),
                jnp.where(ge, cnt, c_lo), jnp.where(ge, c_hi, cnt))

    n_keys = (nst * (GROUP * TILE)).astype(F32)
    top, _, c_top, c_above = lax.fori_loop(
        0, 16, hi_step, (jnp.full((1, QW), -HALF, jnp.int32), jnp.full((1, QW), HALF, jnp.int32),
                         jnp.broadcast_to(n_keys, (1, QW)), jnp.zeros((1, QW), F32)))

    top16 = as16(top)
    min16 = jnp.full((TILE, QW), -HALF, jnp.int16)

    def keep_bucket(kt, c):
        klo2_scr[kt] = jnp.where(khi_scr[kt] == top16, klo_scr[kt], min16)
        return c

    lax.fori_loop(0, nst * GROUP, keep_bucket, 0)
    need = kf - c_above

    def lo_step(lo, hi, c_lo):
        mid = (lo + hi) >> 1
        cnt = count16(klo2_scr, mid)
        active = hi != lo + 1
        ge = jnp.logical_and(active, cnt >= need)
        hi = jnp.where(jnp.logical_and(active, cnt == need), mid + 1,
                       jnp.where(jnp.logical_and(active, cnt < need), mid, hi))
        return jnp.where(ge, mid, lo), hi, jnp.where(ge, cnt, c_lo)

    def unsettled(lo, hi):
        return jnp.max(jnp.where(hi != lo + 1, 1.0, 0.0)) > 0.0

    def late(carry):
        lo, hi, c_lo, it, _ = carry
        lo, hi, c_lo = lo_step(*lo_step(lo, hi, c_lo))
        return lo, hi, c_lo, it + 2, unsettled(lo, hi)

    low0 = (jnp.full((1, QW), -HALF, jnp.int32), jnp.full((1, QW), HALF, jnp.int32), c_top - c_above)
    low = lax.fori_loop(0, 16 - LATE_STEPS, lambda _, c: lo_step(*c), low0)
    low, _, c_low, _, _ = lax.while_loop(lambda carry: jnp.logical_and(carry[3] < 16, carry[4]),
                                         late, (*low, 16 - LATE_STEPS, unsettled(low[0], low[1])))
    thr = top * (2 * HALF) + (low + HALF)
    cnt_ge = c_above + c_low
    tie = jnp.logical_and(cnt_ge > kf, thr > inadm)

    @pl.when(jnp.max(jnp.where(tie, 1.0, 0.0)) > 0.0)
    def _():
        need = kf - count_ge(thr + 1)

        def count_eq_le(pmax):
            def body(kt, c):
                hit = jnp.where(key_scr[kt] == thr, krow + kt * TILE, INT_MAX) <= pmax
                return c + _fold(jnp.where(hit, 1.0, 0.0), jnp.add)
            c = lax.fori_loop(0, nst * GROUP, body, jnp.zeros((8, QW), F32))
            return jnp.sum(c, axis=0, keepdims=True)

        def pbisect(_, lohi):
            plo, phi = lohi
            pmid = (plo + phi) >> 1
            ok = count_eq_le(pmid) >= need
            return jnp.where(ok, plo, pmid), jnp.where(ok, pmid, phi)

        plo0 = jnp.full((1, QW), -1, jnp.int32)
        phi0 = jnp.full((1, QW), (1 << pos_bits) - 1, jnp.int32)
        _, pthr = lax.fori_loop(0, pos_bits + 1, pbisect, (plo0, phi0))
        pthr = jnp.where(tie, pthr, INT_MAX)

        def drop(kt, c):
            kk = key_scr[kt]
            pos = jnp.where(kk == thr, krow + kt * TILE, -1)
            key_scr[kt] = jnp.where(pos > pthr, inadm, kk)
            return c

        lax.fori_loop(0, nst * GROUP, drop, 0)

    thr_eff = jnp.maximum(thr, inadm + 1)

    ones_rows = jnp.ones((ONES_ROWS, GROUP * TILE), BF16)

    def collapse_max():
        for h in range(H_A):
            m_scr[h] = jnp.broadcast_to(jnp.max(m_scr[h], axis=0, keepdims=True), (8, QW))

    def select_bias(base):
        for u in range(GROUP):
            mb_scr[u] = jnp.where(key_scr[base + u] >= thr_eff, 0.0, NEG_BIG)

    def pv(st, c):
        select_bias(st * GROUP)
        for h in range(H_A):
            shift = m_scr[h][0:1, :]
            ps = [jnp.exp((s_scr[h, st * GROUP + u] + mb_scr[u] - shift).astype(BF16))
                  for u in range(GROUP)]
            lhs = jnp.concatenate([vt_ref[st, h * HEAD_DIM:(h + 1) * HEAD_DIM, :], ones_rows], axis=0)
            acc_scr[h] += jnp.dot(lhs, jnp.concatenate(ps, axis=0), preferred_element_type=F32)
        return c

    def run_pv():
        for h in range(H_A):
            acc_scr[h] = jnp.zeros((HEAD_DIM + ONES_ROWS, QW), F32)
        lax.fori_loop(0, nst, pv, 0)

    collapse_max()
    run_pv()
    den_min = acc_scr[0][HEAD_DIM:HEAD_DIM + 1, :]
    for h in range(1, H_A):
        den_min = jnp.minimum(den_min, acc_scr[h][HEAD_DIM:HEAD_DIM + 1, :])

    @pl.when(jnp.logical_not(jnp.min(den_min) > DEN_FLOOR))
    def _():
        for h in range(H_A):
            m_scr[h] = jnp.full((8, QW), NEG_BIG, F32)

        def selected_max(st, c):
            select_bias(st * GROUP)
            for h in range(H_A):
                mx = m_scr[h]
                for u in range(GROUP):
                    mx = jnp.maximum(mx, _fold(s_scr[h, st * GROUP + u] + mb_scr[u], jnp.maximum))
                m_scr[h] = mx
            return c

        lax.fori_loop(0, nst, selected_max, 0)
        collapse_max()
        run_pv()

    outs = []
    for h in range(H_A):
        acc = acc_scr[h]
        outs.append(acc[0:HEAD_DIM, :] / acc[HEAD_DIM:HEAD_DIM + 1, :])
    o_ref[0] = jnp.concatenate(outs, axis=0).T.astype(BF16)


def _attention(q, k, vt, qi, ki, wit, bias_tiles, topk):
    b, s, _ = q.shape
    nt = s // TILE
    nq = s // QW
    ng = s // (GROUP * TILE)
    kern = functools.partial(_attn_kernel, topk=topk, pos_bits=max(1, int(math.ceil(math.log2(s)))))
    qtile = lambda bb, i: (bb, i, 0)
    whole = lambda bb, i: (bb, 0, 0)
    once = pl.Buffered(1)
    return pl.pallas_call(
        kern,
        grid=(b, nq),
        in_specs=[pl.BlockSpec((1, QW, D_A), qtile),
                  pl.BlockSpec((1, s, D_A), whole, pipeline_mode=once),
                  pl.BlockSpec((ng, D_A, GROUP * TILE), whole, pipeline_mode=once),
                  pl.BlockSpec((1, QW, 512), qtile),
                  pl.BlockSpec((1, s, LANES), whole, pipeline_mode=once),
                  pl.BlockSpec((N_IDX_HEADS, QW), lambda bb, i: (0, bb * nq + i)),
                  pl.BlockSpec(bias_tiles.shape, lambda bb, i: (0, 0, 0), pipeline_mode=once)],
        out_specs=pl.BlockSpec((1, QW, D_A), qtile),
        out_shape=jax.ShapeDtypeStruct((b, s, D_A), BF16),
        scratch_shapes=[pltpu.VMEM((nt, TILE, QW), jnp.int32),
                        pltpu.VMEM((nt, TILE, QW), jnp.int16),
                        pltpu.VMEM((nt, TILE, QW), jnp.int16),
                        pltpu.VMEM((nt, TILE, QW), jnp.int16),
                        pltpu.VMEM((N_IDX_HEADS // 2, 2 * QW, LANES), BF16),
                        pltpu.VMEM((H_A // 2, 2 * QW, LANES), BF16),
                        pltpu.VMEM((GROUP, TILE, QW), F32),
                        pltpu.VMEM((H_A, nt, TILE, QW), F32),
                        pltpu.VMEM((H_A, 8, QW), F32),
                        pltpu.VMEM((H_A, HEAD_DIM + ONES_ROWS, QW), F32)],
        compiler_params=pltpu.CompilerParams(
            dimension_semantics=("arbitrary", "arbitrary"), vmem_limit_bytes=ATTN_VMEM_LIMIT),
    )(q, k, vt, qi, ki, wit, bias_tiles)


def _t5_bucket(rel):
    nb = N_BUCKETS // 2
    max_exact = nb // 2
    ret = jnp.where(rel > 0, nb, 0)
    n = jnp.abs(rel)
    nf = jnp.maximum(n, 1).astype(F32)
    large = max_exact + (jnp.log(nf / max_exact) / math.log(MAX_DISTANCE / max_exact)
                         * (nb - max_exact)).astype(jnp.int32)
    large = jnp.minimum(large, nb - 1)
    return ret + jnp.where(n < max_exact, n, large)


def _bias_tiles(rel_bias):
    n = TILE + QW - 1
    n_off = N_BIAS - 1
    m = np.arange(n)
    q_minus_k = np.where(m < QW, m, m - n)
    first_key = (np.arange(n_off)[:, None] - N_FAR) * TILE
    rel = first_key - q_minus_k[None, :]
    u = rel_bias[_t5_bucket(jnp.asarray(rel, jnp.int32))].astype(F32)
    u = u.transpose(2, 0, 1)
    flat = jnp.tile(u, (1, 1, TILE))[:, :, :TILE * (n - 1)]
    tiles = flat.reshape(H_A, n_off, TILE, n - 1)[..., :QW]
    j = np.arange(TILE)[None, :, None]
    t = np.arange(QW)[None, None, :]
    admissible = (first_key[:, :, None] + j) // CHUNK <= t // CHUNK
    tiles = jnp.where(jnp.asarray(admissible)[None], tiles, NEG_BIG)
    tiles = jnp.concatenate([tiles, jnp.full((H_A, 1, TILE, QW), NEG_BIG, F32)], axis=1)
    return tiles.reshape(H_A * N_BIAS, TILE, QW)


CONV_HALO = 32


def _conv_kernel(cur_ref, halo_ref, w_ref, b_ref, g_ref, beta_ref, o_ref, u_scr, sh_scr, *, tt):
    i = pl.program_id(1)

    def glu(t):
        return t[:, :D_B] * _sigmoid(t[:, D_B:])

    u_scr[0:CONV_HALO, :] = jnp.where(i > 0, glu(halo_ref[0]), 0.0)
    u_scr[CONV_HALO:CONV_HALO + tt, :] = glu(cur_ref[0])
    rows = 64
    first = CONV_HALO - (CONV_WIDTH - 1)
    for r0 in range(0, tt, rows):
        acc = jnp.broadcast_to(b_ref[...], (rows, D_B))
        for res in range(8):
            offs = [first + j for j in range(CONV_WIDTH) if (first + j) % 8 == res]
            span = offs[-1] - offs[0] + rows
            sh_scr[0:span, :] = u_scr[r0 + offs[0]:r0 + offs[0] + span, :]
            for off in offs:
                acc = acc + w_ref[off - first:off - first + 1, :] * sh_scr[off - offs[0]:off - offs[0] + rows, :]
        mu = jnp.mean(acc, axis=-1, keepdims=True)
        cen = acc - mu
        var = jnp.mean(cen * cen, axis=-1, keepdims=True)
        y = cen * lax.rsqrt(var + EPS) * g_ref[...] + beta_ref[...]
        o_ref[0, r0:r0 + rows, :] = _silu(y).astype(BF16)


def _conv(conv_in, w, b, g, beta, tt):
    bsz, s, _ = conv_in.shape
    per = tt // CONV_HALO
    vec = lambda bb, i: (0, 0)
    return pl.pallas_call(
        functools.partial(_conv_kernel, tt=tt),
        grid=(bsz, s // tt),
        in_specs=[pl.BlockSpec((1, tt, 2 * D_B), lambda bb, i: (bb, i, 0)),
                  pl.BlockSpec((1, CONV_HALO, 2 * D_B),
                               lambda bb, i: (bb, jnp.maximum(i * per - 1, 0), 0)),
                  pl.BlockSpec((CONV_WIDTH, D_B), vec),
                  pl.BlockSpec((1, D_B), vec),
                  pl.BlockSpec((1, D_B), vec),
                  pl.BlockSpec((1, D_B), vec)],
        out_specs=pl.BlockSpec((1, tt, D_B), lambda bb, i: (bb, i, 0)),
        out_shape=jax.ShapeDtypeStruct((bsz, s, D_B), BF16),
        scratch_shapes=[pltpu.VMEM((CONV_HALO + tt, D_B), F32),
                        pltpu.VMEM((CONV_HALO + 64, D_B), F32)],
        compiler_params=pltpu.CompilerParams(
            dimension_semantics=("arbitrary", "arbitrary"), vmem_limit_bytes=VMEM_LIMIT),
    )(conv_in, conv_in, w, b, g, beta)


def _rwkv_kernel(rw_ref, mu_ref, lora_ref, w0_ref, a0_ref, kk_ref, ka_ref, rk_ref, lg_ref, lb_ref,
                 o_ref, s_scr, prev_scr):
    c = pl.program_id(1)
    n = CHUNK
    nb = RW_CHUNKS * CHUNK

    @pl.when(c == 0)
    def _():
        s_scr[...] = jnp.zeros_like(s_scr)
        prev_scr[...] = jnp.zeros_like(prev_scr)

    ps = rw_ref[0]
    row = lax.broadcasted_iota(jnp.int32, (nb, 1), 0)
    prev = jnp.where(row == 0, prev_scr[0:1, :], pltpu.roll(ps, 1, axis=0))
    prev_scr[0:1, :] = ps[nb - 1:nb, :]
    xs = ps + mu_ref[...] * (prev - ps)
    r = xs[:, 0:D_C]
    k = xs[:, D_C:2 * D_C]
    v = xs[:, 2 * D_C:3 * D_C]
    dn = xs[:, 3 * D_C:3 * D_C + 2 * D_LORA]
    lane64 = lax.broadcasted_iota(jnp.int32, (nb, 2 * D_LORA), 1)
    dn = jnp.where(lane64 < D_LORA, jnp.tanh(dn), dn)
    dn_hi, dn_lo = _split2(dn)
    lora_hi, lora_lo = _split2(lora_ref[...])
    pre = _bdot(dn_hi, lora_hi) + _bdot(dn_hi, lora_lo) + _bdot(dn_lo, lora_hi)
    z = -(w0_ref[...] + pre[:, :D_C])
    softplus = jnp.maximum(z, 0.0) + jnp.log(1.0 + jnp.exp(-jnp.abs(z)))
    logdec = -jnp.exp(-softplus - 0.5)
    a = _sigmoid(a0_ref[...] + pre[:, D_C:])
    kkraw = k * kk_ref[...]
    k2 = k * (1.0 + (a - 1.0) * ka_ref[...])
    bonus_pre = r * k2 * rk_ref[...]

    step_in_chunk = row & (n - 1)
    cum_all = logdec
    shift = 1
    while shift < n:
        cum_all = cum_all + jnp.where(step_in_chunk >= shift, pltpu.roll(cum_all, shift, axis=0), 0.0)
        shift *= 2
    p_in_all = jnp.exp(cum_all)
    p_inv_all = jnp.exp(-cum_all)
    p_ex_all = jnp.exp(cum_all - logdec)

    row_n = lax.broadcasted_iota(jnp.int32, (n, LANES), 0)
    lane_n = lax.broadcasted_iota(jnp.int32, (n, LANES), 1)
    lo = lane_n < HEAD_DIM
    col = lane_n & (HEAD_DIM - 1)
    incl = col <= row_n
    strict = col < row_n
    mask2 = jnp.concatenate([strict, incl], axis=0)
    r2 = lax.broadcasted_iota(jnp.int32, (2 * n, LANES), 0)
    l2 = lax.broadcasted_iota(jnp.int32, (2 * n, LANES), 1)
    same_head = (r2 // HEAD_DIM) == (l2 // HEAD_DIM)

    def bd(t):
        tb = t.astype(BF16)
        z = jnp.zeros_like(tb)
        return jnp.concatenate([jnp.where(lo, tb, z), jnp.where(lo, z, tb)], axis=0)

    def hsum(t):
        s_lo = jnp.sum(jnp.where(lo, t, 0.0), axis=-1, keepdims=True)
        s_hi = jnp.sum(jnp.where(lo, 0.0, t), axis=-1, keepdims=True)
        return jnp.where(lo, s_lo, s_hi)

    pairs = range(H_C // 2)
    chains = [(j, g) for j in range(RW_CHUNKS) for g in pairs]
    ids = range(len(chains))
    sl = lambda t, i: t[chains[i][0] * n:(chains[i][0] + 1) * n,
                        chains[i][1] * LANES:(chains[i][1] + 1) * LANES]
    v_p = [sl(v, i) for i in ids]
    p_in = [sl(p_in_all, i) for i in ids]
    kkn = []
    for i in ids:
        kr = sl(kkraw, i)
        kkn.append(kr * lax.rsqrt(jnp.maximum(hsum(kr * kr), 1e-12)))
    rt = [sl(r, i) * p_in[i] for i in ids]
    at = [kkn[i] * sl(p_ex_all, i) for i in ids]
    kt = [sl(k2, i) * sl(p_inv_all, i) for i in ids]
    bt = [kkn[i] * sl(a, i) * sl(p_inv_all, i) for i in ids]
    lhs = [jnp.concatenate([at[i], rt[i]], axis=0).astype(BF16) for i in ids]
    ak = [lax.dot_general(lhs[i], bd(kt[i]), NT_DIMS, preferred_element_type=F32) for i in ids]
    ab = [lax.dot_general(lhs[i], bd(bt[i]), NT_DIMS, preferred_element_type=F32) for i in ids]
    av = [_bdot(jnp.where(mask2, ak[i], 0.0), bd(v_p[i])) for i in ids]
    a_rb = [jnp.where(incl, ab[i][n:], 0.0) for i in ids]
    npow = [jnp.where(strict, -ab[i][:n], 0.0) for i in ids]
    tm = list(npow)
    for _ in range(5):
        both = [_bdot(jnp.concatenate([npow[i], tm[i]], axis=0), bd(npow[i])) for i in ids]
        npow = [both[i][:n] for i in ids]
        tm = [tm[i] + both[i][n:] for i in ids]
    tm = [tm[i] + _bdot(tm[i], bd(npow[i])) for i in ids]
    wt = [at[i] + _bdot(tm[i], bd(at[i])) for i in ids]
    uloc = [av[i][:n] + _bdot(tm[i], bd(av[i][:n])) for i in ids]
    m_t = [jnp.concatenate([kt[i], bt[i]], axis=0).T.astype(BF16) for i in ids]
    eye = r2 == l2
    p_col = [jnp.sum(jnp.where(eye, p_in[i][n - 1:n, :], 0.0), axis=-1, keepdims=True) for i in ids]
    bonus = [hsum(sl(bonus_pre, i)) * v_p[i] for i in ids]

    state = [s_scr[g] for g in pairs]
    for j in range(RW_CHUNKS):
        cid = [j * (H_C // 2) + g for g in pairs]
        ws = [_bdot(jnp.concatenate([wt[i], rt[i]], axis=0), state[g])
              for g, i in zip(pairs, cid)]
        u = [ws[g][:n] + uloc[i] for g, i in zip(pairs, cid)]
        o = [ws[g][n:] + av[i][n:] - _bdot(a_rb[i], bd(u[g])) for g, i in zip(pairs, cid)]
        for g, i in zip(pairs, cid):
            ncat = jnp.concatenate([v_p[i], -u[g]], axis=0).astype(BF16)
            upd = jnp.dot(m_t[i], ncat, preferred_element_type=F32)
            state[g] = (state[g] + jnp.where(same_head, upd, 0.0)) * p_col[i]
        for g, i in zip(pairs, cid):
            lanes = slice(g * LANES, (g + 1) * LANES)
            cen = o[g] - hsum(o[g]) * (1.0 / HEAD_DIM)
            var = hsum(cen * cen) * (1.0 / HEAD_DIM)
            y = cen * lax.rsqrt(var + LNX_EPS) * lg_ref[:, lanes] + lb_ref[:, lanes]
            o_ref[0, j * n:(j + 1) * n, lanes] = (y + bonus[i]).astype(BF16)
    for g in pairs:
        s_scr[g] = state[g]


def _rwkv(rw, mu, lora, w0, a0, kk, ka, rk, lg, lb):
    bsz, s, _ = rw.shape
    vec = lambda bb, i: (0, 0)
    return pl.pallas_call(
        _rwkv_kernel,
        grid=(bsz, s // (RW_CHUNKS * CHUNK)),
        in_specs=[pl.BlockSpec((1, RW_CHUNKS * CHUNK, RW_COLS), lambda bb, i: (bb, i, 0)),
                  pl.BlockSpec((1, RW_COLS), vec),
                  pl.BlockSpec((2 * D_LORA, 2 * D_C), vec)] +
                 [pl.BlockSpec((1, D_C), vec)] * 7,
        out_specs=pl.BlockSpec((1, RW_CHUNKS * CHUNK, D_C), lambda bb, i: (bb, i, 0)),
        out_shape=jax.ShapeDtypeStruct((bsz, s, D_C), BF16),
        scratch_shapes=[pltpu.VMEM((H_C // 2, 2 * HEAD_DIM, LANES), F32),
                        pltpu.VMEM((8, RW_COLS), F32)],
        compiler_params=pltpu.CompilerParams(
            dimension_semantics=("arbitrary", "arbitrary"), vmem_limit_bytes=VMEM_LIMIT),
    )(rw, mu, lora, w0, a0, kk, ka, rk, lg, lb)


def _outproj_kernel(x_ref, attn_ref, conv_ref, rw_ref, gates_ref, gm_ref, w_ref, o_ref):
    sg = _silu(gates_ref[...].astype(F32))
    ya = (attn_ref[...].astype(F32) * sg[:, 0:D_A]).astype(BF16)
    yb = (conv_ref[...].astype(F32) * sg[:, D_A:D_A + D_B]).astype(BF16)
    yc = (rw_ref[...].astype(F32) * sg[:, D_A + D_B:D_MIX]).astype(BF16)
    y = (jnp.dot(ya, w_ref[0:D_A, :], preferred_element_type=F32)
         + jnp.dot(yb, w_ref[D_A:D_A + D_B, :], preferred_element_type=F32)
         + jnp.dot(yc, w_ref[D_A + D_B:D_MIX, :], preferred_element_type=F32))
    o_ref[...] = x_ref[...] + gm_ref[0] * y


def _outproj(x2, attn, conv, rw, gates, gate_mod, w, seq, tm):
    n, d = x2.shape
    per_b = seq // tm
    row = lambda i: (i, 0)
    return pl.pallas_call(
        _outproj_kernel,
        grid=(n // tm,),
        in_specs=[pl.BlockSpec((tm, d), row),
                  pl.BlockSpec((tm, D_A), row),
                  pl.BlockSpec((tm, D_B), row),
                  pl.BlockSpec((tm, D_C), row),
                  pl.BlockSpec((tm, D_MIX), row),
                  pl.BlockSpec((1, 1, d), lambda i: (i // per_b, 0, 0)),
                  pl.BlockSpec((D_MIX, d), lambda i: (0, 0))],
        out_specs=pl.BlockSpec((tm, d), row),
        out_shape=jax.ShapeDtypeStruct((n, d), F32),
        compiler_params=pltpu.CompilerParams(
            dimension_semantics=("arbitrary",), vmem_limit_bytes=VMEM_LIMIT),
    )(x2, attn, conv, rw, gates, gate_mod, w)


def _reorder_w_in(w_in):
    depth, d, _ = w_in.shape
    a_cols = 4 * D_A + N_IDX_HEADS * D_IDX + D_IDX + N_IDX_HEADS
    b0 = a_cols
    c0 = a_cols + 3 * D_B
    c_shift = 3 * D_C + 2 * D_LORA
    zeros = lambda m: jnp.zeros((depth, d, m), w_in.dtype)
    qi0 = 4 * D_A
    ki0 = qi0 + N_IDX_HEADS * D_IDX
    wi0 = ki0 + D_IDX
    parts = [w_in[..., 0:3 * D_A],
             w_in[..., qi0:ki0],
             w_in[..., ki0:wi0], w_in[..., ki0:wi0],
             w_in[..., wi0:a_cols], zeros(LANES - N_IDX_HEADS),
             w_in[..., 3 * D_A:4 * D_A], w_in[..., b0 + 2 * D_B:b0 + 3 * D_B],
             w_in[..., c0 + c_shift:c0 + c_shift + D_C],
             w_in[..., b0:b0 + 2 * D_B],
             w_in[..., c0:c0 + c_shift], zeros(RW_COLS - c_shift)]
    w = jnp.concatenate(parts, axis=-1)
    assert w.shape[-1] == N_COLS
    return w.astype(BF16)


def kernel(x, c, norm_g, w_ada, b_ada, w_in, w_out, q_norm_g, k_norm_g, rel_bias, conv_w, conv_b,
           conv_ln_g, conv_ln_b, shift_mu, decay_w0, decay_up, iclr_a0, iclr_up, key_k, key_a,
           bonus_r_k, lnx_g, lnx_b):
    bsz, seq, d = x.shape
    depth = w_in.shape[0]
    assert seq % (GROUP * TILE) == 0 and d == w_out.shape[-1]
    topk = min(TOPK_MAX, seq // 4)
    tm = GROUP * TILE
    conv_tt = 512
    tm_out = 2 * tm if seq % (2 * tm) == 0 else tm

    mod = _modulation(c, w_ada, b_ada)
    shift = mod[:, :, None, 0:d]
    sc1p = 1.0 + mod[:, :, None, d:2 * d]
    gate_mod = mod[:, :, None, 2 * d:3 * d]

    w_all = _reorder_w_in(w_in)
    qg = jnp.tile(q_norm_g, (1, H_A))[:, None, :] * (HEAD_DIM ** -0.5)
    kg = jnp.tile(k_norm_g, (1, H_A))[:, None, :]
    bias_tiles = _bias_tiles(rel_bias)
    c_shift = 3 * D_C + 2 * D_LORA
    mu_pad = jnp.pad(shift_mu, ((0, 0), (0, RW_COLS - c_shift)))[:, None, :]
    zl = jnp.zeros((depth, D_LORA, D_C), F32)
    lora = jnp.concatenate([jnp.concatenate([decay_up, zl], axis=2),
                            jnp.concatenate([zl, iclr_up], axis=2)], axis=1)
    row = lambda t: t[:, None, :]
    layers = dict(
        shift=shift, sc1p=sc1p, gate_mod=gate_mod, norm_g=row(norm_g), w=w_all, qg=qg, kg=kg,
        w_out=w_out.astype(BF16), conv_w=conv_w, conv_b=row(conv_b), conv_g=row(conv_ln_g),
        conv_beta=row(conv_ln_b), mu=mu_pad, lora=lora, w0=row(decay_w0), a0=row(iclr_a0),
        kk=row(key_k), ka=row(key_a), rk=row(bonus_r_k.reshape(depth, D_C)), lg=row(lnx_g),
        lb=row(lnx_b))

    def layer(x2, p):
        q, k, vt, qi, ki, wit, gates, conv_in, rw_in = _inproj(
            x2, p["sc1p"], p["shift"], p["norm_g"], p["w"], p["qg"], p["kg"], seq, tm)
        r3 = lambda t: t.reshape(bsz, seq, t.shape[-1])
        attn = _attention(r3(q), r3(k), vt, r3(qi), r3(ki), wit, bias_tiles, topk)
        conv = _conv(r3(conv_in), p["conv_w"], p["conv_b"], p["conv_g"], p["conv_beta"], conv_tt)
        rwo = _rwkv(r3(rw_in), p["mu"], p["lora"], p["w0"], p["a0"], p["kk"], p["ka"], p["rk"],
                    p["lg"], p["lb"])
        x2 = _outproj(x2, attn.reshape(-1, D_A), conv.reshape(-1, D_B), rwo.reshape(-1, D_C),
                      gates, p["gate_mod"], p["w_out"], seq, tm_out)
        return x2

    x2 = x.reshape(bsz * seq, d)
    for l in range(depth):
        x2 = layer(x2, {name: t[l] for name, t in layers.items()})
    return x2.reshape(bsz, seq, d)
```
